```python
import jax
import jax.numpy as jnp
from jax import lax
import numpy as np


D_MODEL = 2048
BATCH = 4
SEQ = 4096
DEPTH = 2

CHUNK = 64
D_MIX = D_MODEL
D_MLSTM = D_MIX // 2
MLSTM_HEADS = 4
MLSTM_HEAD_DIM = D_MLSTM // MLSTM_HEADS
MLSTM_CONV = 4
D_ATTN = D_MIX - D_MLSTM
ATTN_HEADS = 8
ATTN_HEAD_DIM = D_ATTN // ATTN_HEADS
LEFT_CHUNKS = 8
BAND = (LEFT_CHUNKS + 1) * CHUNK
MAX_REL = 256
CONV_WIDTH = 31
D_FF = 4 * D_MODEL
N_EVEN = (DEPTH + 1) // 2
N_ODD = DEPTH // 2
EPS = 1e-6
D_IN_PROJ = 4 * D_MLSTM + 2 * MLSTM_HEADS + 3 * D_ATTN
SPLIT_POINTS = (2 * D_MLSTM, 3 * D_MLSTM, 4 * D_MLSTM, 4 * D_MLSTM + MLSTM_HEADS,
                4 * D_MLSTM + 2 * MLSTM_HEADS, 4 * D_MLSTM + 2 * MLSTM_HEADS + D_ATTN,
                4 * D_MLSTM + 2 * MLSTM_HEADS + 2 * D_ATTN)

kernel_name = 'hybrid_mlstm_chunkattn_conformer'


def rmsnorm(x, g):
    xf = x.astype(jnp.float32)
    y = xf * lax.rsqrt(jnp.mean(xf * xf, axis=-1, keepdims=True) + EPS)
    return (y * g.astype(jnp.float32)).astype(x.dtype)


def layernorm(x, g, b):
    xf = x.astype(jnp.float32)
    mu = jnp.mean(xf, axis=-1, keepdims=True)
    var = jnp.mean(jnp.square(xf - mu), axis=-1, keepdims=True)
    y = (xf - mu) * lax.rsqrt(var + EPS)
    return (y * g.astype(jnp.float32) + b.astype(jnp.float32)).astype(x.dtype)


def causal_dwconv(x, w, b):
    K, C = w.shape
    y = lax.conv_general_dilated(x, w[:, None, :].astype(x.dtype), window_strides=(1,),
                                 padding=[(K - 1, 0)],
                                 dimension_numbers=('NWC', 'WIO', 'NWC'),
                                 feature_group_count=C)
    return y + b.astype(x.dtype)


def mlstm_chunkwise(q, k, v, i_pre, f_pre):
    B_, S, H, Dk = q.shape
    Dv = v.shape[-1]
    NC = S // CHUNK
    f32 = jnp.float32

    def chunks(t):
        t = t.astype(f32).reshape((B_, NC, CHUNK, H) + t.shape[3:])
        return jnp.moveaxis(t, (1, 3), (0, 2))

    qc = chunks(q)
    kc = chunks(k) * (Dk ** -0.5)
    vc = chunks(v)
    ic = chunks(i_pre)
    lfc = chunks(jax.nn.log_sigmoid(f_pre.astype(f32)))
    tri = jnp.tril(jnp.ones((CHUNK, CHUNK), dtype=bool))

    def step(carry, xs):
        C, n, m = carry
        qt, kt, vt, it, lft = xs
        b = jnp.cumsum(lft, axis=-1)
        g = b[..., -1]
        dlog = jnp.where(tri, b[..., :, None] - b[..., None, :] + it[..., None, :], -jnp.inf)
        inter_log = b + m[..., None]
        m_t = jnp.maximum(inter_log, jnp.max(dlog, axis=-1))
        w_intra = jnp.exp(dlog - m_t[..., None])
        w_inter = jnp.exp(inter_log - m_t)
        s = jnp.einsum('bhtk,bhsk->bhts', qt, kt) * w_intra
        num = (jnp.einsum('bhts,bhsv->bhtv', s, vt)
               + w_inter[..., None] * jnp.einsum('bhtk,bhvk->bhtv', qt, C))
        den = jnp.sum(s, axis=-1) + w_inter * jnp.einsum('bhtk,bhk->bht', qt, n)
        h = num / jnp.maximum(jnp.abs(den), jnp.exp(-m_t))[..., None]
        a = g[..., None] - b + it
        m_new = jnp.maximum(g + m, jnp.max(a, axis=-1))
        decay = jnp.exp(g + m - m_new)
        w_state = jnp.exp(a - m_new[..., None])
        C_new = decay[..., None, None] * C + jnp.einsum('bhsv,bhsk->bhvk', vt * w_state[..., None], kt)
        n_new = decay[..., None] * n + jnp.einsum('bhs,bhsk->bhk', w_state, kt)
        return (C_new, n_new, m_new), h

    init = (jnp.zeros((B_, H, Dv, Dk), f32), jnp.zeros((B_, H, Dk), f32), jnp.zeros((B_, H), f32))
    _, hs = lax.scan(step, init, (qc, kc, vc, ic, lfc))
    return jnp.moveaxis(hs, (0, 2), (1, 3)).reshape(B_, S, H, Dv)


def chunked_rel_attention(q, k, v, rel_bias):
    B_, S, H, Dh = q.shape
    NC = S // CHUNK
    band = jnp.arange(NC)[:, None] + jnp.arange(LEFT_CHUNKS + 1)[None, :]

    def band_gather(t):
        t = t.reshape(B_, NC, CHUNK, H, Dh)
        t = jnp.pad(t, ((0, 0), (LEFT_CHUNKS, 0), (0, 0), (0, 0), (0, 0)))
        return t[:, band].reshape(B_, NC, BAND, H, Dh)

    qc = q.reshape(B_, NC, CHUNK, H, Dh)
    kb = band_gather(k)
    vb = band_gather(v)
    scores = jnp.einsum('bnqhd,bnkhd->bhnqk', qc, kb).astype(jnp.float32) * (Dh ** -0.5)
    q_off = jnp.arange(CHUNK)
    k_off = ((jnp.arange(LEFT_CHUNKS + 1) - LEFT_CHUNKS)[:, None] * CHUNK
             + jnp.arange(CHUNK)[None, :]).reshape(-1)
    dist = jnp.clip(q_off[:, None] - k_off[None, :], -MAX_REL, MAX_REL) + MAX_REL
    bias = rel_bias.astype(jnp.float32)[:, dist]
    key_valid = jnp.repeat(band >= LEFT_CHUNKS, CHUNK, axis=1)
    scores = jnp.where(key_valid[None, None, :, None, :], scores + bias[:, None], -jnp.inf)
    p = jax.nn.softmax(scores, axis=-1).astype(v.dtype)
    out = jnp.einsum('bhnqk,bnkhd->bnqhd', p, vb)
    return out.reshape(B_, S, H * Dh)


def mlstm_attn_mixer(u, w_in, qk_conv_w, qk_conv_b, igate_b, fgate_b, mlstm_norm_g, rel_bias, w_out):
    B_, S, _ = u.shape
    proj = u @ w_in
    qk_a, v_a, o_a, i_a, f_a, q_b, k_b, v_b = jnp.split(proj, SPLIT_POINTS, axis=-1)
    qk_a = jax.nn.silu(causal_dwconv(qk_a, qk_conv_w, qk_conv_b))
    q_a, k_a = jnp.split(qk_a, 2, axis=-1)
    ha = lambda t: t.reshape(B_, S, MLSTM_HEADS, MLSTM_HEAD_DIM)
    h_a = mlstm_chunkwise(ha(q_a), ha(k_a), ha(v_a),
                          i_a.astype(jnp.float32) + igate_b.astype(jnp.float32),
                          f_a.astype(jnp.float32) + fgate_b.astype(jnp.float32))
    h_a = h_a * lax.rsqrt(jnp.mean(h_a * h_a, axis=-1, keepdims=True) + EPS)
    h_a = h_a * mlstm_norm_g.astype(jnp.float32).reshape(MLSTM_HEADS, MLSTM_HEAD_DIM)
    h_a = (jax.nn.sigmoid(o_a.astype(jnp.float32)) * h_a.reshape(B_, S, D_MLSTM)).astype(u.dtype)
    hb = lambda t: t.reshape(B_, S, ATTN_HEADS, ATTN_HEAD_DIM)
    h_b = chunked_rel_attention(hb(q_b), hb(k_b), hb(v_b), rel_bias)
    return jnp.concatenate([h_a, h_b], axis=-1) @ w_out


def conformer_conv(u, pw1_w, pw1_b, dw_w, dw_b, ln_g, ln_b, pw2_w, pw2_b):
    a, g = jnp.split(u @ pw1_w + pw1_b, 2, axis=-1)
    z = a * jax.nn.sigmoid(g)
    z = causal_dwconv(z, dw_w, dw_b)
    z = jax.nn.silu(layernorm(z, ln_g, ln_b))
    return z @ pw2_w + pw2_b


def sq_relu_mlp(u, w1, w2):
    return jnp.square(jax.nn.relu(u @ w1)) @ w2


def setup_inputs(seed: int = 0) -> dict:
    key = jax.random.key(seed)
    ks = jax.random.split(key, 24)
    D = D_MODEL

    def nrm(k, shape, scale):
        return jax.random.normal(k, shape, jnp.float32) * scale

    return {
        'x': nrm(ks[0], (BATCH, SEQ, D), 1.0),
        'mixer_norm_g': 1.0 + nrm(ks[1], (DEPTH, D), 0.02),
        'mix_w_in': nrm(ks[2], (N_EVEN, D, D_IN_PROJ), D ** -0.5),
        'qk_conv_w': nrm(ks[3], (N_EVEN, MLSTM_CONV, 2 * D_MLSTM), MLSTM_CONV ** -0.5),
        'qk_conv_b': nrm(ks[4], (N_EVEN, 2 * D_MLSTM), 0.02),
        'igate_b': nrm(ks[5], (N_EVEN, MLSTM_HEADS), 0.1),
        'fgate_b': jnp.linspace(3.0, 6.0, MLSTM_HEADS, dtype=jnp.float32)[None, :]
                   + nrm(ks[6], (N_EVEN, MLSTM_HEADS), 0.1),
        'mlstm_norm_g': 1.0 + nrm(ks[7], (N_EVEN, D_MLSTM), 0.02),
        'rel_bias': nrm(ks[8], (N_EVEN, ATTN_HEADS, 2 * MAX_REL + 1), 0.1),
        'mix_w_out': nrm(ks[9], (N_EVEN, D_MIX, D), D_MIX ** -0.5),
        'conv_pw1_w': nrm(ks[10], (N_ODD, D, 2 * D), D ** -0.5),
        'conv_pw1_b': nrm(ks[11], (N_ODD, 2 * D), 0.02),
        'conv_dw_w': nrm(ks[12], (N_ODD, CONV_WIDTH, D), CONV_WIDTH ** -0.5),
        'conv_dw_b': nrm(ks[13], (N_ODD, D), 0.02),
        'conv_ln_g': 1.0 + nrm(ks[14], (N_ODD, D), 0.02),
        'conv_ln_b': nrm(ks[15], (N_ODD, D), 0.02),
        'conv_pw2_w': nrm(ks[16], (N_ODD, D, D), D ** -0.5),
        'conv_pw2_b': nrm(ks[17], (N_ODD, D), 0.02),
        'mlp_norm_g': 1.0 + nrm(ks[18], (DEPTH, D), 0.02),
        'mlp_w1': nrm(ks[19], (DEPTH, D, D_FF), D ** -0.5),
        'mlp_w2': nrm(ks[20], (DEPTH, D_FF, D), D_FF ** -0.5),
        'final_norm_g': 1.0 + nrm(ks[21], (D,), 0.02),
    }


def reference(x, mixer_norm_g, mix_w_in, qk_conv_w, qk_conv_b, igate_b, fgate_b, mlstm_norm_g,
              rel_bias, mix_w_out, conv_pw1_w, conv_pw1_b, conv_dw_w, conv_dw_b, conv_ln_g,
              conv_ln_b, conv_pw2_w, conv_pw2_b, mlp_norm_g, mlp_w1, mlp_w2, final_norm_g):
    h = x
    for layer in range(DEPTH):
        u = rmsnorm(h, mixer_norm_g[layer])
        if layer % 2 == 0:
            e = layer // 2
            h = h + mlstm_attn_mixer(u, mix_w_in[e], qk_conv_w[e], qk_conv_b[e], igate_b[e],
                                     fgate_b[e], mlstm_norm_g[e], rel_bias[e], mix_w_out[e])
        else:
            o = layer // 2
            h = h + conformer_conv(u, conv_pw1_w[o], conv_pw1_b[o], conv_dw_w[o], conv_dw_b[o],
                                   conv_ln_g[o], conv_ln_b[o], conv_pw2_w[o], conv_pw2_b[o])
        u = rmsnorm(h, mlp_norm_g[layer])
        h = h + sq_relu_mlp(u, mlp_w1[layer], mlp_w2[layer])
    return rmsnorm(h, final_norm_g)
```

```python
import functools

import jax
import jax.numpy as jnp
from jax import lax
from jax.experimental import pallas as pl
from jax.experimental.pallas import tpu as pltpu

CHUNK = 64
MLSTM_HEADS = 4
MLSTM_CONV = 4
ATTN_HEADS = 8
LEFT_CHUNKS = 8
MAX_REL = 256
CONV_WIDTH = 31
EPS = 1e-6

V7X_VMEM_BYTES = 64 * 1024 * 1024
LANES = 128
BF16_SUBLANES = 16

F32 = jnp.float32
BF16 = jnp.bfloat16
NEG_INF = float("-inf")

_NT = (((1,), (1,)), ((), ()))
_TN = (((0,), (0,)), ((), ()))


def _params(semantics, vmem_mib):
    assert vmem_mib * 1024 * 1024 <= V7X_VMEM_BYTES
    return pltpu.CompilerParams(dimension_semantics=semantics,
                                vmem_limit_bytes=vmem_mib * 1024 * 1024)


def _rmsnorm_f32(x, g):
    return x * lax.rsqrt(jnp.mean(x * x, axis=-1, keepdims=True) + EPS) * g


def _sigmoid(x):
    return 1.0 / (1.0 + jnp.exp(-x))


def _in_proj_kernel(x_ref, g_ref, w_ref, wg_ref, o_ref, og_ref, u_ref):
    @pl.when(pl.program_id(1) == 0)
    def _():
        u = _rmsnorm_f32(x_ref[...], g_ref[...]).astype(BF16)
        u_ref[...] = u
        og_ref[...] = jnp.dot(u, wg_ref[...], preferred_element_type=F32)

    o_ref[...] = jnp.dot(u_ref[...], w_ref[...], preferred_element_type=F32).astype(o_ref.dtype)


def _in_proj(x2, g, w_main, w_gate, tm, tn):
    t, d = x2.shape
    n = w_main.shape[1]
    return pl.pallas_call(
        _in_proj_kernel,
        out_shape=(jax.ShapeDtypeStruct((t, n), BF16), jax.ShapeDtypeStruct((t, LANES), F32)),
        grid=(t // tm, n // tn),
        in_specs=[
            pl.BlockSpec((tm, d), lambda i, j: (i, 0)),
            pl.BlockSpec((1, d), lambda i, j: (0, 0)),
            pl.BlockSpec((d, tn), lambda i, j: (0, j)),
            pl.BlockSpec((d, LANES), lambda i, j: (0, 0)),
        ],
        out_specs=(pl.BlockSpec((tm, tn), lambda i, j: (i, j)),
                   pl.BlockSpec((tm, LANES), lambda i, j: (i, 0))),
        scratch_shapes=[pltpu.VMEM((tm, d), BF16)],
        compiler_params=_params(("arbitrary", "arbitrary"), 48),
        name="in_proj",
    )(x2, g, w_main, w_gate)


def _lane_prefix(x, op, ident, lane):
    shift = 1
    while shift < CHUNK:
        y = pltpu.roll(x, shift, axis=1)
        x = op(x, jnp.where(lane >= shift, y, ident))
        shift *= 2
    return x


def _gate_prep_kernel(gi_ref, gf_ref, ib_ref, fb_ref,
                      r_ref, mt_ref, wi_ref, et_ref, ws_ref, dc_ref,
                      g_sc, c_sc, m_sc, *, n_chunks, n_bh):
    shape = gi_ref.shape
    lane = lax.broadcasted_iota(jnp.int32, shape, 1)
    valid = lane < CHUNK
    i_pre = gi_ref[...] + ib_ref[...]
    f_pre = gf_ref[...] + fb_ref[...]
    lf = jnp.minimum(f_pre, 0.0) - jnp.log1p(jnp.exp(-jnp.abs(f_pre)))
    lf = jnp.where(valid, lf, 0.0)
    b = _lane_prefix(lf, jnp.add, 0.0, lane)
    r = jnp.where(valid, i_pre - b, NEG_INF)
    cm = _lane_prefix(r, jnp.maximum, NEG_INF, lane)
    g_tot = jnp.sum(jnp.where(lane == CHUNK - 1, b, 0.0), axis=1, keepdims=True)
    c_last = jnp.max(r, axis=1, keepdims=True)
    g_sc[...] = jnp.broadcast_to(g_tot, shape)
    c_sc[...] = jnp.broadcast_to(c_last, shape)

    def body(c, m):
        rows = pl.ds(pl.multiple_of(c * n_bh, n_bh), n_bh)
        m_sc[rows, :] = m
        return g_sc[rows, :] + jnp.maximum(m, c_sc[rows, :])

    lax.fori_loop(0, n_chunks, body, jnp.zeros((n_bh, shape[1]), F32))
    m = m_sc[...]
    big_m = jnp.maximum(m, cm)
    m_last = jnp.maximum(m, c_sc[...])
    r_ref[...] = jnp.where(valid, r, 0.0)
    mt_ref[...] = big_m
    wi_ref[...] = jnp.exp(m - big_m)
    et_ref[...] = jnp.exp(-b - big_m)
    ws_ref[...] = jnp.exp(r - m_last)
    dc_ref[...] = jnp.exp(m - m_last)


def _gate_prep(gi, gf, ib, fb, n_chunks, n_bh):
    shape = jax.ShapeDtypeStruct(gi.shape, F32)
    return pl.pallas_call(
        functools.partial(_gate_prep_kernel, n_chunks=n_chunks, n_bh=n_bh),
        out_shape=(shape,) * 6,
        scratch_shapes=[pltpu.VMEM(gi.shape, F32)] * 3,
        compiler_params=_params(None, 32),
        name="gate_prep",
    )(gi, gf, ib, fb)


_QK_HALO = BF16_SUBLANES


def _mlstm_kernel(qk_ref, halo_ref, v_ref, o_ref, cw_ref, cb_ref, cols_ref, rows_ref, ng_ref,
                  out_ref, xs_ref, c_ref, n_ref, *, heads, dh):
    chunk_id = pl.program_id(1)
    dm = heads * dh

    @pl.when(chunk_id == 0)
    def _():
        c_ref[...] = jnp.zeros(c_ref.shape, F32)
        n_ref[...] = jnp.zeros(n_ref.shape, F32)

    xs_ref[0:_QK_HALO, :] = jnp.where(chunk_id == 0, 0.0, halo_ref[...].astype(F32))
    xs_ref[_QK_HALO:, :] = qk_ref[...].astype(F32)

    def conv_silu(col0):
        acc = cb_ref[:, col0:col0 + dh]
        for j in range(MLSTM_CONV):
            off = _QK_HALO - (MLSTM_CONV - 1) + j
            acc = acc + cw_ref[j:j + 1, col0:col0 + dh] * xs_ref[off:off + CHUNK, col0:col0 + dh]
        return acc * _sigmoid(acc)

    tri = (lax.broadcasted_iota(jnp.int32, (CHUNK, CHUNK), 0)
           >= lax.broadcasted_iota(jnp.int32, (CHUNK, CHUNK), 1))
    for h in range(heads):
        hs = slice(h * dh, (h + 1) * dh)
        q = conv_silu(h * dh)
        k = conv_silu(dm + h * dh) * (dh ** -0.5)
        qb = q.astype(BF16)
        kb = k.astype(BF16)
        v = v_ref[:, hs]
        mt = cols_ref[:, h:h + 1]
        wi = cols_ref[:, heads + h:heads + h + 1]
        et = cols_ref[:, 2 * heads + h:2 * heads + h + 1]
        ws = cols_ref[:, 3 * heads + h:3 * heads + h + 1]
        r = rows_ref[h:h + 1, 0:CHUNK]
        dc = rows_ref[heads + h:heads + h + 1, 0:1]

        s = lax.dot_general(qb, kb, _NT, preferred_element_type=F32)
        sw = s * jnp.where(tri, jnp.exp(r - mt), 0.0)
        c_old = c_ref[h]
        inter = lax.dot_general(qb, c_old.astype(BF16), _NT, preferred_element_type=F32)
        intra = jnp.dot(sw.astype(BF16), v, preferred_element_type=F32)
        n_old = n_ref[h:h + 1, :]
        num = intra + wi * inter
        den = (jnp.sum(sw, axis=-1, keepdims=True)
               + wi * jnp.sum(q * n_old, axis=-1, keepdims=True))
        hh = num / jnp.maximum(jnp.abs(den), et)
        hn = hh * lax.rsqrt(jnp.mean(hh * hh, axis=-1, keepdims=True) + EPS) * ng_ref[:, hs]
        out_ref[:, hs] = (_sigmoid(o_ref[:, hs].astype(F32)) * hn).astype(out_ref.dtype)

        vw = (v.astype(F32) * ws).astype(BF16)
        upd = lax.dot_general(vw, kb, _TN, preferred_element_type=F32)
        c_ref[h] = dc * c_old + upd
        n_ref[h:h + 1, :] = dc * n_old + jnp.sum(k * ws, axis=0, keepdims=True)


def _mlstm(proj, conv_w, conv_b, cols, rows, norm_g, batch, n_chunks):
    t = proj.shape[0]
    dm = norm_g.shape[1]
    dh = dm // MLSTM_HEADS
    halo_blocks = CHUNK // _QK_HALO
    return pl.pallas_call(
        functools.partial(_mlstm_kernel, heads=MLSTM_HEADS, dh=dh),
        out_shape=jax.ShapeDtypeStruct((t, dm), BF16),
        grid=(batch, n_chunks),
        in_specs=[
            pl.BlockSpec((CHUNK, 2 * dm), lambda b, c: (b * n_chunks + c, 0)),
            pl.BlockSpec((_QK_HALO, 2 * dm),
                         lambda b, c: (jnp.maximum((b * n_chunks + c) * halo_blocks - 1, 0), 0)),
            pl.BlockSpec((CHUNK, dm), lambda b, c: (b * n_chunks + c, 2)),
            pl.BlockSpec((CHUNK, dm), lambda b, c: (b * n_chunks + c, 3)),
            pl.BlockSpec((MLSTM_CONV, 2 * dm), lambda b, c: (0, 0)),
            pl.BlockSpec((1, 2 * dm), lambda b, c: (0, 0)),
            pl.BlockSpec((None, None, CHUNK, LANES), lambda b, c: (b, c, 0, 0)),
            pl.BlockSpec((None, None, 2 * MLSTM_HEADS, LANES), lambda b, c: (b, c, 0, 0)),
            pl.BlockSpec((1, dm), lambda b, c: (0, 0)),
        ],
        out_specs=pl.BlockSpec((CHUNK, dm), lambda b, c: (b * n_chunks + c, 0)),
        scratch_shapes=[
            pltpu.VMEM((_QK_HALO + CHUNK, 2 * dm), F32),
            pltpu.VMEM((MLSTM_HEADS, dh, dh), F32),
            pltpu.VMEM((2 * MLSTM_HEADS, dh), F32),
        ],
        compiler_params=_params(("arbitrary", "arbitrary"), 32),
        name="mlstm",
    )(proj, proj, proj, proj, conv_w, conv_b, cols, rows, norm_g)


def _attn_kernel(*refs, heads, dh, q_rows, n_pieces):
    q_ref = refs[0]
    kv_refs = refs[1:1 + n_pieces]
    tab_ref = refs[1 + n_pieces]
    out_ref = refs[2 + n_pieces]
    qi = pl.program_id(1)
    d_attn = heads * dh
    pad_rows = LEFT_CHUNKS * CHUNK
    scale = dh ** -0.5
    col = lax.broadcasted_iota(jnp.int32, (q_rows, q_rows), 1)
    for h in range(heads):
        hs = slice(h * dh, (h + 1) * dh)
        q = q_ref[:, hs]
        scores = []
        for p, kv_ref in enumerate(kv_refs):
            s = lax.dot_general(q, kv_ref[:, hs], _NT, preferred_element_type=F32) * scale
            s = s + tab_ref[h, :, p * q_rows:(p + 1) * q_rows]
            key_row = (qi + p) * q_rows + col
            scores.append(jnp.where(key_row >= pad_rows, s, NEG_INF))
        m = functools.reduce(jnp.maximum, [jnp.max(s, axis=-1, keepdims=True) for s in scores])
        acc = None
        denom = None
        for p, kv_ref in enumerate(kv_refs):
            e = jnp.exp(scores[p] - m)
            pv = jnp.dot(e.astype(BF16), kv_ref[:, d_attn + h * dh:d_attn + (h + 1) * dh],
                         preferred_element_type=F32)
            es = jnp.sum(e, axis=-1, keepdims=True)
            acc = pv if acc is None else acc + pv
            denom = es if denom is None else denom + es
        out_ref[:, hs] = (acc / denom).astype(out_ref.dtype)


def _attention(proj, kv_pad, tab, batch, seq, q_rows):
    t = proj.shape[0]
    d_attn = kv_pad.shape[2] // 2
    dh = d_attn // ATTN_HEADS
    n_q = seq // q_rows
    n_pieces = 1 + (LEFT_CHUNKS * CHUNK) // q_rows
    q_col_block = (proj.shape[1] - 3 * d_attn) // d_attn
    kv_specs = [pl.BlockSpec((None, q_rows, 2 * d_attn), functools.partial(lambda b, qi, p: (b, qi + p, 0), p=p))
                for p in range(n_pieces)]
    return pl.pallas_call(
        functools.partial(_attn_kernel, heads=ATTN_HEADS, dh=dh, q_rows=q_rows, n_pieces=n_pieces),
        out_shape=jax.ShapeDtypeStruct((t, d_attn), BF16),
        grid=(batch, n_q),
        in_specs=[pl.BlockSpec((q_rows, d_attn), lambda b, qi: (b * n_q + qi, q_col_block))]
        + kv_specs
        + [pl.BlockSpec(tab.shape, lambda b, qi: (0, 0, 0))],
        out_specs=pl.BlockSpec((q_rows, d_attn), lambda b, qi: (b * n_q + qi, 0)),
        compiler_params=_params(("arbitrary", "arbitrary"), 40),
        name="chunk_attn",
    )(proj, *([kv_pad] * n_pieces), tab)


def _out_proj_kernel(a_ref, b_ref, wa_ref, wb_ref, x_ref, o_ref):
    acc = jnp.dot(a_ref[...], wa_ref[...], preferred_element_type=F32)
    acc = acc + jnp.dot(b_ref[...], wb_ref[...], preferred_element_type=F32)
    o_ref[...] = x_ref[...] + acc


def _out_proj(h_a, h_b, w_a, w_b, x2, tm):
    t, d = x2.shape
    ka, kb = h_a.shape[1], h_b.shape[1]
    return pl.pallas_call(
        _out_proj_kernel,
        out_shape=jax.ShapeDtypeStruct((t, d), F32),
        grid=(t // tm,),
        in_specs=[
            pl.BlockSpec((tm, ka), lambda i: (i, 0)),
            pl.BlockSpec((tm, kb), lambda i: (i, 0)),
            pl.BlockSpec((ka, d), lambda i: (0, 0)),
            pl.BlockSpec((kb, d), lambda i: (0, 0)),
            pl.BlockSpec((tm, d), lambda i: (i, 0)),
        ],
        out_specs=pl.BlockSpec((tm, d), lambda i: (i, 0)),
        compiler_params=_params(("arbitrary",), 48),
        name="out_proj",
    )(h_a, h_b, w_a, w_b, x2)


def _mlp_kernel(h_ref, g_ref, w1_ref, w2_ref, fg_ref, o_ref, u_ref, acc_ref, *, final_norm):
    j = pl.program_id(1)

    @pl.when(j == 0)
    def _():
        u_ref[...] = _rmsnorm_f32(h_ref[...], g_ref[...]).astype(BF16)
        acc_ref[...] = jnp.zeros(acc_ref.shape, F32)

    a = jnp.dot(u_ref[...], w1_ref[...], preferred_element_type=F32)
    a = jnp.square(jnp.maximum(a, 0.0)).astype(BF16)
    acc_ref[...] += jnp.dot(a, w2_ref[...], preferred_element_type=F32)

    @pl.when(j == pl.num_programs(1) - 1)
    def _():
        out = h_ref[...] + acc_ref[...]
        if final_norm:
            out = _rmsnorm_f32(out, fg_ref[...])
        o_ref[...] = out


def _mlp(h2, g, w1, w2, final_g, final_norm, tm, tf):
    t, d = h2.shape
    f = w1.shape[1]
    return pl.pallas_call(
        functools.partial(_mlp_kernel, final_norm=final_norm),
        out_shape=jax.ShapeDtypeStruct((t, d), F32),
        grid=(t // tm, f // tf),
        in_specs=[
            pl.BlockSpec((tm, d), lambda i, j: (i, 0)),
            pl.BlockSpec((1, d), lambda i, j: (0, 0)),
            pl.BlockSpec((d, tf), lambda i, j: (0, j)),
            pl.BlockSpec((tf, d), lambda i, j: (j, 0)),
            pl.BlockSpec((1, d), lambda i, j: (0, 0)),
        ],
        out_specs=pl.BlockSpec((tm, d), lambda i, j: (i, 0)),
        scratch_shapes=[pltpu.VMEM((tm, d), BF16), pltpu.VMEM((tm, d), F32)],
        compiler_params=_params(("arbitrary", "arbitrary"), 56),
        name="mlp_final" if final_norm else "mlp",
    )(h2, g, w1, w2, final_g)


def _glu_kernel(h_ref, g_ref, wa_ref, wg_ref, ba_ref, bg_ref, o_ref, u_ref):
    @pl.when(pl.program_id(1) == 0)
    def _():
        u_ref[...] = _rmsnorm_f32(h_ref[...], g_ref[...]).astype(BF16)

    u = u_ref[...]
    a = jnp.dot(u, wa_ref[...], preferred_element_type=F32) + ba_ref[...]
    gate = jnp.dot(u, wg_ref[...], preferred_element_type=F32) + bg_ref[...]
    o_ref[...] = (a * _sigmoid(gate)).astype(o_ref.dtype)


def _glu(h2, g, pw1, pw1_b, tm, tn):
    t, d = h2.shape
    n = pw1.shape[1] // 2
    nb = n // tn
    return pl.pallas_call(
        _glu_kernel,
        out_shape=jax.ShapeDtypeStruct((t, n), BF16),
        grid=(t // tm, nb),
        in_specs=[
            pl.BlockSpec((tm, d), lambda i, j: (i, 0)),
            pl.BlockSpec((1, d), lambda i, j: (0, 0)),
            pl.BlockSpec((d, tn), lambda i, j: (0, j)),
            pl.BlockSpec((d, tn), lambda i, j: (0, j + nb)),
            pl.BlockSpec((1, tn), lambda i, j: (0, j)),
            pl.BlockSpec((1, tn), lambda i, j: (0, j + nb)),
        ],
        out_specs=pl.BlockSpec((tm, tn), lambda i, j: (i, j)),
        scratch_shapes=[pltpu.VMEM((tm, d), BF16)],
        compiler_params=_params(("arbitrary", "arbitrary"), 48),
        name="conv_glu",
    )(h2, g, pw1, pw1, pw1_b, pw1_b)


_DW_HALO = 2 * BF16_SUBLANES
_DW_ROWS = 32
_DW_COLS = 256


def _conv_tail_kernel(z_ref, halo_ref, dw_ref, dwb_ref, lng_ref, lnb_ref, w2_ref, b2_ref, h_ref,
                      o_ref, zs_ref, y_ref, *, tm, tiles_per_seq):
    i = pl.program_id(0)
    d = z_ref.shape[1]
    seq_start = (i % tiles_per_seq) == 0
    zs_ref[0:_DW_HALO, :] = jnp.where(seq_start, 0.0, halo_ref[...].astype(F32))
    zs_ref[_DW_HALO:, :] = z_ref[...].astype(F32)
    first_tap = _DW_HALO - (CONV_WIDTH - 1)

    def col_body(cb, carry):
        cols = pl.ds(pl.multiple_of(cb * _DW_COLS, _DW_COLS), _DW_COLS)
        for rb in range(tm // _DW_ROWS):
            r0 = rb * _DW_ROWS
            acc = jnp.broadcast_to(dwb_ref[:, cols], (_DW_ROWS, _DW_COLS))
            for k in range(CONV_WIDTH):
                acc = acc + dw_ref[k:k + 1, cols] * zs_ref[r0 + first_tap + k:r0 + first_tap + k + _DW_ROWS, cols]
            y_ref[r0:r0 + _DW_ROWS, cols] = acc
        return carry

    lax.fori_loop(0, d // _DW_COLS, col_body, 0)
    y = y_ref[...]
    mu = jnp.mean(y, axis=-1, keepdims=True)
    yc = y - mu
    var = jnp.mean(yc * yc, axis=-1, keepdims=True)
    yn = yc * lax.rsqrt(var + EPS) * lng_ref[...] + lnb_ref[...]
    a = (yn * _sigmoid(yn)).astype(BF16)
    o_ref[...] = h_ref[...] + jnp.dot(a, w2_ref[...], preferred_element_type=F32) + b2_ref[...]


def _conv_tail(z, dw_w, dw_b, ln_g, ln_b, pw2, pw2_b, h2, seq, tm):
    t, d = h2.shape
    halo_blocks = tm // _DW_HALO
    return pl.pallas_call(
        functools.partial(_conv_tail_kernel, tm=tm, tiles_per_seq=seq // tm),
        out_shape=jax.ShapeDtypeStruct((t, d), F32),
        grid=(t // tm,),
        in_specs=[
            pl.BlockSpec((tm, d), lambda i: (i, 0)),
            pl.BlockSpec((_DW_HALO, d), lambda i: (jnp.maximum(i * halo_blocks - 1, 0), 0)),
            pl.BlockSpec(dw_w.shape, lambda i: (0, 0)),
            pl.BlockSpec((1, d), lambda i: (0, 0)),
            pl.BlockSpec((1, d), lambda i: (0, 0)),
            pl.BlockSpec((1, d), lambda i: (0, 0)),
            pl.BlockSpec((d, d), lambda i: (0, 0)),
            pl.BlockSpec((1, d), lambda i: (0, 0)),
            pl.BlockSpec((tm, d), lambda i: (i, 0)),
        ],
        out_specs=pl.BlockSpec((tm, d), lambda i: (i, 0)),
        scratch_shapes=[pltpu.VMEM((_DW_HALO + tm, d), F32), pltpu.VMEM((tm, d), F32)],
        compiler_params=_params(("arbitrary",), 56),
        name="conv_tail",
    )(z, z, dw_w, dw_b, ln_g, ln_b, pw2, pw2_b, h2)


def _tiles(tokens, seq):
    def rows(want):
        return min(want, seq)

    return dict(
        in_proj_tm=rows(1024), in_proj_tn=1024,
        attn_q_rows=rows(256),
        out_proj_tm=rows(512),
        mlp_tm=rows(512), mlp_tf=1024,
        glu_tm=rows(1024), glu_tn=512,
        conv_tm=rows(512),
    )


def _row(v):
    return v.reshape(1, -1).astype(F32)


def kernel(x, mixer_norm_g, mix_w_in, qk_conv_w, qk_conv_b, igate_b, fgate_b, mlstm_norm_g, rel_bias,
           mix_w_out, conv_pw1_w, conv_pw1_b, conv_dw_w, conv_dw_b, conv_ln_g, conv_ln_b, conv_pw2_w,
           conv_pw2_b, mlp_norm_g, mlp_w1, mlp_w2, final_norm_g):
    batch, seq, d = x.shape
    tokens = batch * seq
    n_chunks = seq // CHUNK
    dm = d // 2
    da = d - dm
    heads = MLSTM_HEADS
    tl = _tiles(tokens, seq)
    x2 = x.reshape(tokens, d)

    w_in = mix_w_in[0]
    gate_lo = 4 * dm
    gate_hi = gate_lo + 2 * heads
    w_main = jnp.concatenate([w_in[:, :gate_lo], w_in[:, gate_hi:]], axis=1).astype(BF16)
    w_gate = jnp.pad(w_in[:, gate_lo:gate_hi], ((0, 0), (0, LANES - 2 * heads))).astype(BF16)
    proj, gates = _in_proj(x2, _row(mixer_norm_g[0]), w_main, w_gate, tl["in_proj_tm"], tl["in_proj_tn"])

    n_bh = batch * heads
    g8 = gates[:, :2 * heads].reshape(batch, n_chunks, CHUNK, 2, heads)
    g8 = jnp.transpose(g8, (3, 1, 0, 4, 2)).reshape(2, n_chunks * n_bh, CHUNK)
    g8 = jnp.pad(g8, ((0, 0), (0, 0), (0, LANES - CHUNK)))
    bias_rows = lambda bvec: jnp.broadcast_to(
        jnp.tile(bvec.astype(F32), batch * n_chunks)[:, None], (n_chunks * n_bh, LANES))
    r, mt, wi, et, ws, dc = _gate_prep(g8[0], g8[1], bias_rows(igate_b[0]), bias_rows(fgate_b[0]),
                                       n_chunks, n_bh)
    per_frame = jnp.stack([mt, wi, et, ws], axis=0)[:, :, :CHUNK]
    per_frame = per_frame.reshape(4, n_chunks, batch, heads, CHUNK)
    cols = jnp.transpose(per_frame, (2, 1, 4, 0, 3)).reshape(batch, n_chunks, CHUNK, 4 * heads)
    cols = jnp.pad(cols, ((0, 0), (0, 0), (0, 0), (0, LANES - 4 * heads)))
    rows = jnp.concatenate([r.reshape(n_chunks, batch, heads, LANES),
                            dc.reshape(n_chunks, batch, heads, LANES)], axis=2)
    rows = jnp.transpose(rows, (1, 0, 2, 3))

    h_a = _mlstm(proj, qk_conv_w[0].astype(F32), _row(qk_conv_b[0]), cols, rows,
                 _row(mlstm_norm_g[0]), batch, n_chunks)

    q_rows = tl["attn_q_rows"]
    pad_rows = LEFT_CHUNKS * CHUNK
    kv_pad = jnp.pad(proj.reshape(batch, seq, -1)[:, :, proj.shape[1] - 2 * da:],
                     ((0, 0), (pad_rows, 0), (0, 0)))
    qpos = jnp.arange(q_rows)[:, None]
    kpos = jnp.arange(q_rows + pad_rows)[None, :]
    dist = jnp.clip(pad_rows + qpos - kpos, -MAX_REL, MAX_REL) + MAX_REL
    in_band = (kpos // CHUNK >= qpos // CHUNK) & (kpos // CHUNK <= qpos // CHUNK + LEFT_CHUNKS)
    tab = jnp.where(in_band[None], rel_bias[0].astype(F32)[:, dist], NEG_INF)
    h_b = _attention(proj, kv_pad, tab, batch, seq, q_rows)

    w_out = mix_w_out[0].astype(BF16)
    h = _out_proj(h_a, h_b, w_out[:dm], w_out[dm:], x2, tl["out_proj_tm"])

    h = _mlp(h, _row(mlp_norm_g[0]), mlp_w1[0].astype(BF16), mlp_w2[0].astype(BF16),
             _row(final_norm_g), False, tl["mlp_tm"], tl["mlp_tf"])

    z = _glu(h, _row(mixer_norm_g[1]), conv_pw1_w[0].astype(BF16), _row(conv_pw1_b[0]),
             tl["glu_tm"], tl["glu_tn"])
    h = _conv_tail(z, conv_dw_w[0].astype(F32), _row(conv_dw_b[0]), _row(conv_ln_g[0]),
                   _row(conv_ln_b[0]), conv_pw2_w[0].astype(BF16), _row(conv_pw2_b[0]), h, seq,
                   tl["conv_tm"])

    h = _mlp(h, _row(mlp_norm_g[1]), mlp_w1[1].astype(BF16), mlp_w2[1].astype(BF16),
             _row(final_norm_g), True, tl["mlp_tm"], tl["mlp_tf"])
    return h.reshape(batch, seq, d)
```

```python
import functools

import jax
import jax.numpy as jnp
from jax import lax
from jax.experimental import pallas as pl
from jax.experimental.pallas import tpu as pltpu

CHUNK = 64
MLSTM_HEADS = 4
MLSTM_CONV = 4
ATTN_HEADS = 8
LEFT_CHUNKS = 8
MAX_REL = 256
CONV_WIDTH = 31
EPS = 1e-6

V7X_VMEM_BYTES = 64 * 1024 * 1024
LANES = 128
BF16_SUBLANES = 16

F32 = jnp.float32
BF16 = jnp.bfloat16
NEG_INF = float("-inf")

_NT = (((1,), (1,)), ((), ()))
_TN = (((0,), (0,)), ((), ()))


def _params(semantics, vmem_mib):
    assert vmem_mib * 1024 * 1024 <= V7X_VMEM_BYTES
    return pltpu.CompilerParams(dimension_semantics=semantics,
                                vmem_limit_bytes=vmem_mib * 1024 * 1024)


def _rmsnorm_f32(x, g):
    return x * lax.rsqrt(jnp.mean(x * x, axis=-1, keepdims=True) + EPS) * g


def _sigmoid(x):
    return 1.0 / (1.0 + jnp.exp(-x))


def _in_proj_kernel(x_ref, g_ref, w_ref, wg_ref, o_ref, og_ref, u_ref):
    @pl.when(pl.program_id(1) == 0)
    def _():
        u = _rmsnorm_f32(x_ref[...], g_ref[...]).astype(BF16)
        u_ref[...] = u
        og_ref[...] = jnp.dot(u, wg_ref[...], preferred_element_type=F32)

    o_ref[...] = jnp.dot(u_ref[...], w_ref[...], preferred_element_type=F32).astype(o_ref.dtype)


def _in_proj(x2, g, w_main, w_gate, tm, tn):
    t, d = x2.shape
    n = w_main.shape[1]
    return pl.pallas_call(
        _in_proj_kernel,
        out_shape=(jax.ShapeDtypeStruct((t, n), BF16), jax.ShapeDtypeStruct((t, LANES), F32)),
        grid=(t // tm, n // tn),
        in_specs=[
            pl.BlockSpec((tm, d), lambda i, j: (i, 0)),
            pl.BlockSpec((1, d), lambda i, j: (0, 0)),
            pl.BlockSpec((d, tn), lambda i, j: (0, j)),
            pl.BlockSpec((d, LANES), lambda i, j: (0, 0)),
        ],
        out_specs=(pl.BlockSpec((tm, tn), lambda i, j: (i, j)),
                   pl.BlockSpec((tm, LANES), lambda i, j: (i, 0))),
        scratch_shapes=[pltpu.VMEM((tm, d), BF16)],
        compiler_params=_params(("arbitrary", "arbitrary"), 48),
        name="in_proj",
    )(x2, g, w_main, w_gate)


def _lane_prefix(x, op, ident, lane):
    shift = 1
    while shift < CHUNK:
        y = pltpu.roll(x, shift, axis=1)
        x = op(x, jnp.where(lane >= shift, y, ident))
        shift *= 2
    return x


def _gate_prep_kernel(gi_ref, gf_ref, ib_ref, fb_ref,
                      r_ref, mt_ref, wi_ref, et_ref, ws_ref, dc_ref,
                      g_sc, c_sc, m_sc, *, n_chunks, n_bh):
    shape = gi_ref.shape
    lane = lax.broadcasted_iota(jnp.int32, shape, 1)
    valid = lane < CHUNK
    i_pre = gi_ref[...] + ib_ref[...]
    f_pre = gf_ref[...] + fb_ref[...]
    lf = jnp.minimum(f_pre, 0.0) - jnp.log1p(jnp.exp(-jnp.abs(f_pre)))
    lf = jnp.where(valid, lf, 0.0)
    b = _lane_prefix(lf, jnp.add, 0.0, lane)
    r = jnp.where(valid, i_pre - b, NEG_INF)
    cm = _lane_prefix(r, jnp.maximum, NEG_INF, lane)
    g_tot = jnp.sum(jnp.where(lane == CHUNK - 1, b, 0.0), axis=1, keepdims=True)
    c_last = jnp.max(r, axis=1, keepdims=True)
    g_sc[...] = jnp.broadcast_to(g_tot, shape)
    c_sc[...] = jnp.broadcast_to(c_last, shape)

    def body(c, m):
        rows = pl.ds(pl.multiple_of(c * n_bh, n_bh), n_bh)
        m_sc[rows, :] = m
        return g_sc[rows, :] + jnp.maximum(m, c_sc[rows, :])

    lax.fori_loop(0, n_chunks, body, jnp.zeros((n_bh, shape[1]), F32))
    m = m_sc[...]
    big_m = jnp.maximum(m, cm)
    m_last = jnp.maximum(m, c_sc[...])
    r_ref[...] = jnp.where(valid, r, 0.0)
    mt_ref[...] = big_m
    wi_ref[...] = jnp.exp(m - big_m)
    et_ref[...] = jnp.exp(-b - big_m)
    ws_ref[...] = jnp.exp(r - m_last)
    dc_ref[...] = jnp.exp(m - m_last)


def _gate_prep(gi, gf, ib, fb, n_chunks, n_bh):
    shape = jax.ShapeDtypeStruct(gi.shape, F32)
    return pl.pallas_call(
        functools.partial(_gate_prep_kernel, n_chunks=n_chunks, n_bh=n_bh),
        out_shape=(shape,) * 6,
        scratch_shapes=[pltpu.VMEM(gi.shape, F32)] * 3,
        compiler_params=_params(None, 32),
        name="gate_prep",
    )(gi, gf, ib, fb)


_QK_HALO = BF16_SUBLANES


def _mlstm_kernel(qk_ref, halo_ref, v_ref, o_ref, cw_ref, cb_ref, cols_ref, rows_ref, ng_ref,
                  out_ref, xs_ref, c_ref, n_ref, *, heads, dh, chunks_per_step):
    group_id = pl.program_id(1)
    dm = heads * dh
    n_slabs = 2 * dm // LANES

    @pl.when(group_id == 0)
    def _():
        c_ref[...] = jnp.zeros(c_ref.shape, F32)
        n_ref[...] = jnp.zeros(n_ref.shape, F32)

    for s in range(n_slabs):
        lanes = slice(s * LANES, (s + 1) * LANES)
        xs_ref[s, 0:_QK_HALO, :] = jnp.where(group_id == 0, 0.0, halo_ref[:, lanes].astype(F32))
        xs_ref[s, _QK_HALO:, :] = qk_ref[:, lanes].astype(F32)

    def conv_silu(col0, row0):
        parts = []
        for s in range(col0 // LANES, (col0 + dh) // LANES):
            lanes = slice(s * LANES, (s + 1) * LANES)
            acc = cb_ref[:, lanes]
            for j in range(MLSTM_CONV):
                off = row0 + _QK_HALO - (MLSTM_CONV - 1) + j
                acc = acc + cw_ref[j:j + 1, lanes] * xs_ref[s, off:off + CHUNK, :]
            parts.append(acc * _sigmoid(acc))
        return jnp.concatenate(parts, axis=1)

    tri = (lax.broadcasted_iota(jnp.int32, (CHUNK, CHUNK), 0)
           >= lax.broadcasted_iota(jnp.int32, (CHUNK, CHUNK), 1))
    eye = (lax.broadcasted_iota(jnp.int32, (dh, dh), 0)
           == lax.broadcasted_iota(jnp.int32, (dh, dh), 1)).astype(BF16)
    for cc in range(chunks_per_step):
        row0 = cc * CHUNK
        rs = slice(row0, row0 + CHUNK)
        for h in range(heads):
            hs = slice(h * dh, (h + 1) * dh)
            q = conv_silu(h * dh, row0)
            k = conv_silu(dm + h * dh, row0) * (dh ** -0.5)
            qb = q.astype(BF16)
            kb = k.astype(BF16)
            v = v_ref[rs, hs]
            mt = cols_ref[cc, :, h:h + 1]
            wi = cols_ref[cc, :, heads + h:heads + h + 1]
            et = cols_ref[cc, :, 2 * heads + h:2 * heads + h + 1]
            ws = cols_ref[cc, :, 3 * heads + h:3 * heads + h + 1]
            r = rows_ref[cc, h:h + 1, 0:CHUNK]
            dc = rows_ref[cc, heads + h:heads + h + 1, 0:1]

            s = lax.dot_general(qb, kb, _NT, preferred_element_type=F32)
            sw = s * jnp.where(tri, jnp.exp(r - mt), 0.0)
            c_old = c_ref[h]
            inter = lax.dot_general(qb, c_old.astype(BF16), _NT, preferred_element_type=F32)
            intra = jnp.dot(sw.astype(BF16), v, preferred_element_type=F32)
            n_old = n_ref[h:h + 1, :]
            num = intra + wi * inter
            den = (jnp.sum(sw, axis=-1, keepdims=True)
                   + wi * jnp.sum(q * n_old, axis=-1, keepdims=True))
            hh = num / jnp.maximum(jnp.abs(den), et)
            hn = hh * lax.rsqrt(jnp.mean(hh * hh, axis=-1, keepdims=True) + EPS) * ng_ref[:, hs]
            out_ref[rs, hs] = (_sigmoid(o_ref[rs, hs].astype(F32)) * hn).astype(out_ref.dtype)

            vw = (v.astype(F32) * ws).astype(BF16)
            vw_t = lax.dot_general(eye, vw, _NT, preferred_element_type=F32).astype(BF16)
            upd = jnp.dot(vw_t, kb, preferred_element_type=F32)
            c_ref[h] = dc * c_old + upd
            n_ref[h:h + 1, :] = dc * n_old + jnp.sum(k * ws, axis=0, keepdims=True)


def _mlstm(proj, conv_w, conv_b, cols, rows, norm_g, batch, n_chunks, chunks_per_step):
    t = proj.shape[0]
    dm = norm_g.shape[1]
    dh = dm // MLSTM_HEADS
    n_groups = n_chunks // chunks_per_step
    rows_per_step = chunks_per_step * CHUNK
    halo_blocks = rows_per_step // _QK_HALO
    return pl.pallas_call(
        functools.partial(_mlstm_kernel, heads=MLSTM_HEADS, dh=dh, chunks_per_step=chunks_per_step),
        out_shape=jax.ShapeDtypeStruct((t, dm), BF16),
        grid=(batch, n_groups),
        in_specs=[
            pl.BlockSpec((rows_per_step, 2 * dm), lambda b, c: (b * n_groups + c, 0)),
            pl.BlockSpec((_QK_HALO, 2 * dm),
                         lambda b, c: (jnp.maximum((b * n_groups + c) * halo_blocks - 1, 0), 0)),
            pl.BlockSpec((rows_per_step, dm), lambda b, c: (b * n_groups + c, 2)),
            pl.BlockSpec((rows_per_step, dm), lambda b, c: (b * n_groups + c, 3)),
            pl.BlockSpec((MLSTM_CONV, 2 * dm), lambda b, c: (0, 0)),
            pl.BlockSpec((1, 2 * dm), lambda b, c: (0, 0)),
            pl.BlockSpec((None, chunks_per_step, CHUNK, LANES), lambda b, c: (b, c, 0, 0)),
            pl.BlockSpec((None, chunks_per_step, 2 * MLSTM_HEADS, LANES), lambda b, c: (b, c, 0, 0)),
            pl.BlockSpec((1, dm), lambda b, c: (0, 0)),
        ],
        out_specs=pl.BlockSpec((rows_per_step, dm), lambda b, c: (b * n_groups + c, 0)),
        scratch_shapes=[
            pltpu.VMEM((2 * dm // LANES, _QK_HALO + rows_per_step, LANES), F32),
            pltpu.VMEM((MLSTM_HEADS, dh, dh), F32),
            pltpu.VMEM((2 * MLSTM_HEADS, dh), F32),
        ],
        compiler_params=_params(("arbitrary", "arbitrary"), 32),
        name="mlstm",
    )(proj, proj, proj, proj, conv_w, conv_b, cols, rows, norm_g)


def _attn_kernel(*refs, heads, dh, q_rows, n_pieces):
    q_ref = refs[0]
    k_refs = refs[1:1 + n_pieces]
    v_refs = refs[1 + n_pieces:1 + 2 * n_pieces]
    tab_ref = refs[1 + 2 * n_pieces]
    out_ref = refs[2 + 2 * n_pieces]
    qi = pl.program_id(1)
    pad_rows = LEFT_CHUNKS * CHUNK
    scale = dh ** -0.5
    col = lax.broadcasted_iota(jnp.int32, (q_rows, q_rows), 1)
    for h in range(heads):
        hs = slice(h * dh, (h + 1) * dh)
        q = q_ref[:, hs]
        scores = []
        for p, k_ref in enumerate(k_refs):
            s = lax.dot_general(q, k_ref[:, hs], _NT, preferred_element_type=F32) * scale
            s = s + tab_ref[h, :, p * q_rows:(p + 1) * q_rows]
            window_row = (qi + p) * q_rows + col
            scores.append(jnp.where(window_row >= pad_rows, s, NEG_INF))
        m = functools.reduce(jnp.maximum, [jnp.max(s, axis=-1, keepdims=True) for s in scores])
        acc = None
        denom = None
        for p, v_ref in enumerate(v_refs):
            e = jnp.exp(scores[p] - m)
            pv = jnp.dot(e.astype(BF16), v_ref[:, hs], preferred_element_type=F32)
            es = jnp.sum(e, axis=-1, keepdims=True)
            acc = pv if acc is None else acc + pv
            denom = es if denom is None else denom + es
        out_ref[:, hs] = (acc / denom).astype(out_ref.dtype)


def _attention(proj, tab, d_attn, batch, seq, q_rows):
    t = proj.shape[0]
    dh = d_attn // ATTN_HEADS
    n_q = seq // q_rows
    pad_blocks = (LEFT_CHUNKS * CHUNK) // q_rows
    n_pieces = 1 + pad_blocks
    q_col = proj.shape[1] // d_attn - 3

    def window_spec(col_block, p):
        return pl.BlockSpec(
            (q_rows, d_attn),
            lambda b, qi: (b * n_q + jnp.maximum(qi + p - pad_blocks, 0), col_block))

    return pl.pallas_call(
        functools.partial(_attn_kernel, heads=ATTN_HEADS, dh=dh, q_rows=q_rows, n_pieces=n_pieces),
        out_shape=jax.ShapeDtypeStruct((t, d_attn), BF16),
        grid=(batch, n_q),
        in_specs=[pl.BlockSpec((q_rows, d_attn), lambda b, qi: (b * n_q + qi, q_col))]
        + [window_spec(q_col + 1, p) for p in range(n_pieces)]
        + [window_spec(q_col + 2, p) for p in range(n_pieces)]
        + [pl.BlockSpec(tab.shape, lambda b, qi: (0, 0, 0))],
        out_specs=pl.BlockSpec((q_rows, d_attn), lambda b, qi: (b * n_q + qi, 0)),
        compiler_params=_params(("arbitrary", "arbitrary"), 40),
        name="chunk_attn",
    )(proj, *([proj] * (2 * n_pieces)), tab)


def _bias_table(rel, q_rows):
    heads = rel.shape[0]
    pad_rows = LEFT_CHUNKS * CHUNK
    width = q_rows + pad_rows
    d_lo, d_hi = pad_rows - width + 1, pad_rows + q_rows - 1
    mid = rel[:, max(d_lo, -MAX_REL) + MAX_REL:min(d_hi, MAX_REL) + MAX_REL + 1]
    left = jnp.repeat(rel[:, :1], max(0, -MAX_REL - d_lo), axis=1)
    right = jnp.repeat(rel[:, -1:], max(0, d_hi - MAX_REL), axis=1)
    by_dist = jnp.concatenate([left, mid, right], axis=1)[:, ::-1]
    length = by_dist.shape[1]
    ring = jnp.pad(by_dist, ((0, 0), (0, 1)))
    skew = jnp.tile(ring, (1, q_rows))[:, :q_rows * length].reshape(heads, q_rows, length)
    bias = skew[:, :, q_rows - 1:q_rows - 1 + width]
    q_chunk = jnp.arange(q_rows)[:, None] // CHUNK
    k_chunk = jnp.arange(width)[None, :] // CHUNK
    in_band = (k_chunk >= q_chunk) & (k_chunk <= q_chunk + LEFT_CHUNKS)
    return jnp.where(in_band[None], bias, NEG_INF)


def _out_proj_kernel(a_ref, b_ref, wa_ref, wb_ref, x_ref, o_ref):
    acc = jnp.dot(a_ref[...], wa_ref[...], preferred_element_type=F32)
    acc = acc + jnp.dot(b_ref[...], wb_ref[...], preferred_element_type=F32)
    o_ref[...] = x_ref[...] + acc


def _out_proj(h_a, h_b, w, x2, tm):
    t, d = x2.shape
    ka, kb = h_a.shape[1], h_b.shape[1]
    assert ka == kb
    return pl.pallas_call(
        _out_proj_kernel,
        out_shape=jax.ShapeDtypeStruct((t, d), F32),
        grid=(t // tm,),
        in_specs=[
            pl.BlockSpec((tm, ka), lambda i: (i, 0)),
            pl.BlockSpec((tm, kb), lambda i: (i, 0)),
            pl.BlockSpec((None, ka, d), lambda i: (0, 0, 0)),
            pl.BlockSpec((None, kb, d), lambda i: (1, 0, 0)),
            pl.BlockSpec((tm, d), lambda i: (i, 0)),
        ],
        out_specs=pl.BlockSpec((tm, d), lambda i: (i, 0)),
        compiler_params=_params(("arbitrary",), 48),
        name="out_proj",
    )(h_a, h_b, w, w, x2)


def _mlp_kernel(h_ref, g_ref, w1_ref, w2_ref, fg_ref, o_ref, u_ref, acc_ref, *, final_norm):
    j = pl.program_id(1)

    @pl.when(j == 0)
    def _():
        u_ref[...] = _rmsnorm_f32(h_ref[...], g_ref[...]).astype(BF16)
        acc_ref[...] = jnp.zeros(acc_ref.shape, F32)

    a = jnp.dot(u_ref[...], w1_ref[...], preferred_element_type=F32)
    a = jnp.square(jnp.maximum(a, 0.0)).astype(BF16)
    acc_ref[...] += jnp.dot(a, w2_ref[...], preferred_element_type=F32)

    @pl.when(j == pl.num_programs(1) - 1)
    def _():
        out = h_ref[...] + acc_ref[...]
        if final_norm:
            out = _rmsnorm_f32(out, fg_ref[...])
        o_ref[...] = out


def _mlp(h2, g, w1, w2, layer, final_g, final_norm, tm, tf):
    t, d = h2.shape
    f = w1.shape[2]
    return pl.pallas_call(
        functools.partial(_mlp_kernel, final_norm=final_norm),
        out_shape=jax.ShapeDtypeStruct((t, d), F32),
        grid=(t // tm, f // tf),
        in_specs=[
            pl.BlockSpec((tm, d), lambda i, j: (i, 0)),
            pl.BlockSpec((1, d), lambda i, j: (0, 0)),
            pl.BlockSpec((None, d, tf), lambda i, j: (layer, 0, j)),
            pl.BlockSpec((None, tf, d), lambda i, j: (layer, j, 0)),
            pl.BlockSpec((1, d), lambda i, j: (0, 0)),
        ],
        out_specs=pl.BlockSpec((tm, d), lambda i, j: (i, 0)),
        scratch_shapes=[pltpu.VMEM((tm, d), BF16), pltpu.VMEM((tm, d), F32)],
        compiler_params=_params(("arbitrary", "arbitrary"), 56),
        name="mlp_final" if final_norm else "mlp",
    )(h2, g, w1, w2, final_g)


def _glu_kernel(h_ref, g_ref, wa_ref, wg_ref, ba_ref, bg_ref, o_ref, u_ref):
    @pl.when(pl.program_id(1) == 0)
    def _():
        u_ref[...] = _rmsnorm_f32(h_ref[...], g_ref[...]).astype(BF16)

    u = u_ref[...]
    a = jnp.dot(u, wa_ref[...], preferred_element_type=F32) + ba_ref[...]
    gate = jnp.dot(u, wg_ref[...], preferred_element_type=F32) + bg_ref[...]
    o_ref[...] = (a * _sigmoid(gate)).astype(o_ref.dtype)


def _glu(h2, g, pw1, pw1_b, tm, tn):
    t, d = h2.shape
    n = pw1.shape[2] // 2
    nb = n // tn
    return pl.pallas_call(
        _glu_kernel,
        out_shape=jax.ShapeDtypeStruct((t, n), BF16),
        grid=(t // tm, nb),
        in_specs=[
            pl.BlockSpec((tm, d), lambda i, j: (i, 0)),
            pl.BlockSpec((1, d), lambda i, j: (0, 0)),
            pl.BlockSpec((None, d, tn), lambda i, j: (0, 0, j)),
            pl.BlockSpec((None, d, tn), lambda i, j: (0, 0, j + nb)),
            pl.BlockSpec((1, tn), lambda i, j: (0, j)),
            pl.BlockSpec((1, tn), lambda i, j: (0, j + nb)),
        ],
        out_specs=pl.BlockSpec((tm, tn), lambda i, j: (i, j)),
        scratch_shapes=[pltpu.VMEM((tm, d), BF16)],
        compiler_params=_params(("arbitrary", "arbitrary"), 48),
        name="conv_glu",
    )(h2, g, pw1, pw1, pw1_b, pw1_b)


_DW_HALO = 2 * BF16_SUBLANES
_DW_ROWS = 64


def _conv_tail_kernel(z_ref, halo_ref, dw_ref, dwb_ref, lng_ref, lnb_ref, w2_ref, b2_ref, h_ref,
                      o_ref, zs_ref, y_ref, *, tm, tiles_per_seq):
    i = pl.program_id(0)
    n_slabs = z_ref.shape[1] // LANES
    seq_start = (i % tiles_per_seq) == 0
    for s in range(n_slabs):
        lanes = slice(s * LANES, (s + 1) * LANES)
        zs_ref[s, 0:_DW_HALO, :] = jnp.where(seq_start, 0.0, halo_ref[:, lanes].astype(F32))
        zs_ref[s, _DW_HALO:, :] = z_ref[:, lanes].astype(F32)
    first_tap = _DW_HALO - (CONV_WIDTH - 1)

    def slab_body(s, carry):
        lanes = pl.ds(pl.multiple_of(s * LANES, LANES), LANES)
        for rb in range(tm // _DW_ROWS):
            r0 = rb * _DW_ROWS
            acc = jnp.broadcast_to(dwb_ref[:, lanes], (_DW_ROWS, LANES))
            for k in range(CONV_WIDTH):
                acc = acc + dw_ref[k:k + 1, lanes] * zs_ref[s, r0 + first_tap + k:r0 + first_tap + k + _DW_ROWS, :]
            y_ref[r0:r0 + _DW_ROWS, lanes] = acc
        return carry

    lax.fori_loop(0, n_slabs, slab_body, 0)
    y = y_ref[...]
    mu = jnp.mean(y, axis=-1, keepdims=True)
    yc = y - mu
    var = jnp.mean(yc * yc, axis=-1, keepdims=True)
    yn = yc * lax.rsqrt(var + EPS) * lng_ref[...] + lnb_ref[...]
    a = (yn * _sigmoid(yn)).astype(BF16)
    o_ref[...] = h_ref[...] + jnp.dot(a, w2_ref[...], preferred_element_type=F32) + b2_ref[...]


def _conv_tail(z, dw_w, dw_b, ln_g, ln_b, pw2, pw2_b, h2, seq, tm):
    t, d = h2.shape
    halo_blocks = tm // _DW_HALO
    return pl.pallas_call(
        functools.partial(_conv_tail_kernel, tm=tm, tiles_per_seq=seq // tm),
        out_shape=jax.ShapeDtypeStruct((t, d), F32),
        grid=(t // tm,),
        in_specs=[
            pl.BlockSpec((tm, d), lambda i: (i, 0)),
            pl.BlockSpec((_DW_HALO, d), lambda i: (jnp.maximum(i * halo_blocks - 1, 0), 0)),
            pl.BlockSpec(dw_w.shape, lambda i: (0, 0)),
            pl.BlockSpec((1, d), lambda i: (0, 0)),
            pl.BlockSpec((1, d), lambda i: (0, 0)),
            pl.BlockSpec((1, d), lambda i: (0, 0)),
            pl.BlockSpec((None, d, d), lambda i: (0, 0, 0)),
            pl.BlockSpec((1, d), lambda i: (0, 0)),
            pl.BlockSpec((tm, d), lambda i: (i, 0)),
        ],
        out_specs=pl.BlockSpec((tm, d), lambda i: (i, 0)),
        scratch_shapes=[pltpu.VMEM((d // LANES, _DW_HALO + tm, LANES), F32), pltpu.VMEM((tm, d), F32)],
        compiler_params=_params(("arbitrary",), 56),
        name="conv_tail",
    )(z, z, dw_w, dw_b, ln_g, ln_b, pw2, pw2_b, h2)


def _tiles(seq):
    def rows(want):
        return min(want, seq)

    return dict(
        in_proj_tm=rows(1024), in_proj_tn=1024,
        mlstm_chunks_per_step=min(4, seq // CHUNK),
        attn_q_rows=rows(256),
        out_proj_tm=rows(512),
        mlp_tm=rows(512), mlp_tf=1024,
        glu_tm=rows(1024), glu_tn=512,
        conv_tm=rows(512),
    )


def _row(v):
    return v.reshape(1, -1).astype(F32)


def kernel(x, mixer_norm_g, mix_w_in, qk_conv_w, qk_conv_b, igate_b, fgate_b, mlstm_norm_g, rel_bias,
           mix_w_out, conv_pw1_w, conv_pw1_b, conv_dw_w, conv_dw_b, conv_ln_g, conv_ln_b, conv_pw2_w,
           conv_pw2_b, mlp_norm_g, mlp_w1, mlp_w2, final_norm_g):
    batch, seq, d = x.shape
    tokens = batch * seq
    n_chunks = seq // CHUNK
    dm = d // 2
    da = d - dm
    heads = MLSTM_HEADS
    tl = _tiles(seq)
    x2 = x.reshape(tokens, d)
    mlp_w1_bf = mlp_w1.astype(BF16)
    mlp_w2_bf = mlp_w2.astype(BF16)

    w_in = mix_w_in[0]
    gate_lo = 4 * dm
    gate_hi = gate_lo + 2 * heads
    w_main = jnp.concatenate([w_in[:, :gate_lo], w_in[:, gate_hi:]], axis=1).astype(BF16)
    w_gate = jnp.pad(w_in[:, gate_lo:gate_hi], ((0, 0), (0, LANES - 2 * heads))).astype(BF16)
    proj, gates = _in_proj(x2, _row(mixer_norm_g[0]), w_main, w_gate, tl["in_proj_tm"], tl["in_proj_tn"])

    n_bh = batch * heads
    g8 = gates[:, :2 * heads].reshape(batch, n_chunks, CHUNK, 2, heads)
    g8 = jnp.transpose(g8, (3, 1, 0, 4, 2)).reshape(2, n_chunks * n_bh, CHUNK)
    g8 = jnp.pad(g8, ((0, 0), (0, 0), (0, LANES - CHUNK)))
    bias_rows = lambda bvec: jnp.broadcast_to(
        jnp.tile(bvec.astype(F32), batch * n_chunks)[:, None], (n_chunks * n_bh, LANES))
    r, mt, wi, et, ws, dc = _gate_prep(g8[0], g8[1], bias_rows(igate_b[0]), bias_rows(fgate_b[0]),
                                       n_chunks, n_bh)
    per_frame = jnp.stack([mt, wi, et, ws], axis=0)[:, :, :CHUNK]
    per_frame = per_frame.reshape(4, n_chunks, batch, heads, CHUNK)
    cols = jnp.transpose(per_frame, (2, 1, 4, 0, 3)).reshape(batch, n_chunks, CHUNK, 4 * heads)
    cols = jnp.pad(cols, ((0, 0), (0, 0), (0, 0), (0, LANES - 4 * heads)))
    rows = jnp.concatenate([r.reshape(n_chunks, batch, heads, LANES),
                            dc.reshape(n_chunks, batch, heads, LANES)], axis=2)
    rows = jnp.transpose(rows, (1, 0, 2, 3))

    h_a = _mlstm(proj, qk_conv_w[0].astype(F32), _row(qk_conv_b[0]), cols, rows,
                 _row(mlstm_norm_g[0]), batch, n_chunks, tl["mlstm_chunks_per_step"])

    tab = _bias_table(rel_bias[0].astype(F32), tl["attn_q_rows"])
    h_b = _attention(proj, tab, da, batch, seq, tl["attn_q_rows"])

    h = _out_proj(h_a, h_b, mix_w_out.astype(BF16).reshape(2, dm, d), x2, tl["out_proj_tm"])

    h = _mlp(h, _row(mlp_norm_g[0]), mlp_w1_bf, mlp_w2_bf, 0, _row(final_norm_g), False,
             tl["mlp_tm"], tl["mlp_tf"])

    z = _glu(h, _row(mixer_norm_g[1]), conv_pw1_w.astype(BF16), _row(conv_pw1_b[0]),
             tl["glu_tm"], tl["glu_tn"])
    h = _conv_tail(z, conv_dw_w[0].astype(F32), _row(conv_dw_b[0]), _row(conv_ln_g[0]),
                   _row(conv_ln_b[0]), conv_pw2_w.astype(BF16), _row(conv_pw2_b[0]), h, seq,
                   tl["conv_tm"])

    h = _mlp(h, _row(mlp_norm_g[1]), mlp_w1_bf, mlp_w2_bf, 1, _row(final_norm_g), True,
             tl["mlp_tm"], tl["mlp_tf"])
    return h.reshape(batch, seq, d)
```

```python
import functools
import math

import jax
import jax.numpy as jnp
from jax import lax
from jax.experimental import pallas as pl
from jax.experimental.pallas import tpu as pltpu

CHUNK = 64
MLSTM_HEADS = 4
MLSTM_CONV = 4
ATTN_HEADS = 8
LEFT_CHUNKS = 8
MAX_REL = 256
CONV_WIDTH = 31
EPS = 1e-6

V7X_VMEM_BYTES = 64 * 1024 * 1024
LANES = 128
BF16_SUBLANES = 16

F32 = jnp.float32
BF16 = jnp.bfloat16
NEG_INF = float("-inf")
_LOG2_E = math.log2(math.e)

_NT = (((1,), (1,)), ((), ()))
_TN = (((0,), (0,)), ((), ()))


def _params(semantics, vmem_mib):
    assert vmem_mib * 1024 * 1024 <= V7X_VMEM_BYTES
    return pltpu.CompilerParams(dimension_semantics=semantics,
                                vmem_limit_bytes=vmem_mib * 1024 * 1024)


def _rmsnorm_f32(x, g):
    return x * lax.rsqrt(jnp.mean(x * x, axis=-1, keepdims=True) + EPS) * g


def _sigmoid(x):
    return 0.5 * jnp.tanh(0.5 * x) + 0.5


def _in_proj_kernel(x_ref, g_ref, w_ref, wg_ref, o_ref, og_ref, u_ref):
    @pl.when(pl.program_id(1) == 0)
    def _():
        u = _rmsnorm_f32(x_ref[...], g_ref[...]).astype(BF16)
        u_ref[...] = u
        og_ref[...] = jnp.dot(u, wg_ref[...], preferred_element_type=F32)

    o_ref[...] = jnp.dot(u_ref[...], w_ref[...], preferred_element_type=F32).astype(o_ref.dtype)


def _in_proj(x2, g, w_main, w_gate, tm, tn):
    t, d = x2.shape
    n = w_main.shape[1]
    return pl.pallas_call(
        _in_proj_kernel,
        out_shape=(jax.ShapeDtypeStruct((t, n), BF16), jax.ShapeDtypeStruct((t, LANES), F32)),
        grid=(t // tm, n // tn),
        in_specs=[
            pl.BlockSpec((tm, d), lambda i, j: (i, 0)),
            pl.BlockSpec((1, d), lambda i, j: (0, 0)),
            pl.BlockSpec((d, tn), lambda i, j: (0, j)),
            pl.BlockSpec((d, LANES), lambda i, j: (0, 0)),
        ],
        out_specs=(pl.BlockSpec((tm, tn), lambda i, j: (i, j)),
                   pl.BlockSpec((tm, LANES), lambda i, j: (i, 0))),
        scratch_shapes=[pltpu.VMEM((tm, d), BF16)],
        compiler_params=_params(("arbitrary", "arbitrary"), 56),
        name="in_proj",
    )(x2, g, w_main, w_gate)


def _lane_prefix(x, op, ident, lane):
    shift = 1
    while shift < CHUNK:
        y = pltpu.roll(x, shift, axis=1)
        x = op(x, jnp.where(lane >= shift, y, ident))
        shift *= 2
    return x


def _gate_prep_kernel(gi_ref, gf_ref, ib_ref, fb_ref,
                      r_ref, mt_ref, wi_ref, et_ref, ws_ref, dc_ref,
                      g_sc, c_sc, m_sc, *, n_chunks, n_bh):
    shape = gi_ref.shape
    lane = lax.broadcasted_iota(jnp.int32, shape, 1)
    valid = lane < CHUNK
    i_pre = gi_ref[...] + ib_ref[...]
    f_pre = gf_ref[...] + fb_ref[...]
    lf = jnp.minimum(f_pre, 0.0) - jnp.log1p(jnp.exp(-jnp.abs(f_pre)))
    lf = jnp.where(valid, lf, 0.0)
    b = _lane_prefix(lf, jnp.add, 0.0, lane)
    r = jnp.where(valid, i_pre - b, NEG_INF)
    cm = _lane_prefix(r, jnp.maximum, NEG_INF, lane)
    g_tot = jnp.sum(jnp.where(lane == CHUNK - 1, b, 0.0), axis=1, keepdims=True)
    c_last = jnp.max(r, axis=1, keepdims=True)
    g_sc[...] = jnp.broadcast_to(g_tot, shape)
    c_sc[...] = jnp.broadcast_to(c_last, shape)

    def body(c, m):
        rows = pl.ds(pl.multiple_of(c * n_bh, n_bh), n_bh)
        m_sc[rows, :] = m
        return g_sc[rows, :] + jnp.maximum(m, c_sc[rows, :])

    lax.fori_loop(0, n_chunks, body, jnp.zeros((n_bh, shape[1]), F32))
    m = m_sc[...]
    big_m = jnp.maximum(m, cm)
    m_last = jnp.maximum(m, c_sc[...])
    r_ref[...] = jnp.where(valid, r, 0.0)
    mt_ref[...] = big_m
    wi_ref[...] = jnp.exp(m - big_m)
    et_ref[...] = jnp.exp(-b - big_m)
    ws_ref[...] = jnp.exp(r - m_last)
    dc_ref[...] = jnp.exp(m - m_last)


def _gate_prep(gi, gf, ib, fb, n_chunks, n_bh):
    shape = jax.ShapeDtypeStruct(gi.shape, F32)
    return pl.pallas_call(
        functools.partial(_gate_prep_kernel, n_chunks=n_chunks, n_bh=n_bh),
        out_shape=(shape,) * 6,
        scratch_shapes=[pltpu.VMEM(gi.shape, F32)] * 3,
        compiler_params=_params(None, 32),
        name="gate_prep",
    )(gi, gf, ib, fb)


_QK_HALO = BF16_SUBLANES


def _mlstm_kernel(qk_ref, halo_ref, v_ref, o_ref, cw_ref, cb_ref, cols_ref, rows_ref, ng_ref,
                  out_ref, xs_ref, c_ref, n_ref, *, heads, dh, chunks_per_step):
    group_id = pl.program_id(1)
    dm = heads * dh
    n_slabs = 2 * dm // LANES

    @pl.when(group_id == 0)
    def _():
        c_ref[...] = jnp.zeros(c_ref.shape, F32)
        n_ref[...] = jnp.zeros(n_ref.shape, F32)

    for s in range(n_slabs):
        lanes = slice(s * LANES, (s + 1) * LANES)
        xs_ref[s, 0:_QK_HALO, :] = jnp.where(group_id == 0, 0.0, halo_ref[:, lanes].astype(F32))
        xs_ref[s, _QK_HALO:, :] = qk_ref[:, lanes].astype(F32)

    def conv_silu(col0, row0):
        parts = []
        for s in range(col0 // LANES, (col0 + dh) // LANES):
            lanes = slice(s * LANES, (s + 1) * LANES)
            acc = cb_ref[:, lanes]
            for j in range(MLSTM_CONV):
                off = row0 + _QK_HALO - (MLSTM_CONV - 1) + j
                acc = acc + cw_ref[j:j + 1, lanes] * xs_ref[s, off:off + CHUNK, :]
            parts.append(acc * _sigmoid(acc))
        return jnp.concatenate(parts, axis=1)

    tri = (lax.broadcasted_iota(jnp.int32, (CHUNK, CHUNK), 0)
           >= lax.broadcasted_iota(jnp.int32, (CHUNK, CHUNK), 1))
    eye = (lax.broadcasted_iota(jnp.int32, (dh, dh), 0)
           == lax.broadcasted_iota(jnp.int32, (dh, dh), 1)).astype(BF16)
    for cc in range(chunks_per_step):
        row0 = cc * CHUNK
        rs = slice(row0, row0 + CHUNK)
        for h in range(heads):
            hs = slice(h * dh, (h + 1) * dh)
            q = conv_silu(h * dh, row0)
            k = conv_silu(dm + h * dh, row0) * (dh ** -0.5)
            qb = q.astype(BF16)
            kb = k.astype(BF16)
            v = v_ref[rs, hs]
            mt = cols_ref[cc, :, h:h + 1]
            wi = cols_ref[cc, :, heads + h:heads + h + 1]
            et = cols_ref[cc, :, 2 * heads + h:2 * heads + h + 1]
            ws = cols_ref[cc, :, 3 * heads + h:3 * heads + h + 1]
            r = rows_ref[cc, h:h + 1, 0:CHUNK]
            dc = rows_ref[cc, heads + h:heads + h + 1, 0:1]

            s = lax.dot_general(qb, kb, _NT, preferred_element_type=F32)
            sw = s * jnp.where(tri, jnp.exp(r - mt), 0.0)
            c_old = c_ref[h]
            inter = lax.dot_general(qb, c_old.astype(BF16), _NT, preferred_element_type=F32)
            intra = jnp.dot(sw.astype(BF16), v, preferred_element_type=F32)
            n_old = n_ref[h:h + 1, :]
            num = intra + wi * inter
            den = (jnp.sum(sw, axis=-1, keepdims=True)
                   + wi * jnp.sum(q * n_old, axis=-1, keepdims=True))
            hh = num / jnp.maximum(jnp.abs(den), et)
            hn = hh * lax.rsqrt(jnp.mean(hh * hh, axis=-1, keepdims=True) + EPS) * ng_ref[:, hs]
            out_ref[rs, hs] = (_sigmoid(o_ref[rs, hs].astype(F32)) * hn).astype(out_ref.dtype)

            vw = (v.astype(F32) * ws).astype(BF16)
            vw_t = lax.dot_general(eye, vw, _NT, preferred_element_type=F32).astype(BF16)
            upd = jnp.dot(vw_t, kb, preferred_element_type=F32)
            c_ref[h] = dc * c_old + upd
            n_ref[h:h + 1, :] = dc * n_old + jnp.sum(k * ws, axis=0, keepdims=True)


def _mlstm(proj, conv_w, conv_b, cols, rows, norm_g, batch, n_chunks, chunks_per_step):
    t = proj.shape[0]
    dm = norm_g.shape[1]
    dh = dm // MLSTM_HEADS
    n_groups = n_chunks // chunks_per_step
    rows_per_step = chunks_per_step * CHUNK
    halo_blocks = rows_per_step // _QK_HALO
    return pl.pallas_call(
        functools.partial(_mlstm_kernel, heads=MLSTM_HEADS, dh=dh, chunks_per_step=chunks_per_step),
        out_shape=jax.ShapeDtypeStruct((t, dm), BF16),
        grid=(batch, n_groups),
        in_specs=[
            pl.BlockSpec((rows_per_step, 2 * dm), lambda b, c: (b * n_groups + c, 0)),
            pl.BlockSpec((_QK_HALO, 2 * dm),
                         lambda b, c: (jnp.maximum((b * n_groups + c) * halo_blocks - 1, 0), 0)),
            pl.BlockSpec((rows_per_step, dm), lambda b, c: (b * n_groups + c, 2)),
            pl.BlockSpec((rows_per_step, dm), lambda b, c: (b * n_groups + c, 3)),
            pl.BlockSpec((MLSTM_CONV, 2 * dm), lambda b, c: (0, 0)),
            pl.BlockSpec((1, 2 * dm), lambda b, c: (0, 0)),
            pl.BlockSpec((None, chunks_per_step, CHUNK, LANES), lambda b, c: (b, c, 0, 0)),
            pl.BlockSpec((None, chunks_per_step, 2 * MLSTM_HEADS, LANES), lambda b, c: (b, c, 0, 0)),
            pl.BlockSpec((1, dm), lambda b, c: (0, 0)),
        ],
        out_specs=pl.BlockSpec((rows_per_step, dm), lambda b, c: (b * n_groups + c, 0)),
        scratch_shapes=[
            pltpu.VMEM((2 * dm // LANES, _QK_HALO + rows_per_step, LANES), F32),
            pltpu.VMEM((MLSTM_HEADS, dh, dh), F32),
            pltpu.VMEM((2 * MLSTM_HEADS, dh), F32),
        ],
        compiler_params=_params(("arbitrary", "arbitrary"), 32),
        name="mlstm",
    )(proj, proj, proj, proj, conv_w, conv_b, cols, rows, norm_g)


def _attn_kernel(*refs, heads, dh, n_pieces):
    q_ref = refs[0]
    k_refs = refs[1:1 + n_pieces]
    v_refs = refs[1 + n_pieces:1 + 2 * n_pieces]
    tab_refs = refs[1 + 2 * n_pieces:1 + 3 * n_pieces]
    out_ref = refs[1 + 3 * n_pieces]
    scale2 = (dh ** -0.5) * _LOG2_E
    for h in range(heads):
        hs = slice(h * dh, (h + 1) * dh)
        q = q_ref[:, hs]
        scores = [lax.dot_general(q, k_ref[:, hs], _NT, preferred_element_type=F32) * scale2 + tab_ref[h]
                  for k_ref, tab_ref in zip(k_refs, tab_refs)]
        m = jnp.max(functools.reduce(jnp.maximum, scores), axis=-1, keepdims=True)
        es = [jnp.exp2(s - m) for s in scores]
        denom = jnp.sum(functools.reduce(jnp.add, es), axis=-1, keepdims=True)
        acc = functools.reduce(jnp.add, [
            jnp.dot(e.astype(BF16), v_ref[:, hs], preferred_element_type=F32) for e, v_ref in zip(es, v_refs)])
        out_ref[:, hs] = (acc / denom).astype(out_ref.dtype)


def _attention(proj, tab, d_attn, batch, seq, q_rows):
    t = proj.shape[0]
    dh = d_attn // ATTN_HEADS
    n_q = seq // q_rows
    pad_blocks = (LEFT_CHUNKS * CHUNK) // q_rows
    n_pieces = 1 + pad_blocks
    assert tab.shape == (n_pieces + 1, ATTN_HEADS, q_rows, q_rows)
    q_col = proj.shape[1] // d_attn - 3

    def window_spec(col_block, p):
        return pl.BlockSpec(
            (q_rows, d_attn),
            lambda b, qi: (b * n_q + jnp.maximum(qi + p - pad_blocks, 0), col_block))

    def tab_spec(p):
        return pl.BlockSpec(
            (None, ATTN_HEADS, q_rows, q_rows),
            lambda b, qi: (jnp.where(qi + p >= pad_blocks, p, n_pieces), 0, 0, 0))

    return pl.pallas_call(
        functools.partial(_attn_kernel, heads=ATTN_HEADS, dh=dh, n_pieces=n_pieces),
        out_shape=jax.ShapeDtypeStruct((t, d_attn), BF16),
        grid=(batch, n_q),
        in_specs=[pl.BlockSpec((q_rows, d_attn), lambda b, qi: (b * n_q + qi, q_col))]
        + [window_spec(q_col + 1, p) for p in range(n_pieces)]
        + [window_spec(q_col + 2, p) for p in range(n_pieces)]
        + [tab_spec(p) for p in range(n_pieces)],
        out_specs=pl.BlockSpec((q_rows, d_attn), lambda b, qi: (b * n_q + qi, 0)),
        compiler_params=_params(("arbitrary", "arbitrary"), 40),
        name="chunk_attn",
    )(proj, *([proj] * (2 * n_pieces)), *([tab] * n_pieces))


def _bias_table(rel, q_rows):
    heads = rel.shape[0]
    pad_rows = LEFT_CHUNKS * CHUNK
    width = q_rows + pad_rows
    d_lo, d_hi = pad_rows - width + 1, pad_rows + q_rows - 1
    mid = rel[:, max(d_lo, -MAX_REL) + MAX_REL:min(d_hi, MAX_REL) + MAX_REL + 1]
    left = jnp.repeat(rel[:, :1], max(0, -MAX_REL - d_lo), axis=1)
    right = jnp.repeat(rel[:, -1:], max(0, d_hi - MAX_REL), axis=1)
    by_dist = jnp.concatenate([left, mid, right], axis=1)[:, ::-1]
    length = by_dist.shape[1]
    ring = jnp.pad(by_dist, ((0, 0), (0, 1)))
    skew = jnp.tile(ring, (1, q_rows))[:, :q_rows * length].reshape(heads, q_rows, length)
    bias = skew[:, :, q_rows - 1:q_rows - 1 + width]
    q_chunk = jnp.arange(q_rows)[:, None] // CHUNK
    k_chunk = jnp.arange(width)[None, :] // CHUNK
    in_band = (k_chunk >= q_chunk) & (k_chunk <= q_chunk + LEFT_CHUNKS)
    tab = jnp.where(in_band[None], bias * _LOG2_E, NEG_INF)
    pieces = [tab[:, :, p * q_rows:(p + 1) * q_rows] for p in range(width // q_rows)]
    pieces.append(jnp.full((heads, q_rows, q_rows), NEG_INF, F32))
    return jnp.stack(pieces, axis=0)


def _out_proj_kernel(a_ref, b_ref, wa_ref, wb_ref, x_ref, g_ref, o_ref, u_ref):
    acc = jnp.dot(a_ref[...], wa_ref[...], preferred_element_type=F32)
    acc = acc + jnp.dot(b_ref[...], wb_ref[...], preferred_element_type=F32)
    h = x_ref[...] + acc
    o_ref[...] = h
    u_ref[...] = _rmsnorm_f32(h, g_ref[...]).astype(BF16)


def _out_proj(h_a, h_b, w, x2, g_next, tm):
    t, d = x2.shape
    ka, kb = h_a.shape[1], h_b.shape[1]
    assert ka == kb
    return pl.pallas_call(
        _out_proj_kernel,
        out_shape=(jax.ShapeDtypeStruct((t, d), F32), jax.ShapeDtypeStruct((t, d), BF16)),
        grid=(t // tm,),
        in_specs=[
            pl.BlockSpec((tm, ka), lambda i: (i, 0)),
            pl.BlockSpec((tm, kb), lambda i: (i, 0)),
            pl.BlockSpec((None, ka, d), lambda i: (0, 0, 0)),
            pl.BlockSpec((None, kb, d), lambda i: (1, 0, 0)),
            pl.BlockSpec((tm, d), lambda i: (i, 0)),
            pl.BlockSpec((1, d), lambda i: (0, 0)),
        ],
        out_specs=(pl.BlockSpec((tm, d), lambda i: (i, 0)), pl.BlockSpec((tm, d), lambda i: (i, 0))),
        compiler_params=_params(("arbitrary",), 56),
        name="out_proj",
    )(h_a, h_b, w, w, x2, g_next)


def _mlp_kernel(h_ref, u_ref, w1_ref, w2_ref, gp_ref, *out_refs, final_norm):
    o_ref = out_refs[0]
    j = pl.program_id(1)

    @pl.when(j == 0)
    def _():
        o_ref[...] = h_ref[...]

    a = jnp.dot(u_ref[...], w1_ref[...], preferred_element_type=F32)
    a = jnp.square(jnp.maximum(a, 0.0)).astype(BF16)
    o_ref[...] += jnp.dot(a, w2_ref[...], preferred_element_type=F32)

    @pl.when(j == pl.num_programs(1) - 1)
    def _():
        normed = _rmsnorm_f32(o_ref[...], gp_ref[...])
        if final_norm:
            o_ref[...] = normed
        else:
            out_refs[1][...] = normed.astype(BF16)


def _mlp(h2, u, w1, w2, layer, g_post, final_norm, tm, tf):
    t, d = h2.shape
    f = w1.shape[2]
    row_tile = pl.BlockSpec((tm, d), lambda i, j: (i, 0))
    out_shape = [jax.ShapeDtypeStruct((t, d), F32)]
    if not final_norm:
        out_shape.append(jax.ShapeDtypeStruct((t, d), BF16))
    return pl.pallas_call(
        functools.partial(_mlp_kernel, final_norm=final_norm),
        out_shape=tuple(out_shape),
        grid=(t // tm, f // tf),
        in_specs=[
            row_tile,
            row_tile,
            pl.BlockSpec((None, d, tf), lambda i, j: (layer, 0, j)),
            pl.BlockSpec((None, tf, d), lambda i, j: (layer, j, 0)),
            pl.BlockSpec((1, d), lambda i, j: (0, 0)),
        ],
        out_specs=tuple([row_tile] * len(out_shape)),
        compiler_params=_params(("arbitrary", "arbitrary"), 56),
        name="mlp_final" if final_norm else "mlp",
    )(h2, u, w1, w2, g_post)


def _glu_kernel(u_ref, wa_ref, wg_ref, ba_ref, bg_ref, o_ref):
    u = u_ref[...]
    a = jnp.dot(u, wa_ref[...], preferred_element_type=F32) + ba_ref[...]
    gate = jnp.dot(u, wg_ref[...], preferred_element_type=F32) + bg_ref[...]
    o_ref[...] = (a * _sigmoid(gate)).astype(o_ref.dtype)


def _glu(u, pw1, pw1_b, tm, tn):
    t, d = u.shape
    n = pw1.shape[2] // 2
    nb = n // tn
    return pl.pallas_call(
        _glu_kernel,
        out_shape=jax.ShapeDtypeStruct((t, n), BF16),
        grid=(t // tm, nb),
        in_specs=[
            pl.BlockSpec((tm, d), lambda i, j: (i, 0)),
            pl.BlockSpec((None, d, tn), lambda i, j: (0, 0, j)),
            pl.BlockSpec((None, d, tn), lambda i, j: (0, 0, j + nb)),
            pl.BlockSpec((1, tn), lambda i, j: (0, j)),
            pl.BlockSpec((1, tn), lambda i, j: (0, j + nb)),
        ],
        out_specs=pl.BlockSpec((tm, tn), lambda i, j: (i, j)),
        compiler_params=_params(("arbitrary", "arbitrary"), 56),
        name="conv_glu",
    )(u, pw1, pw1, pw1_b, pw1_b)


_DW_HALO = 2 * BF16_SUBLANES
_DW_ROWS = 64


def _conv_tail_kernel(z_ref, halo_ref, dw_ref, dwb_ref, lng_ref, lnb_ref, w2_ref, b2_ref, h_ref, g_ref,
                      o_ref, u_ref, zs_ref, y_ref, *, tm, tiles_per_seq):
    i = pl.program_id(0)
    n_slabs = z_ref.shape[1] // LANES
    seq_start = (i % tiles_per_seq) == 0
    for s in range(n_slabs):
        lanes = slice(s * LANES, (s + 1) * LANES)
        zs_ref[s, 0:_DW_HALO, :] = jnp.where(seq_start, 0.0, halo_ref[:, lanes].astype(F32))
        zs_ref[s, _DW_HALO:, :] = z_ref[:, lanes].astype(F32)
    first_tap = _DW_HALO - (CONV_WIDTH - 1)

    def slab_body(s, carry):
        lanes = pl.ds(pl.multiple_of(s * LANES, LANES), LANES)
        for rb in range(tm // _DW_ROWS):
            r0 = rb * _DW_ROWS
            acc = jnp.broadcast_to(dwb_ref[:, lanes], (_DW_ROWS, LANES))
            for k in range(CONV_WIDTH):
                acc = acc + dw_ref[k:k + 1, lanes] * zs_ref[s, r0 + first_tap + k:r0 + first_tap + k + _DW_ROWS, :]
            y_ref[r0:r0 + _DW_ROWS, lanes] = acc
        return carry

    lax.fori_loop(0, n_slabs, slab_body, 0)
    y = y_ref[...]
    mu = jnp.mean(y, axis=-1, keepdims=True)
    yc = y - mu
    var = jnp.mean(yc * yc, axis=-1, keepdims=True)
    yn = yc * lax.rsqrt(var + EPS) * lng_ref[...] + lnb_ref[...]
    a = (yn * _sigmoid(yn)).astype(BF16)
    h = h_ref[...] + jnp.dot(a, w2_ref[...], preferred_element_type=F32) + b2_ref[...]
    o_ref[...] = h
    u_ref[...] = _rmsnorm_f32(h, g_ref[...]).astype(BF16)


def _conv_tail(z, dw_w, dw_b, ln_g, ln_b, pw2, pw2_b, h2, g_next, seq, tm):
    t, d = h2.shape
    halo_blocks = tm // _DW_HALO
    row_tile = pl.BlockSpec((tm, d), lambda i: (i, 0))
    vec = pl.BlockSpec((1, d), lambda i: (0, 0))
    return pl.pallas_call(
        functools.partial(_conv_tail_kernel, tm=tm, tiles_per_seq=seq // tm),
        out_shape=(jax.ShapeDtypeStruct((t, d), F32), jax.ShapeDtypeStruct((t, d), BF16)),
        grid=(t // tm,),
        in_specs=[
            row_tile,
            pl.BlockSpec((_DW_HALO, d), lambda i: (jnp.maximum(i * halo_blocks - 1, 0), 0)),
            pl.BlockSpec(dw_w.shape, lambda i: (0, 0)),
            vec, vec, vec,
            pl.BlockSpec((None, d, d), lambda i: (0, 0, 0), pipeline_mode=pl.Buffered(1)),
            vec,
            row_tile,
            vec,
        ],
        out_specs=(row_tile, row_tile),
        scratch_shapes=[pltpu.VMEM((d // LANES, _DW_HALO + tm, LANES), F32), pltpu.VMEM((tm, d), F32)],
        compiler_params=_params(("arbitrary",), 56),
        name="conv_tail",
    )(z, z, dw_w, dw_b, ln_g, ln_b, pw2, pw2_b, h2, g_next)


def _tiles(seq):
    def rows(want):
        return min(want, seq)

    return dict(
        in_proj_tm=rows(512),
        mlstm_chunks_per_step=min(4, seq // CHUNK),
        attn_q_rows=rows(256),
        out_proj_tm=rows(512),
        mlp_tm=rows(512), mlp_tf=1024,
        glu_tm=rows(1024), glu_tn=1024,
        conv_tm=rows(512),
    )


def _row(v):
    return v.reshape(1, -1).astype(F32)


def kernel(x, mixer_norm_g, mix_w_in, qk_conv_w, qk_conv_b, igate_b, fgate_b, mlstm_norm_g, rel_bias,
           mix_w_out, conv_pw1_w, conv_pw1_b, conv_dw_w, conv_dw_b, conv_ln_g, conv_ln_b, conv_pw2_w,
           conv_pw2_b, mlp_norm_g, mlp_w1, mlp_w2, final_norm_g):
    batch, seq, d = x.shape
    tokens = batch * seq
    n_chunks = seq // CHUNK
    dm = d // 2
    da = d - dm
    heads = MLSTM_HEADS
    tl = _tiles(seq)
    x2 = x.reshape(tokens, d)
    mlp_w1_bf = mlp_w1.astype(BF16)
    mlp_w2_bf = mlp_w2.astype(BF16)

    w_in = mix_w_in[0]
    gate_lo = 4 * dm
    gate_hi = gate_lo + 2 * heads
    n_main = w_in.shape[1] - 2 * heads
    keep_left = lax.broadcasted_iota(jnp.int32, (1, n_main), 1) < gate_lo
    w_main = jnp.where(keep_left, w_in[:, :n_main], w_in[:, 2 * heads:]).astype(BF16)
    w_gate = jnp.pad(w_in[:, gate_lo:gate_hi], ((0, 0), (0, LANES - 2 * heads))).astype(BF16)
    proj, gates = _in_proj(x2, _row(mixer_norm_g[0]), w_main, w_gate, tl["in_proj_tm"], n_main // 2)

    n_bh = batch * heads
    g8 = gates[:, :2 * heads].reshape(batch, n_chunks, CHUNK, 2, heads)
    g8 = jnp.transpose(g8, (3, 1, 0, 4, 2)).reshape(2, n_chunks * n_bh, CHUNK)
    g8 = jnp.pad(g8, ((0, 0), (0, 0), (0, LANES - CHUNK)))
    bias_rows = lambda bvec: jnp.broadcast_to(
        jnp.tile(bvec.astype(F32), batch * n_chunks)[:, None], (n_chunks * n_bh, LANES))
    r, mt, wi, et, ws, dc = _gate_prep(g8[0], g8[1], bias_rows(igate_b[0]), bias_rows(fgate_b[0]),
                                       n_chunks, n_bh)
    per_frame = jnp.stack([mt, wi, et, ws], axis=0)[:, :, :CHUNK]
    per_frame = per_frame.reshape(4, n_chunks, batch, heads, CHUNK)
    cols = jnp.transpose(per_frame, (2, 1, 4, 0, 3)).reshape(batch, n_chunks, CHUNK, 4 * heads)
    cols = jnp.pad(cols, ((0, 0), (0, 0), (0, 0), (0, LANES - 4 * heads)))
    rows = jnp.concatenate([r.reshape(n_chunks, batch, heads, LANES),
                            dc.reshape(n_chunks, batch, heads, LANES)], axis=2)
    rows = jnp.transpose(rows, (1, 0, 2, 3))

    h_a = _mlstm(proj, qk_conv_w[0].astype(F32), _row(qk_conv_b[0]), cols, rows,
                 _row(mlstm_norm_g[0]), batch, n_chunks, tl["mlstm_chunks_per_step"])

    tab = _bias_table(rel_bias[0].astype(F32), tl["attn_q_rows"])
    h_b = _attention(proj, tab, da, batch, seq, tl["attn_q_rows"])

    h, u = _out_proj(h_a, h_b, mix_w_out.astype(BF16).reshape(2, dm, d), x2, _row(mlp_norm_g[0]),
                     tl["out_proj_tm"])

    h, u = _mlp(h, u, mlp_w1_bf, mlp_w2_bf, 0, _row(mixer_norm_g[1]), False, tl["mlp_tm"], tl["mlp_tf"])

    z = _glu(u, conv_pw1_w.astype(BF16), _row(conv_pw1_b[0]), tl["glu_tm"], tl["glu_tn"])
    h, u = _conv_tail(z, conv_dw_w[0].astype(F32), _row(conv_dw_b[0]), _row(conv_ln_g[0]),
                      _row(conv_ln_b[0]), conv_pw2_w.astype(BF16), _row(conv_pw2_b[0]), h,
                      _row(mlp_norm_g[1]), seq, tl["conv_tm"])

    (out,) = _mlp(h, u, mlp_w1_bf, mlp_w2_bf, 1, _row(final_norm_g), True, tl["mlp_tm"], tl["mlp_tf"])
    return out.reshape(batch, seq, d)
```

```python
import functools
import math

import jax
import jax.numpy as jnp
from jax import lax
from jax.experimental import pallas as pl
from jax.experimental.pallas import tpu as pltpu

CHUNK = 64
MLSTM_HEADS = 4
MLSTM_CONV = 4
ATTN_HEADS = 8
LEFT_CHUNKS = 8
MAX_REL = 256
CONV_WIDTH = 31
EPS = 1e-6

V7X_VMEM_BYTES = 64 * 1024 * 1024
LANES = 128
BF16_SUBLANES = 16

F32 = jnp.float32
BF16 = jnp.bfloat16
NEG_INF = float("-inf")
_LOG2_E = math.log2(math.e)

_NT = (((1,), (1,)), ((), ()))
_TN = (((0,), (0,)), ((), ()))


def _params(semantics, vmem_mib):
    assert vmem_mib * 1024 * 1024 <= V7X_VMEM_BYTES
    return pltpu.CompilerParams(dimension_semantics=semantics,
                                vmem_limit_bytes=vmem_mib * 1024 * 1024)


def _rmsnorm_f32(x, g):
    return x * lax.rsqrt(jnp.mean(x * x, axis=-1, keepdims=True) + EPS) * g


def _sigmoid(x):
    return 0.5 * jnp.tanh(0.5 * x) + 0.5


def _drop_columns_kernel(w_ref, o_ref, *, lo, hi):
    o_ref[:, 0:lo] = w_ref[:, 0:lo].astype(BF16)
    o_ref[:, lo:] = w_ref[:, lo:][:, hi - lo:].astype(BF16)


def _drop_columns_bf16(w, lo, hi, rows):
    d, n_in = w.shape
    n = n_in - (hi - lo)
    assert lo % LANES == 0 and n % LANES == 0
    return pl.pallas_call(
        functools.partial(_drop_columns_kernel, lo=lo, hi=hi),
        out_shape=jax.ShapeDtypeStruct((d, n), BF16),
        grid=(d // rows,),
        in_specs=[pl.BlockSpec((rows, n_in), lambda i: (i, 0))],
        out_specs=pl.BlockSpec((rows, n), lambda i: (i, 0)),
        compiler_params=_params(("arbitrary",), 32),
        name="w_in_repack",
    )(w)


def _in_proj_kernel(x_ref, g_ref, w_ref, wg_ref, o_ref, og_ref, u_ref):
    @pl.when(pl.program_id(1) == 0)
    def _():
        u = _rmsnorm_f32(x_ref[...], g_ref[...]).astype(BF16)
        u_ref[...] = u
        og_ref[...] = jnp.dot(u, wg_ref[...], preferred_element_type=F32)

    o_ref[...] = jnp.dot(u_ref[...], w_ref[...], preferred_element_type=F32).astype(o_ref.dtype)


def _in_proj(x2, g, w_main, w_gate, tm, tn):
    t, d = x2.shape
    n = w_main.shape[1]
    return pl.pallas_call(
        _in_proj_kernel,
        out_shape=(jax.ShapeDtypeStruct((t, n), BF16), jax.ShapeDtypeStruct((t, LANES), F32)),
        grid=(t // tm, n // tn),
        in_specs=[
            pl.BlockSpec((tm, d), lambda i, j: (i, 0)),
            pl.BlockSpec((1, d), lambda i, j: (0, 0)),
            pl.BlockSpec((d, tn), lambda i, j: (0, j)),
            pl.BlockSpec((d, LANES), lambda i, j: (0, 0)),
        ],
        out_specs=(pl.BlockSpec((tm, tn), lambda i, j: (i, j)),
                   pl.BlockSpec((tm, LANES), lambda i, j: (i, 0))),
        scratch_shapes=[pltpu.VMEM((tm, d), BF16)],
        compiler_params=_params(("arbitrary", "arbitrary"), 56),
        name="in_proj",
    )(x2, g, w_main, w_gate)


def _lane_prefix(x, op, ident, lane):
    shift = 1
    while shift < CHUNK:
        y = pltpu.roll(x, shift, axis=1)
        x = op(x, jnp.where(lane >= shift, y, ident))
        shift *= 2
    return x


def _gate_prep_kernel(gi_ref, gf_ref, ib_ref, fb_ref,
                      r_ref, mt_ref, wi_ref, et_ref, ws_ref, dc_ref,
                      g_sc, c_sc, m_sc, *, n_chunks, n_bh):
    shape = gi_ref.shape
    lane = lax.broadcasted_iota(jnp.int32, shape, 1)
    valid = lane < CHUNK
    i_pre = gi_ref[...] + ib_ref[...]
    f_pre = gf_ref[...] + fb_ref[...]
    lf = jnp.minimum(f_pre, 0.0) - jnp.log1p(jnp.exp(-jnp.abs(f_pre)))
    lf = jnp.where(valid, lf, 0.0)
    b = _lane_prefix(lf, jnp.add, 0.0, lane)
    r = jnp.where(valid, i_pre - b, NEG_INF)
    cm = _lane_prefix(r, jnp.maximum, NEG_INF, lane)
    g_tot = jnp.sum(jnp.where(lane == CHUNK - 1, b, 0.0), axis=1, keepdims=True)
    c_last = jnp.max(r, axis=1, keepdims=True)
    g_sc[...] = jnp.broadcast_to(g_tot, shape)
    c_sc[...] = jnp.broadcast_to(c_last, shape)

    def body(c, m):
        rows = pl.ds(pl.multiple_of(c * n_bh, n_bh), n_bh)
        m_sc[rows, :] = m
        return g_sc[rows, :] + jnp.maximum(m, c_sc[rows, :])

    lax.fori_loop(0, n_chunks, body, jnp.zeros((n_bh, shape[1]), F32))
    m = m_sc[...]
    big_m = jnp.maximum(m, cm)
    m_last = jnp.maximum(m, c_sc[...])
    r_ref[...] = jnp.where(valid, r, 0.0)
    mt_ref[...] = big_m
    wi_ref[...] = jnp.exp(m - big_m)
    et_ref[...] = jnp.exp(-b - big_m)
    ws_ref[...] = jnp.exp(r - m_last)
    dc_ref[...] = jnp.exp(m - m_last)


def _gate_prep(gi, gf, ib, fb, n_chunks, n_bh):
    shape = jax.ShapeDtypeStruct(gi.shape, F32)
    return pl.pallas_call(
        functools.partial(_gate_prep_kernel, n_chunks=n_chunks, n_bh=n_bh),
        out_shape=(shape,) * 6,
        scratch_shapes=[pltpu.VMEM(gi.shape, F32)] * 3,
        compiler_params=_params(None, 32),
        name="gate_prep",
    )(gi, gf, ib, fb)


_QK_HALO = BF16_SUBLANES


def _mlstm_kernel(qk_ref, halo_ref, v_ref, o_ref, cw_ref, cb_ref, cols_ref, rows_ref, ng_ref,
                  out_ref, xs_ref, c_ref, n_ref, *, heads, dh, chunks_per_step):
    group_id = pl.program_id(1)
    dm = heads * dh
    n_slabs = 2 * dm // LANES

    @pl.when(group_id == 0)
    def _():
        c_ref[...] = jnp.zeros(c_ref.shape, F32)
        n_ref[...] = jnp.zeros(n_ref.shape, F32)

    for s in range(n_slabs):
        lanes = slice(s * LANES, (s + 1) * LANES)
        xs_ref[s, 0:_QK_HALO, :] = jnp.where(group_id == 0, 0.0, halo_ref[:, lanes].astype(F32))
        xs_ref[s, _QK_HALO:, :] = qk_ref[:, lanes].astype(F32)

    def conv_silu(col0, row0):
        parts = []
        for s in range(col0 // LANES, (col0 + dh) // LANES):
            lanes = slice(s * LANES, (s + 1) * LANES)
            acc = cb_ref[:, lanes]
            for j in range(MLSTM_CONV):
                off = row0 + _QK_HALO - (MLSTM_CONV - 1) + j
                acc = acc + cw_ref[j:j + 1, lanes] * xs_ref[s, off:off + CHUNK, :]
            parts.append(acc * _sigmoid(acc))
        return jnp.concatenate(parts, axis=1)

    tri = (lax.broadcasted_iota(jnp.int32, (CHUNK, CHUNK), 0)
           >= lax.broadcasted_iota(jnp.int32, (CHUNK, CHUNK), 1))
    eye = (lax.broadcasted_iota(jnp.int32, (dh, dh), 0)
           == lax.broadcasted_iota(jnp.int32, (dh, dh), 1)).astype(BF16)
    for cc in range(chunks_per_step):
        row0 = cc * CHUNK
        rs = slice(row0, row0 + CHUNK)
        for h in range(heads):
            hs = slice(h * dh, (h + 1) * dh)
            q = conv_silu(h * dh, row0)
            k = conv_silu(dm + h * dh, row0) * (dh ** -0.5)
            qb = q.astype(BF16)
            kb = k.astype(BF16)
            v = v_ref[rs, hs]
            mt = cols_ref[cc, :, h:h + 1]
            wi = cols_ref[cc, :, heads + h:heads + h + 1]
            et = cols_ref[cc, :, 2 * heads + h:2 * heads + h + 1]
            ws = cols_ref[cc, :, 3 * heads + h:3 * heads + h + 1]
            r = rows_ref[cc, h:h + 1, 0:CHUNK]
            dc = rows_ref[cc, heads + h:heads + h + 1, 0:1]

            s = lax.dot_general(qb, kb, _NT, preferred_element_type=F32)
            sw = s * jnp.where(tri, jnp.exp(r - mt), 0.0)
            c_old = c_ref[h]
            inter = lax.dot_general(qb, c_old.astype(BF16), _NT, preferred_element_type=F32)
            intra = jnp.dot(sw.astype(BF16), v, preferred_element_type=F32)
            n_old = n_ref[h:h + 1, :]
            num = intra + wi * inter
            den = (jnp.sum(sw, axis=-1, keepdims=True)
                   + wi * jnp.sum(q * n_old, axis=-1, keepdims=True))
            hh = num / jnp.maximum(jnp.abs(den), et)
            hn = hh * lax.rsqrt(jnp.mean(hh * hh, axis=-1, keepdims=True) + EPS) * ng_ref[:, hs]
            out_ref[rs, hs] = (_sigmoid(o_ref[rs, hs].astype(F32)) * hn).astype(out_ref.dtype)

            vw = (v.astype(F32) * ws).astype(BF16)
            vw_t = lax.dot_general(eye, vw, _NT, preferred_element_type=F32).astype(BF16)
            upd = jnp.dot(vw_t, kb, preferred_element_type=F32)
            c_ref[h] = dc * c_old + upd
            n_ref[h:h + 1, :] = dc * n_old + jnp.sum(k * ws, axis=0, keepdims=True)


def _mlstm(proj, conv_w, conv_b, cols, rows, norm_g, batch, n_chunks, chunks_per_step):
    t = proj.shape[0]
    dm = norm_g.shape[1]
    dh = dm // MLSTM_HEADS
    n_groups = n_chunks // chunks_per_step
    rows_per_step = chunks_per_step * CHUNK
    halo_blocks = rows_per_step // _QK_HALO
    return pl.pallas_call(
        functools.partial(_mlstm_kernel, heads=MLSTM_HEADS, dh=dh, chunks_per_step=chunks_per_step),
        out_shape=jax.ShapeDtypeStruct((t, dm), BF16),
        grid=(batch, n_groups),
        in_specs=[
            pl.BlockSpec((rows_per_step, 2 * dm), lambda b, c: (b * n_groups + c, 0)),
            pl.BlockSpec((_QK_HALO, 2 * dm),
                         lambda b, c: (jnp.maximum((b * n_groups + c) * halo_blocks - 1, 0), 0)),
            pl.BlockSpec((rows_per_step, dm), lambda b, c: (b * n_groups + c, 2)),
            pl.BlockSpec((rows_per_step, dm), lambda b, c: (b * n_groups + c, 3)),
            pl.BlockSpec((MLSTM_CONV, 2 * dm), lambda b, c: (0, 0)),
            pl.BlockSpec((1, 2 * dm), lambda b, c: (0, 0)),
            pl.BlockSpec((None, chunks_per_step, CHUNK, LANES), lambda b, c: (b, c, 0, 0)),
            pl.BlockSpec((None, chunks_per_step, 2 * MLSTM_HEADS, LANES), lambda b, c: (b, c, 0, 0)),
            pl.BlockSpec((1, dm), lambda b, c: (0, 0)),
        ],
        out_specs=pl.BlockSpec((rows_per_step, dm), lambda b, c: (b * n_groups + c, 0)),
        scratch_shapes=[
            pltpu.VMEM((2 * dm // LANES, _QK_HALO + rows_per_step, LANES), F32),
            pltpu.VMEM((MLSTM_HEADS, dh, dh), F32),
            pltpu.VMEM((2 * MLSTM_HEADS, dh), F32),
        ],
        compiler_params=_params(("arbitrary", "arbitrary"), 32),
        name="mlstm",
    )(proj, proj, proj, proj, conv_w, conv_b, cols, rows, norm_g)


def _attn_kernel(*refs, heads, dh, n_pieces):
    q_ref = refs[0]
    k_refs = refs[1:1 + n_pieces]
    v_refs = refs[1 + n_pieces:1 + 2 * n_pieces]
    tab_refs = refs[1 + 2 * n_pieces:1 + 3 * n_pieces]
    out_ref = refs[1 + 3 * n_pieces]
    scale2 = (dh ** -0.5) * _LOG2_E
    for h in range(heads):
        hs = slice(h * dh, (h + 1) * dh)
        q = q_ref[:, hs]
        scores = [lax.dot_general(q, k_ref[:, hs], _NT, preferred_element_type=F32) * scale2 + tab_ref[h]
                  for k_ref, tab_ref in zip(k_refs, tab_refs)]
        m = jnp.max(functools.reduce(jnp.maximum, scores), axis=-1, keepdims=True)
        es = [jnp.exp2(s - m) for s in scores]
        denom = jnp.sum(functools.reduce(jnp.add, es), axis=-1, keepdims=True)
        acc = functools.reduce(jnp.add, [
            jnp.dot(e.astype(BF16), v_ref[:, hs], preferred_element_type=F32) for e, v_ref in zip(es, v_refs)])
        out_ref[:, hs] = (acc / denom).astype(out_ref.dtype)


def _attention(proj, tab, d_attn, batch, seq, q_rows):
    t = proj.shape[0]
    dh = d_attn // ATTN_HEADS
    n_q = seq // q_rows
    pad_blocks = (LEFT_CHUNKS * CHUNK) // q_rows
    n_pieces = 1 + pad_blocks
    assert tab.shape == (n_pieces + 1, ATTN_HEADS, q_rows, q_rows)
    q_col = proj.shape[1] // d_attn - 3

    def window_spec(col_block, p):
        return pl.BlockSpec(
            (q_rows, d_attn),
            lambda b, qi: (b * n_q + jnp.maximum(qi + p - pad_blocks, 0), col_block))

    def tab_spec(p):
        return pl.BlockSpec(
            (None, ATTN_HEADS, q_rows, q_rows),
            lambda b, qi: (jnp.where(qi + p >= pad_blocks, p, n_pieces), 0, 0, 0))

    return pl.pallas_call(
        functools.partial(_attn_kernel, heads=ATTN_HEADS, dh=dh, n_pieces=n_pieces),
        out_shape=jax.ShapeDtypeStruct((t, d_attn), BF16),
        grid=(batch, n_q),
        in_specs=[pl.BlockSpec((q_rows, d_attn), lambda b, qi: (b * n_q + qi, q_col))]
        + [window_spec(q_col + 1, p) for p in range(n_pieces)]
        + [window_spec(q_col + 2, p) for p in range(n_pieces)]
        + [tab_spec(p) for p in range(n_pieces)],
        out_specs=pl.BlockSpec((q_rows, d_attn), lambda b, qi: (b * n_q + qi, 0)),
        compiler_params=_params(("arbitrary", "arbitrary"), 40),
        name="chunk_attn",
    )(proj, *([proj] * (2 * n_pieces)), *([tab] * n_pieces))


def _bias_table(rel, q_rows):
    heads = rel.shape[0]
    pad_rows = LEFT_CHUNKS * CHUNK
    width = q_rows + pad_rows
    d_lo, d_hi = pad_rows - width + 1, pad_rows + q_rows - 1
    mid = rel[:, max(d_lo, -MAX_REL) + MAX_REL:min(d_hi, MAX_REL) + MAX_REL + 1]
    left = jnp.repeat(rel[:, :1], max(0, -MAX_REL - d_lo), axis=1)
    right = jnp.repeat(rel[:, -1:], max(0, d_hi - MAX_REL), axis=1)
    by_dist = jnp.concatenate([left, mid, right], axis=1)[:, ::-1]
    length = by_dist.shape[1]
    ring = jnp.pad(by_dist, ((0, 0), (0, 1)))
    skew = jnp.tile(ring, (1, q_rows))[:, :q_rows * length].reshape(heads, q_rows, length)
    bias = skew[:, :, q_rows - 1:q_rows - 1 + width]
    q_chunk = jnp.arange(q_rows)[:, None] // CHUNK
    k_chunk = jnp.arange(width)[None, :] // CHUNK
    in_band = (k_chunk >= q_chunk) & (k_chunk <= q_chunk + LEFT_CHUNKS)
    tab = jnp.where(in_band[None], bias * _LOG2_E, NEG_INF)
    pieces = [tab[:, :, p * q_rows:(p + 1) * q_rows] for p in range(width // q_rows)]
    pieces.append(jnp.full((heads, q_rows, q_rows), NEG_INF, F32))
    return jnp.stack(pieces, axis=0)


def _out_proj_kernel(a_ref, b_ref, wa_ref, wb_ref, x_ref, g_ref, o_ref, u_ref):
    acc = jnp.dot(a_ref[...], wa_ref[...], preferred_element_type=F32)
    acc = acc + jnp.dot(b_ref[...], wb_ref[...], preferred_element_type=F32)
    h = x_ref[...] + acc
    o_ref[...] = h
    u_ref[...] = _rmsnorm_f32(h, g_ref[...]).astype(BF16)


def _out_proj(h_a, h_b, w, x2, g_next, tm):
    t, d = x2.shape
    ka, kb = h_a.shape[1], h_b.shape[1]
    assert ka == kb
    return pl.pallas_call(
        _out_proj_kernel,
        out_shape=(jax.ShapeDtypeStruct((t, d), F32), jax.ShapeDtypeStruct((t, d), BF16)),
        grid=(t // tm,),
        in_specs=[
            pl.BlockSpec((tm, ka), lambda i: (i, 0)),
            pl.BlockSpec((tm, kb), lambda i: (i, 0)),
            pl.BlockSpec((None, ka, d), lambda i: (0, 0, 0)),
            pl.BlockSpec((None, kb, d), lambda i: (1, 0, 0)),
            pl.BlockSpec((tm, d), lambda i: (i, 0)),
            pl.BlockSpec((1, d), lambda i: (0, 0)),
        ],
        out_specs=(pl.BlockSpec((tm, d), lambda i: (i, 0)), pl.BlockSpec((tm, d), lambda i: (i, 0))),
        compiler_params=_params(("arbitrary",), 56),
        name="out_proj",
    )(h_a, h_b, w, w, x2, g_next)


def _mlp_kernel(h_ref, u_ref, w1_ref, w2_ref, gp_ref, *out_refs, final_norm):
    o_ref = out_refs[0]
    j = pl.program_id(1)

    @pl.when(j == 0)
    def _():
        o_ref[...] = h_ref[...]

    a = jnp.dot(u_ref[...], w1_ref[...], preferred_element_type=F32)
    a = jnp.square(jnp.maximum(a, 0.0)).astype(BF16)
    o_ref[...] += jnp.dot(a, w2_ref[...], preferred_element_type=F32)

    @pl.when(j == pl.num_programs(1) - 1)
    def _():
        normed = _rmsnorm_f32(o_ref[...], gp_ref[...])
        if final_norm:
            o_ref[...] = normed
        else:
            out_refs[1][...] = normed.astype(BF16)


def _mlp(h2, u, w1, w2, layer, g_post, final_norm, tm, tf):
    t, d = h2.shape
    f = w1.shape[2]
    row_tile = pl.BlockSpec((tm, d), lambda i, j: (i, 0))
    out_shape = [jax.ShapeDtypeStruct((t, d), F32)]
    if not final_norm:
        out_shape.append(jax.ShapeDtypeStruct((t, d), BF16))
    return pl.pallas_call(
        functools.partial(_mlp_kernel, final_norm=final_norm),
        out_shape=tuple(out_shape),
        grid=(t // tm, f // tf),
        in_specs=[
            row_tile,
            row_tile,
            pl.BlockSpec((None, d, tf), lambda i, j: (layer, 0, j)),
            pl.BlockSpec((None, tf, d), lambda i, j: (layer, j, 0)),
            pl.BlockSpec((1, d), lambda i, j: (0, 0)),
        ],
        out_specs=tuple([row_tile] * len(out_shape)),
        compiler_params=_params(("arbitrary", "arbitrary"), 56),
        name="mlp_final" if final_norm else "mlp",
    )(h2, u, w1, w2, g_post)


def _glu_kernel(u_ref, wa_ref, wg_ref, ba_ref, bg_ref, o_ref):
    u = u_ref[...]
    a = jnp.dot(u, wa_ref[...], preferred_element_type=F32) + ba_ref[...]
    gate = jnp.dot(u, wg_ref[...], preferred_element_type=F32) + bg_ref[...]
    o_ref[...] = (a * _sigmoid(gate)).astype(o_ref.dtype)


def _glu(u, pw1, pw1_b, tm, tn):
    t, d = u.shape
    n = pw1.shape[2] // 2
    nb = n // tn
    return pl.pallas_call(
        _glu_kernel,
        out_shape=jax.ShapeDtypeStruct((t, n), BF16),
        grid=(t // tm, nb),
        in_specs=[
            pl.BlockSpec((tm, d), lambda i, j: (i, 0)),
            pl.BlockSpec((None, d, tn), lambda i, j: (0, 0, j)),
            pl.BlockSpec((None, d, tn), lambda i, j: (0, 0, j + nb)),
            pl.BlockSpec((1, tn), lambda i, j: (0, j)),
            pl.BlockSpec((1, tn), lambda i, j: (0, j + nb)),
        ],
        out_specs=pl.BlockSpec((tm, tn), lambda i, j: (i, j)),
        compiler_params=_params(("arbitrary", "arbitrary"), 56),
        name="conv_glu",
    )(u, pw1, pw1, pw1_b, pw1_b)


_DW_HALO = 2 * BF16_SUBLANES
_DW_ROWS = 64


def _conv_tail_kernel(z_ref, halo_ref, dw_ref, dwb_ref, lng_ref, lnb_ref, w2_ref, b2_ref, h_ref, g_ref,
                      o_ref, u_ref, zs_ref, y_ref, *, tm, tiles_per_seq):
    i = pl.program_id(0)
    n_slabs = z_ref.shape[1] // LANES
    seq_start = (i % tiles_per_seq) == 0
    for s in range(n_slabs):
        lanes = slice(s * LANES, (s + 1) * LANES)
        zs_ref[s, 0:_DW_HALO, :] = jnp.where(seq_start, 0.0, halo_ref[:, lanes].astype(F32))
        zs_ref[s, _DW_HALO:, :] = z_ref[:, lanes].astype(F32)
    first_tap = _DW_HALO - (CONV_WIDTH - 1)

    def slab_body(s, carry):
        lanes = pl.ds(pl.multiple_of(s * LANES, LANES), LANES)
        for rb in range(tm // _DW_ROWS):
            r0 = rb * _DW_ROWS
            acc = jnp.broadcast_to(dwb_ref[:, lanes], (_DW_ROWS, LANES))
            for k in range(CONV_WIDTH):
                acc = acc + dw_ref[k:k + 1, lanes] * zs_ref[s, r0 + first_tap + k:r0 + first_tap + k + _DW_ROWS, :]
            y_ref[r0:r0 + _DW_ROWS, lanes] = acc
        return carry

    lax.fori_loop(0, n_slabs, slab_body, 0)
    y = y_ref[...]
    mu = jnp.mean(y, axis=-1, keepdims=True)
    yc = y - mu
    var = jnp.mean(yc * yc, axis=-1, keepdims=True)
    yn = yc * lax.rsqrt(var + EPS) * lng_ref[...] + lnb_ref[...]
    a = (yn * _sigmoid(yn)).astype(BF16)
    h = h_ref[...] + jnp.dot(a, w2_ref[...], preferred_element_type=F32) + b2_ref[...]
    o_ref[...] = h
    u_ref[...] = _rmsnorm_f32(h, g_ref[...]).astype(BF16)


def _conv_tail(z, dw_w, dw_b, ln_g, ln_b, pw2, pw2_b, h2, g_next, seq, tm):
    t, d = h2.shape
    halo_blocks = tm // _DW_HALO
    row_tile = pl.BlockSpec((tm, d), lambda i: (i, 0))
    vec = pl.BlockSpec((1, d), lambda i: (0, 0))
    return pl.pallas_call(
        functools.partial(_conv_tail_kernel, tm=tm, tiles_per_seq=seq // tm),
        out_shape=(jax.ShapeDtypeStruct((t, d), F32), jax.ShapeDtypeStruct((t, d), BF16)),
        grid=(t // tm,),
        in_specs=[
            row_tile,
            pl.BlockSpec((_DW_HALO, d), lambda i: (jnp.maximum(i * halo_blocks - 1, 0), 0)),
            pl.BlockSpec(dw_w.shape, lambda i: (0, 0)),
            vec, vec, vec,
            pl.BlockSpec((None, d, d), lambda i: (0, 0, 0), pipeline_mode=pl.Buffered(1)),
            vec,
            row_tile,
            vec,
        ],
        out_specs=(row_tile, row_tile),
        scratch_shapes=[pltpu.VMEM((d // LANES, _DW_HALO + tm, LANES), F32), pltpu.VMEM((tm, d), F32)],
        compiler_params=_params(("arbitrary",), 56),
        name="conv_tail",
    )(z, z, dw_w, dw_b, ln_g, ln_b, pw2, pw2_b, h2, g_next)


def _tiles(seq):
    def rows(want):
        return min(want, seq)

    return dict(
        repack_rows=256,
        in_proj_tm=rows(512),
        mlstm_chunks_per_step=min(4, seq // CHUNK),
        attn_q_rows=rows(256),
        out_proj_tm=rows(512),
        mlp_tm=rows(512), mlp_tf=1024,
        glu_tm=rows(1024), glu_tn=1024,
        conv_tm=rows(512),
    )


def _row(v):
    return v.reshape(1, -1).astype(F32)


def kernel(x, mixer_norm_g, mix_w_in, qk_conv_w, qk_conv_b, igate_b, fgate_b, mlstm_norm_g, rel_bias,
           mix_w_out, conv_pw1_w, conv_pw1_b, conv_dw_w, conv_dw_b, conv_ln_g, conv_ln_b, conv_pw2_w,
           conv_pw2_b, mlp_norm_g, mlp_w1, mlp_w2, final_norm_g):
    batch, seq, d = x.shape
    tokens = batch * seq
    n_chunks = seq // CHUNK
    dm = d // 2
    da = d - dm
    heads = MLSTM_HEADS
    tl = _tiles(seq)
    x2 = x.reshape(tokens, d)
    mlp_w1_bf = mlp_w1.astype(BF16)
    mlp_w2_bf = mlp_w2.astype(BF16)

    w_in = mix_w_in[0]
    gate_lo = 4 * dm
    gate_hi = gate_lo + 2 * heads
    w_main = _drop_columns_bf16(w_in, gate_lo, gate_hi, tl["repack_rows"])
    w_gate = jnp.pad(w_in[:, gate_lo:gate_hi], ((0, 0), (0, LANES - 2 * heads))).astype(BF16)
    proj, gates = _in_proj(x2, _row(mixer_norm_g[0]), w_main, w_gate, tl["in_proj_tm"], w_main.shape[1] // 2)

    n_bh = batch * heads
    g8 = gates[:, :2 * heads].reshape(batch, n_chunks, CHUNK, 2, heads)
    g8 = jnp.transpose(g8, (3, 1, 0, 4, 2)).reshape(2, n_chunks * n_bh, CHUNK)
    g8 = jnp.pad(g8, ((0, 0), (0, 0), (0, LANES - CHUNK)))
    bias_rows = lambda bvec: jnp.broadcast_to(
        jnp.tile(bvec.astype(F32), batch * n_chunks)[:, None], (n_chunks * n_bh, LANES))
    r, mt, wi, et, ws, dc = _gate_prep(g8[0], g8[1], bias_rows(igate_b[0]), bias_rows(fgate_b[0]),
                                       n_chunks, n_bh)
    per_frame = jnp.stack([mt, wi, et, ws], axis=0)[:, :, :CHUNK]
    per_frame = per_frame.reshape(4, n_chunks, batch, heads, CHUNK)
    cols = jnp.transpose(per_frame, (2, 1, 4, 0, 3)).reshape(batch, n_chunks, CHUNK, 4 * heads)
    cols = jnp.pad(cols, ((0, 0), (0, 0), (0, 0), (0, LANES - 4 * heads)))
    rows = jnp.concatenate([r.reshape(n_chunks, batch, heads, LANES),
                            dc.reshape(n_chunks, batch, heads, LANES)], axis=2)
    rows = jnp.transpose(rows, (1, 0, 2, 3))

    h_a = _mlstm(proj, qk_conv_w[0].astype(F32), _row(qk_conv_b[0]), cols, rows,
                 _row(mlstm_norm_g[0]), batch, n_chunks, tl["mlstm_chunks_per_step"])

    tab = _bias_table(rel_bias[0].astype(F32), tl["attn_q_rows"])
    h_b = _attention(proj, tab, da, batch, seq, tl["attn_q_rows"])

    h, u = _out_proj(h_a, h_b, mix_w_out.astype(BF16).reshape(2, dm, d), x2, _row(mlp_norm_g[0]),
                     tl["out_proj_tm"])

    h, u = _mlp(h, u, mlp_w1_bf, mlp_w2_bf, 0, _row(mixer_norm_g[1]), False, tl["mlp_tm"], tl["mlp_tf"])

    z = _glu(u, conv_pw1_w.astype(BF16), _row(conv_pw1_b[0]), tl["glu_tm"], tl["glu_tn"])
    h, u = _conv_tail(z, conv_dw_w[0].astype(F32), _row(conv_dw_b[0]), _row(conv_ln_g[0]),
                      _row(conv_ln_b[0]), conv_pw2_w.astype(BF16), _row(conv_pw2_b[0]), h,
                      _row(mlp_norm_g[1]), seq, tl["conv_tm"])

    (out,) = _mlp(h, u, mlp_w1_bf, mlp_w2_bf, 1, _row(final_norm_g), True, tl["mlp_tm"], tl["mlp_tf"])
    return out.reshape(batch, seq, d)
```

```python
import functools
import math

import jax
import jax.numpy as jnp
from jax import lax
from jax.experimental import pallas as pl
from jax.experimental.pallas import tpu as pltpu

CHUNK = 64
MLSTM_HEADS = 4
MLSTM_CONV = 4
ATTN_HEADS = 8
LEFT_CHUNKS = 8
MAX_REL = 256
CONV_WIDTH = 31
EPS = 1e-6

V7X_VMEM_BYTES = 64 * 1024 * 1024
LANES = 128
BF16_SUBLANES = 16

F32 = jnp.float32
BF16 = jnp.bfloat16
NEG_INF = float("-inf")
_LOG2_E = math.log2(math.e)

_NT = (((1,), (1,)), ((), ()))
_TN = (((0,), (0,)), ((), ()))


def _params(semantics, vmem_mib):
    assert vmem_mib * 1024 * 1024 <= V7X_VMEM_BYTES
    return pltpu.CompilerParams(dimension_semantics=semantics,
                                vmem_limit_bytes=vmem_mib * 1024 * 1024)


def _rmsnorm_f32(x, g):
    return x * lax.rsqrt(jnp.mean(x * x, axis=-1, keepdims=True) + EPS) * g


def _sigmoid(x):
    return 0.5 * jnp.tanh(0.5 * x) + 0.5


def _drop_rows_kernel(w_ref, o_ref):
    o_ref[...] = w_ref[0].astype(BF16)


def _drop_rows_bf16(w_t, lo, hi, rows):
    n_in, d = w_t.shape[1:]
    n = n_in - (hi - lo)
    assert lo % rows == 0 and n % rows == 0 and (hi - lo) % 8 == 0
    return pl.pallas_call(
        _drop_rows_kernel,
        out_shape=jax.ShapeDtypeStruct((n, d), BF16),
        grid=(n // rows,),
        in_specs=[pl.BlockSpec((pl.Element(1), pl.Element(rows), pl.Element(d)),
                               lambda i: (0, pl.multiple_of(jnp.where(i * rows < lo, i * rows, i * rows + (hi - lo)), 8), 0))],
        out_specs=pl.BlockSpec((rows, d), lambda i: (i, 0)),
        compiler_params=_params(("arbitrary",), 32),
        name="w_in_repack",
    )(w_t)


def _in_proj_kernel(x_ref, g_ref, w_ref, wg_ref, o_ref, og_ref, u_ref):
    @pl.when(pl.program_id(1) == 0)
    def _():
        u = _rmsnorm_f32(x_ref[...], g_ref[...]).astype(BF16)
        u_ref[...] = u
        og_ref[...] = lax.dot_general(u, wg_ref[...], _NT, preferred_element_type=F32)

    o_ref[...] = lax.dot_general(u_ref[...], w_ref[...], _NT, preferred_element_type=F32).astype(o_ref.dtype)


def _in_proj(x2, g, w_main_t, w_gate_t, tm, tn):
    t, d = x2.shape
    n = w_main_t.shape[0]
    return pl.pallas_call(
        _in_proj_kernel,
        out_shape=(jax.ShapeDtypeStruct((t, n), BF16), jax.ShapeDtypeStruct((t, LANES), F32)),
        grid=(t // tm, n // tn),
        in_specs=[
            pl.BlockSpec((tm, d), lambda i, j: (i, 0)),
            pl.BlockSpec((1, d), lambda i, j: (0, 0)),
            pl.BlockSpec((tn, d), lambda i, j: (j, 0)),
            pl.BlockSpec((LANES, d), lambda i, j: (0, 0)),
        ],
        out_specs=(pl.BlockSpec((tm, tn), lambda i, j: (i, j)),
                   pl.BlockSpec((tm, LANES), lambda i, j: (i, 0))),
        scratch_shapes=[pltpu.VMEM((tm, d), BF16)],
        compiler_params=_params(("arbitrary", "arbitrary"), 56),
        name="in_proj",
    )(x2, g, w_main_t, w_gate_t)


def _lane_prefix(x, op, ident, lane):
    shift = 1
    while shift < CHUNK:
        y = pltpu.roll(x, shift, axis=1)
        x = op(x, jnp.where(lane >= shift, y, ident))
        shift *= 2
    return x


def _gate_prep_kernel(gi_ref, gf_ref, ib_ref, fb_ref,
                      r_ref, mt_ref, wi_ref, et_ref, ws_ref, dc_ref,
                      g_sc, c_sc, m_sc, *, n_chunks, n_bh):
    shape = gi_ref.shape
    lane = lax.broadcasted_iota(jnp.int32, shape, 1)
    valid = lane < CHUNK
    i_pre = gi_ref[...] + ib_ref[...]
    f_pre = gf_ref[...] + fb_ref[...]
    lf = jnp.minimum(f_pre, 0.0) - jnp.log1p(jnp.exp(-jnp.abs(f_pre)))
    lf = jnp.where(valid, lf, 0.0)
    b = _lane_prefix(lf, jnp.add, 0.0, lane)
    r = jnp.where(valid, i_pre - b, NEG_INF)
    cm = _lane_prefix(r, jnp.maximum, NEG_INF, lane)
    g_tot = jnp.sum(jnp.where(lane == CHUNK - 1, b, 0.0), axis=1, keepdims=True)
    c_last = jnp.max(r, axis=1, keepdims=True)
    g_sc[...] = jnp.broadcast_to(g_tot, shape)
    c_sc[...] = jnp.broadcast_to(c_last, shape)

    def body(c, m):
        rows = pl.ds(pl.multiple_of(c * n_bh, n_bh), n_bh)
        m_sc[rows, :] = m
        return g_sc[rows, :] + jnp.maximum(m, c_sc[rows, :])

    lax.fori_loop(0, n_chunks, body, jnp.zeros((n_bh, shape[1]), F32))
    m = m_sc[...]
    big_m = jnp.maximum(m, cm)
    m_last = jnp.maximum(m, c_sc[...])
    r_ref[...] = jnp.where(valid, r, 0.0)
    mt_ref[...] = big_m
    wi_ref[...] = jnp.exp(m - big_m)
    et_ref[...] = jnp.exp(-b - big_m)
    ws_ref[...] = jnp.exp(r - m_last)
    dc_ref[...] = jnp.exp(m - m_last)


def _gate_prep(gi, gf, ib, fb, n_chunks, n_bh):
    shape = jax.ShapeDtypeStruct(gi.shape, F32)
    return pl.pallas_call(
        functools.partial(_gate_prep_kernel, n_chunks=n_chunks, n_bh=n_bh),
        out_shape=(shape,) * 6,
        scratch_shapes=[pltpu.VMEM(gi.shape, F32)] * 3,
        compiler_params=_params(None, 32),
        name="gate_prep",
    )(gi, gf, ib, fb)


_QK_HALO = BF16_SUBLANES


def _mlstm_kernel(qk_ref, halo_ref, v_ref, o_ref, cw_ref, cb_ref, cols_ref, rows_ref, ng_ref,
                  out_ref, xs_ref, c_ref, n_ref, *, heads, dh, chunks_per_step):
    group_id = pl.program_id(1)
    dm = heads * dh
    n_slabs = 2 * dm // LANES

    @pl.when(group_id == 0)
    def _():
        c_ref[...] = jnp.zeros(c_ref.shape, F32)
        n_ref[...] = jnp.zeros(n_ref.shape, F32)

    for s in range(n_slabs):
        lanes = slice(s * LANES, (s + 1) * LANES)
        xs_ref[s, 0:_QK_HALO, :] = jnp.where(group_id == 0, 0.0, halo_ref[:, lanes].astype(F32))
        xs_ref[s, _QK_HALO:, :] = qk_ref[:, lanes].astype(F32)

    def conv_silu(col0, row0):
        parts = []
        for s in range(col0 // LANES, (col0 + dh) // LANES):
            lanes = slice(s * LANES, (s + 1) * LANES)
            acc = cb_ref[:, lanes]
            for j in range(MLSTM_CONV):
                off = row0 + _QK_HALO - (MLSTM_CONV - 1) + j
                acc = acc + cw_ref[j:j + 1, lanes] * xs_ref[s, off:off + CHUNK, :]
            parts.append(acc * _sigmoid(acc))
        return jnp.concatenate(parts, axis=1)

    tri = (lax.broadcasted_iota(jnp.int32, (CHUNK, CHUNK), 0)
           >= lax.broadcasted_iota(jnp.int32, (CHUNK, CHUNK), 1))
    eye = (lax.broadcasted_iota(jnp.int32, (dh, dh), 0)
           == lax.broadcasted_iota(jnp.int32, (dh, dh), 1)).astype(BF16)
    for cc in range(chunks_per_step):
        row0 = cc * CHUNK
        rs = slice(row0, row0 + CHUNK)
        for h in range(heads):
            hs = slice(h * dh, (h + 1) * dh)
            q = conv_silu(h * dh, row0)
            k = conv_silu(dm + h * dh, row0) * (dh ** -0.5)
            qb = q.astype(BF16)
            kb = k.astype(BF16)
            v = v_ref[rs, hs]
            mt = cols_ref[cc, :, h:h + 1]
            wi = cols_ref[cc, :, heads + h:heads + h + 1]
            et = cols_ref[cc, :, 2 * heads + h:2 * heads + h + 1]
            ws = cols_ref[cc, :, 3 * heads + h:3 * heads + h + 1]
            r = rows_ref[cc, h:h + 1, 0:CHUNK]
            dc = rows_ref[cc, heads + h:heads + h + 1, 0:1]

            s = lax.dot_general(qb, kb, _NT, preferred_element_type=F32)
            sw = s * jnp.where(tri, jnp.exp(r - mt), 0.0)
            c_old = c_ref[h]
            inter = lax.dot_general(qb, c_old.astype(BF16), _NT, preferred_element_type=F32)
            intra = jnp.dot(sw.astype(BF16), v, preferred_element_type=F32)
            n_old = n_ref[h:h + 1, :]
            num = intra + wi * inter
            den = (jnp.sum(sw, axis=-1, keepdims=True)
                   + wi * jnp.sum(q * n_old, axis=-1, keepdims=True))
            hh = num / jnp.maximum(jnp.abs(den), et)
            hn = hh * lax.rsqrt(jnp.mean(hh * hh, axis=-1, keepdims=True) + EPS) * ng_ref[:, hs]
            out_ref[rs, hs] = (_sigmoid(o_ref[rs, hs].astype(F32)) * hn).astype(out_ref.dtype)

            vw = (v.astype(F32) * ws).astype(BF16)
            vw_t = lax.dot_general(eye, vw, _NT, preferred_element_type=F32).astype(BF16)
            upd = jnp.dot(vw_t, kb, preferred_element_type=F32)
            c_ref[h] = dc * c_old + upd
            n_ref[h:h + 1, :] = dc * n_old + jnp.sum(k * ws, axis=0, keepdims=True)


def _mlstm(proj, conv_w, conv_b, cols, rows, norm_g, batch, n_chunks, chunks_per_step):
    t = proj.shape[0]
    dm = norm_g.shape[1]
    dh = dm // MLSTM_HEADS
    n_groups = n_chunks // chunks_per_step
    rows_per_step = chunks_per_step * CHUNK
    halo_blocks = rows_per_step // _QK_HALO
    return pl.pallas_call(
        functools.partial(_mlstm_kernel, heads=MLSTM_HEADS, dh=dh, chunks_per_step=chunks_per_step),
        out_shape=jax.ShapeDtypeStruct((t, dm), BF16),
        grid=(batch, n_groups),
        in_specs=[
            pl.BlockSpec((rows_per_step, 2 * dm), lambda b, c: (b * n_groups + c, 0)),
            pl.BlockSpec((_QK_HALO, 2 * dm),
                         lambda b, c: (jnp.maximum((b * n_groups + c) * halo_blocks - 1, 0), 0)),
            pl.BlockSpec((rows_per_step, dm), lambda b, c: (b * n_groups + c, 2)),
            pl.BlockSpec((rows_per_step, dm), lambda b, c: (b * n_groups + c, 3)),
            pl.BlockSpec((MLSTM_CONV, 2 * dm), lambda b, c: (0, 0)),
            pl.BlockSpec((1, 2 * dm), lambda b, c: (0, 0)),
            pl.BlockSpec((None, chunks_per_step, CHUNK, LANES), lambda b, c: (b, c, 0, 0)),
            pl.BlockSpec((None, chunks_per_step, 2 * MLSTM_HEADS, LANES), lambda b, c: (b, c, 0, 0)),
            pl.BlockSpec((1, dm), lambda b, c: (0, 0)),
        ],
        out_specs=pl.BlockSpec((rows_per_step, dm), lambda b, c: (b * n_groups + c, 0)),
        scratch_shapes=[
            pltpu.VMEM((2 * dm // LANES, _QK_HALO + rows_per_step, LANES), F32),
            pltpu.VMEM((MLSTM_HEADS, dh, dh), F32),
            pltpu.VMEM((2 * MLSTM_HEADS, dh), F32),
        ],
        compiler_params=_params(("arbitrary", "arbitrary"), 32),
        name="mlstm",
    )(proj, proj, proj, proj, conv_w, conv_b, cols, rows, norm_g)


def _attn_kernel(*refs, heads, dh, n_pieces):
    q_ref = refs[0]
    k_refs = refs[1:1 + n_pieces]
    v_refs = refs[1 + n_pieces:1 + 2 * n_pieces]
    tab_refs = refs[1 + 2 * n_pieces:1 + 3 * n_pieces]
    out_ref = refs[1 + 3 * n_pieces]
    scale2 = (dh ** -0.5) * _LOG2_E
    for h in range(heads):
        hs = slice(h * dh, (h + 1) * dh)
        q = q_ref[:, hs]
        scores = [lax.dot_general(q, k_ref[:, hs], _NT, preferred_element_type=F32) * scale2 + tab_ref[h]
                  for k_ref, tab_ref in zip(k_refs, tab_refs)]
        m = jnp.max(functools.reduce(jnp.maximum, scores), axis=-1, keepdims=True)
        es = [jnp.exp2(s - m) for s in scores]
        denom = jnp.sum(functools.reduce(jnp.add, es), axis=-1, keepdims=True)
        acc = functools.reduce(jnp.add, [
            jnp.dot(e.astype(BF16), v_ref[:, hs], preferred_element_type=F32) for e, v_ref in zip(es, v_refs)])
        out_ref[:, hs] = (acc / denom).astype(out_ref.dtype)


def _attention(proj, tab, d_attn, batch, seq, q_rows):
    t = proj.shape[0]
    dh = d_attn // ATTN_HEADS
    n_q = seq // q_rows
    pad_blocks = (LEFT_CHUNKS * CHUNK) // q_rows
    n_pieces = 1 + pad_blocks
    assert tab.shape == (n_pieces + 1, ATTN_HEADS, q_rows, q_rows)
    q_col = proj.shape[1] // d_attn - 3

    def window_spec(col_block, p):
        return pl.BlockSpec(
            (q_rows, d_attn),
            lambda b, qi: (b * n_q + jnp.maximum(qi + p - pad_blocks, 0), col_block))

    def tab_spec(p):
        return pl.BlockSpec(
            (None, ATTN_HEADS, q_rows, q_rows),
            lambda b, qi: (jnp.where(qi + p >= pad_blocks, p, n_pieces), 0, 0, 0))

    return pl.pallas_call(
        functools.partial(_attn_kernel, heads=ATTN_HEADS, dh=dh, n_pieces=n_pieces),
        out_shape=jax.ShapeDtypeStruct((t, d_attn), BF16),
        grid=(batch, n_q),
        in_specs=[pl.BlockSpec((q_rows, d_attn), lambda b, qi: (b * n_q + qi, q_col))]
        + [window_spec(q_col + 1, p) for p in range(n_pieces)]
        + [window_spec(q_col + 2, p) for p in range(n_pieces)]
        + [tab_spec(p) for p in range(n_pieces)],
        out_specs=pl.BlockSpec((q_rows, d_attn), lambda b, qi: (b * n_q + qi, 0)),
        compiler_params=_params(("arbitrary", "arbitrary"), 40),
        name="chunk_attn",
    )(proj, *([proj] * (2 * n_pieces)), *([tab] * n_pieces))


def _bias_table(rel, q_rows):
    heads = rel.shape[0]
    pad_rows = LEFT_CHUNKS * CHUNK
    width = q_rows + pad_rows
    d_lo, d_hi = pad_rows - width + 1, pad_rows + q_rows - 1
    mid = rel[:, max(d_lo, -MAX_REL) + MAX_REL:min(d_hi, MAX_REL) + MAX_REL + 1]
    left = jnp.repeat(rel[:, :1], max(0, -MAX_REL - d_lo), axis=1)
    right = jnp.repeat(rel[:, -1:], max(0, d_hi - MAX_REL), axis=1)
    by_dist = jnp.concatenate([left, mid, right], axis=1)[:, ::-1]
    length = by_dist.shape[1]
    ring = jnp.pad(by_dist, ((0, 0), (0, 1)))
    skew = jnp.tile(ring, (1, q_rows))[:, :q_rows * length].reshape(heads, q_rows, length)
    bias = skew[:, :, q_rows - 1:q_rows - 1 + width]
    q_chunk = jnp.arange(q_rows)[:, None] // CHUNK
    k_chunk = jnp.arange(width)[None, :] // CHUNK
    in_band = (k_chunk >= q_chunk) & (k_chunk <= q_chunk + LEFT_CHUNKS)
    tab = jnp.where(in_band[None], bias * _LOG2_E, NEG_INF)
    pieces = [tab[:, :, p * q_rows:(p + 1) * q_rows] for p in range(width // q_rows)]
    pieces.append(jnp.full((heads, q_rows, q_rows), NEG_INF, F32))
    return jnp.stack(pieces, axis=0)


def _out_proj_kernel(a_ref, b_ref, wa_ref, wb_ref, x_ref, g_ref, o_ref, u_ref):
    acc = jnp.dot(a_ref[...], wa_ref[...], preferred_element_type=F32)
    acc = acc + jnp.dot(b_ref[...], wb_ref[...], preferred_element_type=F32)
    h = x_ref[...] + acc
    o_ref[...] = h
    u_ref[...] = _rmsnorm_f32(h, g_ref[...]).astype(BF16)


def _out_proj(h_a, h_b, w, x2, g_next, tm):
    t, d = x2.shape
    ka, kb = h_a.shape[1], h_b.shape[1]
    assert ka == kb
    return pl.pallas_call(
        _out_proj_kernel,
        out_shape=(jax.ShapeDtypeStruct((t, d), F32), jax.ShapeDtypeStruct((t, d), BF16)),
        grid=(t // tm,),
        in_specs=[
            pl.BlockSpec((tm, ka), lambda i: (i, 0)),
            pl.BlockSpec((tm, kb), lambda i: (i, 0)),
            pl.BlockSpec((None, ka, d), lambda i: (0, 0, 0)),
            pl.BlockSpec((None, kb, d), lambda i: (1, 0, 0)),
            pl.BlockSpec((tm, d), lambda i: (i, 0)),
            pl.BlockSpec((1, d), lambda i: (0, 0)),
        ],
        out_specs=(pl.BlockSpec((tm, d), lambda i: (i, 0)), pl.BlockSpec((tm, d), lambda i: (i, 0))),
        compiler_params=_params(("arbitrary",), 56),
        name="out_proj",
    )(h_a, h_b, w, w, x2, g_next)


def _mlp_kernel(h_ref, u_ref, w1_ref, w2_ref, gp_ref, *out_refs, final_norm):
    o_ref = out_refs[0]
    j = pl.program_id(1)

    @pl.when(j == 0)
    def _():
        o_ref[...] = h_ref[...]

    a = jnp.dot(u_ref[...], w1_ref[...], preferred_element_type=F32)
    a = jnp.square(jnp.maximum(a, 0.0)).astype(BF16)
    o_ref[...] += jnp.dot(a, w2_ref[...], preferred_element_type=F32)

    @pl.when(j == pl.num_programs(1) - 1)
    def _():
        normed = _rmsnorm_f32(o_ref[...], gp_ref[...])
        if final_norm:
            o_ref[...] = normed
        else:
            out_refs[1][...] = normed.astype(BF16)


def _mlp(h2, u, w1, w2, layer, g_post, final_norm, tm, tf):
    t, d = h2.shape
    f = w1.shape[2]
    row_tile = pl.BlockSpec((tm, d), lambda i, j: (i, 0))
    out_shape = [jax.ShapeDtypeStruct((t, d), F32)]
    if not final_norm:
        out_shape.append(jax.ShapeDtypeStruct((t, d), BF16))
    return pl.pallas_call(
        functools.partial(_mlp_kernel, final_norm=final_norm),
        out_shape=tuple(out_shape),
        grid=(t // tm, f // tf),
        in_specs=[
            row_tile,
            row_tile,
            pl.BlockSpec((None, d, tf), lambda i, j: (layer, 0, j)),
            pl.BlockSpec((None, tf, d), lambda i, j: (layer, j, 0)),
            pl.BlockSpec((1, d), lambda i, j: (0, 0)),
        ],
        out_specs=tuple([row_tile] * len(out_shape)),
        compiler_params=_params(("arbitrary", "arbitrary"), 56),
        name="mlp_final" if final_norm else "mlp",
    )(h2, u, w1, w2, g_post)


def _glu_kernel(u_ref, wa_ref, wg_ref, ba_ref, bg_ref, o_ref):
    u = u_ref[...]
    a = jnp.dot(u, wa_ref[...], preferred_element_type=F32) + ba_ref[...]
    gate = jnp.dot(u, wg_ref[...], preferred_element_type=F32) + bg_ref[...]
    o_ref[...] = (a * _sigmoid(gate)).astype(o_ref.dtype)


def _glu(u, pw1, pw1_b, tm, tn):
    t, d = u.shape
    n = pw1.shape[2] // 2
    nb = n // tn
    return pl.pallas_call(
        _glu_kernel,
        out_shape=jax.ShapeDtypeStruct((t, n), BF16),
        grid=(t // tm, nb),
        in_specs=[
            pl.BlockSpec((tm, d), lambda i, j: (i, 0)),
            pl.BlockSpec((None, d, tn), lambda i, j: (0, 0, j)),
            pl.BlockSpec((None, d, tn), lambda i, j: (0, 0, j + nb)),
            pl.BlockSpec((1, tn), lambda i, j: (0, j)),
            pl.BlockSpec((1, tn), lambda i, j: (0, j + nb)),
        ],
        out_specs=pl.BlockSpec((tm, tn), lambda i, j: (i, j)),
        compiler_params=_params(("arbitrary", "arbitrary"), 56),
        name="conv_glu",
    )(u, pw1, pw1, pw1_b, pw1_b)


_DW_HALO = 2 * BF16_SUBLANES
_DW_ROWS = 64


def _conv_tail_kernel(z_ref, halo_ref, dw_ref, dwb_ref, lng_ref, lnb_ref, w2_ref, b2_ref, h_ref, g_ref,
                      o_ref, u_ref, zs_ref, y_ref, *, tm, tiles_per_seq):
    i = pl.program_id(0)
    n_slabs = z_ref.shape[1] // LANES
    seq_start = (i % tiles_per_seq) == 0
    for s in range(n_slabs):
        lanes = slice(s * LANES, (s + 1) * LANES)
        zs_ref[s, 0:_DW_HALO, :] = jnp.where(seq_start, 0.0, halo_ref[:, lanes].astype(F32))
        zs_ref[s, _DW_HALO:, :] = z_ref[:, lanes].astype(F32)
    first_tap = _DW_HALO - (CONV_WIDTH - 1)

    def slab_body(s, carry):
        lanes = pl.ds(pl.multiple_of(s * LANES, LANES), LANES)
        for rb in range(tm // _DW_ROWS):
            r0 = rb * _DW_ROWS
            acc = jnp.broadcast_to(dwb_ref[:, lanes], (_DW_ROWS, LANES))
            for k in range(CONV_WIDTH):
                acc = acc + dw_ref[k:k + 1, lanes] * zs_ref[s, r0 + first_tap + k:r0 + first_tap + k + _DW_ROWS, :]
            y_ref[r0:r0 + _DW_ROWS, lanes] = acc
        return carry

    lax.fori_loop(0, n_slabs, slab_body, 0)
    y = y_ref[...]
    mu = jnp.mean(y, axis=-1, keepdims=True)
    yc = y - mu
    var = jnp.mean(yc * yc, axis=-1, keepdims=True)
    yn = yc * lax.rsqrt(var + EPS) * lng_ref[...] + lnb_ref[...]
    a = (yn * _sigmoid(yn)).astype(BF16)
    h = h_ref[...] + jnp.dot(a, w2_ref[...], preferred_element_type=F32) + b2_ref[...]
    o_ref[...] = h
    u_ref[...] = _rmsnorm_f32(h, g_ref[...]).astype(BF16)


def _conv_tail(z, dw_w, dw_b, ln_g, ln_b, pw2, pw2_b, h2, g_next, seq, tm):
    t, d = h2.shape
    halo_blocks = tm // _DW_HALO
    row_tile = pl.BlockSpec((tm, d), lambda i: (i, 0))
    vec = pl.BlockSpec((1, d), lambda i: (0, 0))
    return pl.pallas_call(
        functools.partial(_conv_tail_kernel, tm=tm, tiles_per_seq=seq // tm),
        out_shape=(jax.ShapeDtypeStruct((t, d), F32), jax.ShapeDtypeStruct((t, d), BF16)),
        grid=(t // tm,),
        in_specs=[
            row_tile,
            pl.BlockSpec((_DW_HALO, d), lambda i: (jnp.maximum(i * halo_blocks - 1, 0), 0)),
            pl.BlockSpec(dw_w.shape, lambda i: (0, 0)),
            vec, vec, vec,
            pl.BlockSpec((None, d, d), lambda i: (0, 0, 0), pipeline_mode=pl.Buffered(1)),
            vec,
            row_tile,
            vec,
        ],
        out_specs=(row_tile, row_tile),
        scratch_shapes=[pltpu.VMEM((d // LANES, _DW_HALO + tm, LANES), F32), pltpu.VMEM((tm, d), F32)],
        compiler_params=_params(("arbitrary",), 56),
        name="conv_tail",
    )(z, z, dw_w, dw_b, ln_g, ln_b, pw2, pw2_b, h2, g_next)


def _tiles(seq):
    def rows(want):
        return min(want, seq)

    return dict(
        repack_rows=512,
        in_proj_tm=rows(512),
        mlstm_chunks_per_step=min(4, seq // CHUNK),
        attn_q_rows=rows(256),
        out_proj_tm=rows(512),
        mlp_tm=rows(512), mlp_tf=1024,
        glu_tm=rows(1024), glu_tn=1024,
        conv_tm=rows(512),
    )


def _row(v):
    return v.reshape(1, -1).astype(F32)


def kernel(x, mixer_norm_g, mix_w_in, qk_conv_w, qk_conv_b, igate_b, fgate_b, mlstm_norm_g, rel_bias,
           mix_w_out, conv_pw1_w, conv_pw1_b, conv_dw_w, conv_dw_b, conv_ln_g, conv_ln_b, conv_pw2_w,
           conv_pw2_b, mlp_norm_g, mlp_w1, mlp_w2, final_norm_g):
    batch, seq, d = x.shape
    tokens = batch * seq
    n_chunks = seq // CHUNK
    dm = d // 2
    da = d - dm
    heads = MLSTM_HEADS
    tl = _tiles(seq)
    x2 = x.reshape(tokens, d)
    mlp_w1_bf = mlp_w1.astype(BF16)
    mlp_w2_bf = mlp_w2.astype(BF16)

    w_in_t = jnp.swapaxes(mix_w_in, 1, 2)
    gate_lo = 4 * dm
    gate_hi = gate_lo + 2 * heads
    w_main_t = _drop_rows_bf16(w_in_t, gate_lo, gate_hi, tl["repack_rows"])
    w_gate_t = jnp.pad(w_in_t[0, gate_lo:gate_hi], ((0, LANES - 2 * heads), (0, 0))).astype(BF16)
    proj, gates = _in_proj(x2, _row(mixer_norm_g[0]), w_main_t, w_gate_t, tl["in_proj_tm"],
                           w_main_t.shape[0] // 2)

    n_bh = batch * heads
    g8 = gates[:, :2 * heads].reshape(batch, n_chunks, CHUNK, 2, heads)
    g8 = jnp.transpose(g8, (3, 1, 0, 4, 2)).reshape(2, n_chunks * n_bh, CHUNK)
    g8 = jnp.pad(g8, ((0, 0), (0, 0), (0, LANES - CHUNK)))
    bias_rows = lambda bvec: jnp.broadcast_to(
        jnp.tile(bvec.astype(F32), batch * n_chunks)[:, None], (n_chunks * n_bh, LANES))
    r, mt, wi, et, ws, dc = _gate_prep(g8[0], g8[1], bias_rows(igate_b[0]), bias_rows(fgate_b[0]),
                                       n_chunks, n_bh)
    per_frame = jnp.stack([mt, wi, et, ws], axis=0)[:, :, :CHUNK]
    per_frame = per_frame.reshape(4, n_chunks, batch, heads, CHUNK)
    cols = jnp.transpose(per_frame, (2, 1, 4, 0, 3)).reshape(batch, n_chunks, CHUNK, 4 * heads)
    cols = jnp.pad(cols, ((0, 0), (0, 0), (0, 0), (0, LANES - 4 * heads)))
    rows = jnp.concatenate([r.reshape(n_chunks, batch, heads, LANES),
                            dc.reshape(n_chunks, batch, heads, LANES)], axis=2)
    rows = jnp.transpose(rows, (1, 0, 2, 3))

    h_a = _mlstm(proj, qk_conv_w[0].astype(F32), _row(qk_conv_b[0]), cols, rows,
                 _row(mlstm_norm_g[0]), batch, n_chunks, tl["mlstm_chunks_per_step"])

    tab = _bias_table(rel_bias[0].astype(F32), tl["attn_q_rows"])
    h_b = _attention(proj, tab, da, batch, seq, tl["attn_q_rows"])

    h, u = _out_proj(h_a, h_b, mix_w_out.astype(BF16).reshape(2, dm, d), x2, _row(mlp_norm_g[0]),
                     tl["out_proj_tm"])

    h, u = _mlp(h, u, mlp_w1_bf, mlp_w2_bf, 0, _row(mixer_norm_g[1]), False, tl["mlp_tm"], tl["mlp_tf"])

    z = _glu(u, conv_pw1_w.astype(BF16), _row(conv_pw1_b[0]), tl["glu_tm"], tl["glu_tn"])
    h, u = _conv_tail(z, conv_dw_w[0].astype(F32), _row(conv_dw_b[0]), _row(conv_ln_g[0]),
                      _row(conv_ln_b[0]), conv_pw2_w.astype(BF16), _row(conv_pw2_b[0]), h,
                      _row(mlp_norm_g[1]), seq, tl["conv_tm"])

    (out,) = _mlp(h, u, mlp_w1_bf, mlp_w2_bf, 1, _row(final_norm_g), True, tl["mlp_tm"], tl["mlp_tf"])
    return out.reshape(batch, seq, d)
```

```python
import functools
import math

import jax
import jax.numpy as jnp
from jax import lax
from jax.experimental import pallas as pl
from jax.experimental.pallas import tpu as pltpu

CHUNK = 64
MLSTM_HEADS = 4
MLSTM_CONV = 4
ATTN_HEADS = 8
LEFT_CHUNKS = 8
MAX_REL = 256
CONV_WIDTH = 31
EPS = 1e-6

V7X_VMEM_BYTES = 64 * 1024 * 1024
LANES = 128
BF16_SUBLANES = 16

F32 = jnp.float32
BF16 = jnp.bfloat16
NEG_INF = float("-inf")
_LOG2_E = math.log2(math.e)

_NT = (((1,), (1,)), ((), ()))
_TN = (((0,), (0,)), ((), ()))


def _params(semantics, vmem_mib):
    assert vmem_mib * 1024 * 1024 <= V7X_VMEM_BYTES
    return pltpu.CompilerParams(dimension_semantics=semantics,
                                vmem_limit_bytes=vmem_mib * 1024 * 1024)


def _rmsnorm_f32(x, g):
    return x * lax.rsqrt(jnp.mean(x * x, axis=-1, keepdims=True) + EPS) * g


def _sigmoid(x):
    return 0.5 * jnp.tanh(0.5 * x) + 0.5


def _drop_rows_kernel(w_ref, o_ref):
    o_ref[...] = w_ref[0].astype(BF16)


def _drop_rows_bf16(w_t, lo, hi, rows):
    n_in, d = w_t.shape[1:]
    n = n_in - (hi - lo)
    assert lo % rows == 0 and n % rows == 0 and (hi - lo) % 8 == 0
    return pl.pallas_call(
        _drop_rows_kernel,
        out_shape=jax.ShapeDtypeStruct((n, d), BF16),
        grid=(n // rows,),
        in_specs=[pl.BlockSpec((pl.Element(1), pl.Element(rows), pl.Element(d)),
                               lambda i: (0, pl.multiple_of(jnp.where(i * rows < lo, i * rows, i * rows + (hi - lo)), 8), 0))],
        out_specs=pl.BlockSpec((rows, d), lambda i: (i, 0)),
        compiler_params=_params(("arbitrary",), 32),
        name="w_in_repack",
    )(w_t)


def _in_proj_kernel(x_ref, g_ref, w_ref, wg_ref, o_ref, og_ref, u_ref):
    @pl.when(pl.program_id(1) == 0)
    def _():
        u = _rmsnorm_f32(x_ref[...], g_ref[...]).astype(BF16)
        u_ref[...] = u
        og_ref[...] = lax.dot_general(u, wg_ref[...], _NT, preferred_element_type=F32)

    o_ref[...] = lax.dot_general(u_ref[...], w_ref[...], _NT, preferred_element_type=F32).astype(o_ref.dtype)


def _in_proj(x2, g, w_main_t, w_gate_t, tm, tn):
    t, d = x2.shape
    n = w_main_t.shape[0]
    return pl.pallas_call(
        _in_proj_kernel,
        out_shape=(jax.ShapeDtypeStruct((t, n), BF16), jax.ShapeDtypeStruct((t, LANES), F32)),
        grid=(t // tm, n // tn),
        in_specs=[
            pl.BlockSpec((tm, d), lambda i, j: (i, 0)),
            pl.BlockSpec((1, d), lambda i, j: (0, 0)),
            pl.BlockSpec((tn, d), lambda i, j: (j, 0)),
            pl.BlockSpec((LANES, d), lambda i, j: (0, 0)),
        ],
        out_specs=(pl.BlockSpec((tm, tn), lambda i, j: (i, j)),
                   pl.BlockSpec((tm, LANES), lambda i, j: (i, 0))),
        scratch_shapes=[pltpu.VMEM((tm, d), BF16)],
        compiler_params=_params(("arbitrary", "arbitrary"), 56),
        name="in_proj",
    )(x2, g, w_main_t, w_gate_t)


def _lane_prefix(x, op, ident, lane):
    shift = 1
    while shift < CHUNK:
        y = pltpu.roll(x, shift, axis=1)
        x = op(x, jnp.where(lane >= shift, y, ident))
        shift *= 2
    return x


def _gate_prep_kernel(gi_ref, gf_ref, ib_ref, fb_ref,
                      r_ref, mt_ref, wi_ref, et_ref, ws_ref, dc_ref,
                      g_sc, c_sc, m_sc, *, n_chunks, n_bh):
    shape = gi_ref.shape
    lane = lax.broadcasted_iota(jnp.int32, shape, 1)
    valid = lane < CHUNK
    i_pre = gi_ref[...] + ib_ref[...]
    f_pre = gf_ref[...] + fb_ref[...]
    lf = jnp.minimum(f_pre, 0.0) - jnp.log1p(jnp.exp(-jnp.abs(f_pre)))
    lf = jnp.where(valid, lf, 0.0)
    b = _lane_prefix(lf, jnp.add, 0.0, lane)
    r = jnp.where(valid, i_pre - b, NEG_INF)
    cm = _lane_prefix(r, jnp.maximum, NEG_INF, lane)
    g_tot = jnp.sum(jnp.where(lane == CHUNK - 1, b, 0.0), axis=1, keepdims=True)
    c_last = jnp.max(r, axis=1, keepdims=True)
    g_sc[...] = jnp.broadcast_to(g_tot, shape)
    c_sc[...] = jnp.broadcast_to(c_last, shape)

    def body(c, m):
        rows = pl.ds(pl.multiple_of(c * n_bh, n_bh), n_bh)
        m_sc[rows, :] = m
        return g_sc[rows, :] + jnp.maximum(m, c_sc[rows, :])

    lax.fori_loop(0, n_chunks, body, jnp.zeros((n_bh, shape[1]), F32))
    m = m_sc[...]
    big_m = jnp.maximum(m, cm)
    m_last = jnp.maximum(m, c_sc[...])
    r_ref[...] = jnp.where(valid, r, 0.0)
    mt_ref[...] = big_m
    wi_ref[...] = jnp.exp(m - big_m)
    et_ref[...] = jnp.exp(-b - big_m)
    ws_ref[...] = jnp.exp(r - m_last)
    dc_ref[...] = jnp.exp(m - m_last)


def _gate_prep(gi, gf, ib, fb, n_chunks, n_bh):
    shape = jax.ShapeDtypeStruct(gi.shape, F32)
    return pl.pallas_call(
        functools.partial(_gate_prep_kernel, n_chunks=n_chunks, n_bh=n_bh),
        out_shape=(shape,) * 6,
        scratch_shapes=[pltpu.VMEM(gi.shape, F32)] * 3,
        compiler_params=_params(None, 32),
        name="gate_prep",
    )(gi, gf, ib, fb)


_QK_HALO = BF16_SUBLANES


def _mlstm_kernel(qk_ref, halo_ref, v_ref, o_ref, cw_ref, cb_ref, cols_ref, rows_ref, ng_ref,
                  out_ref, xs_ref, c_ref, n_ref, *, heads, dh, chunks_per_step):
    group_id = pl.program_id(1)
    dm = heads * dh
    n_slabs = 2 * dm // LANES

    @pl.when(group_id == 0)
    def _():
        c_ref[...] = jnp.zeros(c_ref.shape, F32)
        n_ref[...] = jnp.zeros(n_ref.shape, F32)

    for s in range(n_slabs):
        lanes = slice(s * LANES, (s + 1) * LANES)
        xs_ref[s, 0:_QK_HALO, :] = jnp.where(group_id == 0, 0.0, halo_ref[:, lanes].astype(F32))
        xs_ref[s, _QK_HALO:, :] = qk_ref[:, lanes].astype(F32)

    def conv_silu(col0, row0):
        parts = []
        for s in range(col0 // LANES, (col0 + dh) // LANES):
            lanes = slice(s * LANES, (s + 1) * LANES)
            acc = cb_ref[:, lanes]
            for j in range(MLSTM_CONV):
                off = row0 + _QK_HALO - (MLSTM_CONV - 1) + j
                acc = acc + cw_ref[j:j + 1, lanes] * xs_ref[s, off:off + CHUNK, :]
            parts.append(acc * _sigmoid(acc))
        return jnp.concatenate(parts, axis=1)

    tri = (lax.broadcasted_iota(jnp.int32, (CHUNK, CHUNK), 0)
           >= lax.broadcasted_iota(jnp.int32, (CHUNK, CHUNK), 1))
    eye = (lax.broadcasted_iota(jnp.int32, (dh, dh), 0)
           == lax.broadcasted_iota(jnp.int32, (dh, dh), 1)).astype(BF16)
    for cc in range(chunks_per_step):
        row0 = cc * CHUNK
        rs = slice(row0, row0 + CHUNK)
        for h in range(heads):
            hs = slice(h * dh, (h + 1) * dh)
            q = conv_silu(h * dh, row0)
            k = conv_silu(dm + h * dh, row0) * (dh ** -0.5)
            qb = q.astype(BF16)
            kb = k.astype(BF16)
            v = v_ref[rs, hs]
            mt = cols_ref[cc, :, h:h + 1]
            wi = cols_ref[cc, :, heads + h:heads + h + 1]
            et = cols_ref[cc, :, 2 * heads + h:2 * heads + h + 1]
            ws = cols_ref[cc, :, 3 * heads + h:3 * heads + h + 1]
            r = rows_ref[cc, h:h + 1, 0:CHUNK]
            dc = rows_ref[cc, heads + h:heads + h + 1, 0:1]

            s = lax.dot_general(qb, kb, _NT, preferred_element_type=F32)
            sw = s * jnp.where(tri, jnp.exp(r - mt), 0.0)
            c_old = c_ref[h]
            inter = lax.dot_general(qb, c_old.astype(BF16), _NT, preferred_element_type=F32)
            intra = jnp.dot(sw.astype(BF16), v, preferred_element_type=F32)
            n_old = n_ref[h:h + 1, :]
            num = intra + wi * inter
            den = (jnp.sum(sw, axis=-1, keepdims=True)
                   + wi * jnp.sum(q * n_old, axis=-1, keepdims=True))
            hh = num / jnp.maximum(jnp.abs(den), et)
            hn = hh * lax.rsqrt(jnp.mean(hh * hh, axis=-1, keepdims=True) + EPS) * ng_ref[:, hs]
            out_ref[rs, hs] = (_sigmoid(o_ref[rs, hs].astype(F32)) * hn).astype(out_ref.dtype)

            vw = (v.astype(F32) * ws).astype(BF16)
            vw_t = lax.dot_general(eye, vw, _NT, preferred_element_type=F32).astype(BF16)
            upd = jnp.dot(vw_t, kb, preferred_element_type=F32)
            c_ref[h] = dc * c_old + upd
            n_ref[h:h + 1, :] = dc * n_old + jnp.sum(k * ws, axis=0, keepdims=True)


def _mlstm(proj, conv_w, conv_b, cols, rows, norm_g, batch, n_chunks, chunks_per_step):
    t = proj.shape[0]
    dm = norm_g.shape[1]
    dh = dm // MLSTM_HEADS
    n_groups = n_chunks // chunks_per_step
    rows_per_step = chunks_per_step * CHUNK
    halo_blocks = rows_per_step // _QK_HALO
    return pl.pallas_call(
        functools.partial(_mlstm_kernel, heads=MLSTM_HEADS, dh=dh, chunks_per_step=chunks_per_step),
        out_shape=jax.ShapeDtypeStruct((t, dm), BF16),
        grid=(batch, n_groups),
        in_specs=[
            pl.BlockSpec((rows_per_step, 2 * dm), lambda b, c: (b * n_groups + c, 0)),
            pl.BlockSpec((_QK_HALO, 2 * dm),
                         lambda b, c: (jnp.maximum((b * n_groups + c) * halo_blocks - 1, 0), 0)),
            pl.BlockSpec((rows_per_step, dm), lambda b, c: (b * n_groups + c, 2)),
            pl.BlockSpec((rows_per_step, dm), lambda b, c: (b * n_groups + c, 3)),
            pl.BlockSpec((MLSTM_CONV, 2 * dm), lambda b, c: (0, 0)),
            pl.BlockSpec((1, 2 * dm), lambda b, c: (0, 0)),
            pl.BlockSpec((None, chunks_per_step, CHUNK, LANES), lambda b, c: (b, c, 0, 0)),
            pl.BlockSpec((None, chunks_per_step, 2 * MLSTM_HEADS, LANES), lambda b, c: (b, c, 0, 0)),
            pl.BlockSpec((1, dm), lambda b, c: (0, 0)),
        ],
        out_specs=pl.BlockSpec((rows_per_step, dm), lambda b, c: (b * n_groups + c, 0)),
        scratch_shapes=[
            pltpu.VMEM((2 * dm // LANES, _QK_HALO + rows_per_step, LANES), F32),
            pltpu.VMEM((MLSTM_HEADS, dh, dh), F32),
            pltpu.VMEM((2 * MLSTM_HEADS, dh), F32),
        ],
        compiler_params=_params(("arbitrary", "arbitrary"), 32),
        name="mlstm",
    )(proj, proj, proj, proj, conv_w, conv_b, cols, rows, norm_g)


def _attn_kernel(*refs, heads, dh, n_pieces, n_cast):
    q_ref = refs[0]
    k_refs = refs[1:1 + n_pieces]
    v_refs = refs[1 + n_pieces:1 + 2 * n_pieces]
    tab_refs = refs[1 + 2 * n_pieces:1 + 3 * n_pieces]
    cast_in = refs[1 + 3 * n_pieces:1 + 3 * n_pieces + n_cast]
    out_ref = refs[1 + 3 * n_pieces + n_cast]
    cast_out = refs[2 + 3 * n_pieces + n_cast:]
    for src, dst in zip(cast_in, cast_out):
        dst[...] = src[...].astype(BF16)
    scale2 = (dh ** -0.5) * _LOG2_E
    for h in range(heads):
        hs = slice(h * dh, (h + 1) * dh)
        q = q_ref[:, hs]
        scores = [lax.dot_general(q, k_ref[:, hs], _NT, preferred_element_type=F32) * scale2 + tab_ref[h]
                  for k_ref, tab_ref in zip(k_refs, tab_refs)]
        m = jnp.max(functools.reduce(jnp.maximum, scores), axis=-1, keepdims=True)
        es = [jnp.exp2(s - m) for s in scores]
        denom = jnp.sum(functools.reduce(jnp.add, es), axis=-1, keepdims=True)
        acc = functools.reduce(jnp.add, [
            jnp.dot(e.astype(BF16), v_ref[:, hs], preferred_element_type=F32) for e, v_ref in zip(es, v_refs)])
        out_ref[:, hs] = (acc / denom).astype(out_ref.dtype)


def _attention(proj, tab, d_attn, batch, seq, q_rows, cast_weights):
    t = proj.shape[0]
    dh = d_attn // ATTN_HEADS
    n_q = seq // q_rows
    n_steps = batch * n_q
    pad_blocks = (LEFT_CHUNKS * CHUNK) // q_rows
    n_pieces = 1 + pad_blocks
    assert tab.shape == (n_pieces + 1, ATTN_HEADS, q_rows, q_rows)
    q_col = proj.shape[1] // d_attn - 3

    def window_spec(col_block, p):
        return pl.BlockSpec(
            (q_rows, d_attn),
            lambda b, qi: (b * n_q + jnp.maximum(qi + p - pad_blocks, 0), col_block))

    def tab_spec(p):
        return pl.BlockSpec(
            (None, ATTN_HEADS, q_rows, q_rows),
            lambda b, qi: (jnp.where(qi + p >= pad_blocks, p, n_pieces), 0, 0, 0))

    def cast_spec(w):
        assert w.shape[0] % (n_steps * BF16_SUBLANES) == 0
        return pl.BlockSpec((w.shape[0] // n_steps, w.shape[1]), lambda b, qi: (b * n_q + qi, 0))

    cast_specs = [cast_spec(w) for w in cast_weights]
    outs = pl.pallas_call(
        functools.partial(_attn_kernel, heads=ATTN_HEADS, dh=dh, n_pieces=n_pieces, n_cast=len(cast_weights)),
        out_shape=(jax.ShapeDtypeStruct((t, d_attn), BF16),
                   *[jax.ShapeDtypeStruct(w.shape, BF16) for w in cast_weights]),
        grid=(batch, n_q),
        in_specs=[pl.BlockSpec((q_rows, d_attn), lambda b, qi: (b * n_q + qi, q_col))]
        + [window_spec(q_col + 1, p) for p in range(n_pieces)]
        + [window_spec(q_col + 2, p) for p in range(n_pieces)]
        + [tab_spec(p) for p in range(n_pieces)]
        + cast_specs,
        out_specs=(pl.BlockSpec((q_rows, d_attn), lambda b, qi: (b * n_q + qi, 0)), *cast_specs),
        compiler_params=_params(("arbitrary", "arbitrary"), 48),
        name="chunk_attn",
    )(proj, *([proj] * (2 * n_pieces)), *([tab] * n_pieces), *cast_weights)
    return outs[0], outs[1:]


def _bias_table(rel, q_rows):
    heads = rel.shape[0]
    pad_rows = LEFT_CHUNKS * CHUNK
    width = q_rows + pad_rows
    d_lo, d_hi = pad_rows - width + 1, pad_rows + q_rows - 1
    mid = rel[:, max(d_lo, -MAX_REL) + MAX_REL:min(d_hi, MAX_REL) + MAX_REL + 1]
    left = jnp.repeat(rel[:, :1], max(0, -MAX_REL - d_lo), axis=1)
    right = jnp.repeat(rel[:, -1:], max(0, d_hi - MAX_REL), axis=1)
    by_dist = jnp.concatenate([left, mid, right], axis=1)[:, ::-1]
    length = by_dist.shape[1]
    ring = jnp.pad(by_dist, ((0, 0), (0, 1)))
    skew = jnp.tile(ring, (1, q_rows))[:, :q_rows * length].reshape(heads, q_rows, length)
    bias = skew[:, :, q_rows - 1:q_rows - 1 + width]
    q_chunk = jnp.arange(q_rows)[:, None] // CHUNK
    k_chunk = jnp.arange(width)[None, :] // CHUNK
    in_band = (k_chunk >= q_chunk) & (k_chunk <= q_chunk + LEFT_CHUNKS)
    tab = jnp.where(in_band[None], bias * _LOG2_E, NEG_INF)
    pieces = [tab[:, :, p * q_rows:(p + 1) * q_rows] for p in range(width // q_rows)]
    pieces.append(jnp.full((heads, q_rows, q_rows), NEG_INF, F32))
    return jnp.stack(pieces, axis=0)


def _out_proj_kernel(a_ref, b_ref, wa_ref, wb_ref, x_ref, g_ref, o_ref, u_ref):
    acc = jnp.dot(a_ref[...], wa_ref[...], preferred_element_type=F32)
    acc = acc + jnp.dot(b_ref[...], wb_ref[...], preferred_element_type=F32)
    h = x_ref[...] + acc
    o_ref[...] = h
    u_ref[...] = _rmsnorm_f32(h, g_ref[...]).astype(BF16)


def _out_proj(h_a, h_b, w, x2, g_next, tm):
    t, d = x2.shape
    ka, kb = h_a.shape[1], h_b.shape[1]
    assert ka == kb
    return pl.pallas_call(
        _out_proj_kernel,
        out_shape=(jax.ShapeDtypeStruct((t, d), F32), jax.ShapeDtypeStruct((t, d), BF16)),
        grid=(t // tm,),
        in_specs=[
            pl.BlockSpec((tm, ka), lambda i: (i, 0)),
            pl.BlockSpec((tm, kb), lambda i: (i, 0)),
            pl.BlockSpec((None, ka, d), lambda i: (0, 0, 0)),
            pl.BlockSpec((None, kb, d), lambda i: (1, 0, 0)),
            pl.BlockSpec((tm, d), lambda i: (i, 0)),
            pl.BlockSpec((1, d), lambda i: (0, 0)),
        ],
        out_specs=(pl.BlockSpec((tm, d), lambda i: (i, 0)), pl.BlockSpec((tm, d), lambda i: (i, 0))),
        compiler_params=_params(("arbitrary",), 56),
        name="out_proj",
    )(h_a, h_b, w, w, x2, g_next)


def _mlp_kernel(h_ref, u_ref, w1_ref, w2_ref, gp_ref, *out_refs, final_norm):
    o_ref = out_refs[0]
    j = pl.program_id(1)

    @pl.when(j == 0)
    def _():
        o_ref[...] = h_ref[...]

    a = jnp.dot(u_ref[...], w1_ref[...], preferred_element_type=F32)
    a = jnp.square(jnp.maximum(a, 0.0)).astype(BF16)
    o_ref[...] += jnp.dot(a, w2_ref[...], preferred_element_type=F32)

    @pl.when(j == pl.num_programs(1) - 1)
    def _():
        normed = _rmsnorm_f32(o_ref[...], gp_ref[...])
        if final_norm:
            o_ref[...] = normed
        else:
            out_refs[1][...] = normed.astype(BF16)


def _mlp(h2, u, w1, w2, layer, g_post, final_norm, tm, tf):
    t, d = h2.shape
    f = w1.shape[2]
    row_tile = pl.BlockSpec((tm, d), lambda i, j: (i, 0))
    out_shape = [jax.ShapeDtypeStruct((t, d), F32)]
    if not final_norm:
        out_shape.append(jax.ShapeDtypeStruct((t, d), BF16))
    return pl.pallas_call(
        functools.partial(_mlp_kernel, final_norm=final_norm),
        out_shape=tuple(out_shape),
        grid=(t // tm, f // tf),
        in_specs=[
            row_tile,
            row_tile,
            pl.BlockSpec((None, d, tf), lambda i, j: (layer, 0, j)),
            pl.BlockSpec((None, tf, d), lambda i, j: (layer, j, 0)),
            pl.BlockSpec((1, d), lambda i, j: (0, 0)),
        ],
        out_specs=tuple([row_tile] * len(out_shape)),
        compiler_params=_params(("arbitrary", "arbitrary"), 56),
        name="mlp_final" if final_norm else "mlp",
    )(h2, u, w1, w2, g_post)


def _glu_kernel(u_ref, wa_ref, wg_ref, ba_ref, bg_ref, o_ref):
    u = u_ref[...]
    a = jnp.dot(u, wa_ref[...], preferred_element_type=F32) + ba_ref[...]
    gate = jnp.dot(u, wg_ref[...], preferred_element_type=F32) + bg_ref[...]
    o_ref[...] = (a * _sigmoid(gate)).astype(o_ref.dtype)


def _glu(u, pw1, pw1_b, tm, tn):
    t, d = u.shape
    n = pw1.shape[2] // 2
    nb = n // tn
    return pl.pallas_call(
        _glu_kernel,
        out_shape=jax.ShapeDtypeStruct((t, n), BF16),
        grid=(t // tm, nb),
        in_specs=[
            pl.BlockSpec((tm, d), lambda i, j: (i, 0)),
            pl.BlockSpec((None, d, tn), lambda i, j: (0, 0, j)),
            pl.BlockSpec((None, d, tn), lambda i, j: (0, 0, j + nb)),
            pl.BlockSpec((1, tn), lambda i, j: (0, j)),
            pl.BlockSpec((1, tn), lambda i, j: (0, j + nb)),
        ],
        out_specs=pl.BlockSpec((tm, tn), lambda i, j: (i, j)),
        compiler_params=_params(("arbitrary", "arbitrary"), 56),
        name="conv_glu",
    )(u, pw1, pw1, pw1_b, pw1_b)


_DW_HALO = 2 * BF16_SUBLANES
_DW_ROWS = 64


def _conv_tail_kernel(z_ref, halo_ref, dw_ref, dwb_ref, lng_ref, lnb_ref, w2_ref, b2_ref, h_ref, g_ref,
                      o_ref, u_ref, zs_ref, y_ref, *, tm, tiles_per_seq):
    i = pl.program_id(0)
    n_slabs = z_ref.shape[1] // LANES
    seq_start = (i % tiles_per_seq) == 0
    for s in range(n_slabs):
        lanes = slice(s * LANES, (s + 1) * LANES)
        zs_ref[s, 0:_DW_HALO, :] = jnp.where(seq_start, 0.0, halo_ref[:, lanes].astype(F32))
        zs_ref[s, _DW_HALO:, :] = z_ref[:, lanes].astype(F32)
    first_tap = _DW_HALO - (CONV_WIDTH - 1)

    def slab_body(s, carry):
        lanes = pl.ds(pl.multiple_of(s * LANES, LANES), LANES)
        for rb in range(tm // _DW_ROWS):
            r0 = rb * _DW_ROWS
            acc = jnp.broadcast_to(dwb_ref[:, lanes], (_DW_ROWS, LANES))
            for k in range(CONV_WIDTH):
                acc = acc + dw_ref[k:k + 1, lanes] * zs_ref[s, r0 + first_tap + k:r0 + first_tap + k + _DW_ROWS, :]
            y_ref[r0:r0 + _DW_ROWS, lanes] = acc
        return carry

    lax.fori_loop(0, n_slabs, slab_body, 0)
    y = y_ref[...]
    mu = jnp.mean(y, axis=-1, keepdims=True)
    yc = y - mu
    var = jnp.mean(yc * yc, axis=-1, keepdims=True)
    yn = yc * lax.rsqrt(var + EPS) * lng_ref[...] + lnb_ref[...]
    a = (yn * _sigmoid(yn)).astype(BF16)
    h = h_ref[...] + jnp.dot(a, w2_ref[...], preferred_element_type=F32) + b2_ref[...]
    o_ref[...] = h
    u_ref[...] = _rmsnorm_f32(h, g_ref[...]).astype(BF16)


def _conv_tail(z, dw_w, dw_b, ln_g, ln_b, pw2, pw2_b, h2, g_next, seq, tm):
    t, d = h2.shape
    halo_blocks = tm // _DW_HALO
    row_tile = pl.BlockSpec((tm, d), lambda i: (i, 0))
    vec = pl.BlockSpec((1, d), lambda i: (0, 0))
    return pl.pallas_call(
        functools.partial(_conv_tail_kernel, tm=tm, tiles_per_seq=seq // tm),
        out_shape=(jax.ShapeDtypeStruct((t, d), F32), jax.ShapeDtypeStruct((t, d), BF16)),
        grid=(t // tm,),
        in_specs=[
            row_tile,
            pl.BlockSpec((_DW_HALO, d), lambda i: (jnp.maximum(i * halo_blocks - 1, 0), 0)),
            pl.BlockSpec(dw_w.shape, lambda i: (0, 0)),
            vec, vec, vec,
            pl.BlockSpec((None, d, d), lambda i: (0, 0, 0), pipeline_mode=pl.Buffered(1)),
            vec,
            row_tile,
            vec,
        ],
        out_specs=(row_tile, row_tile),
        scratch_shapes=[pltpu.VMEM((d // LANES, _DW_HALO + tm, LANES), F32), pltpu.VMEM((tm, d), F32)],
        compiler_params=_params(("arbitrary",), 56),
        name="conv_tail",
    )(z, z, dw_w, dw_b, ln_g, ln_b, pw2, pw2_b, h2, g_next)


def _tiles(seq):
    def rows(want):
        return min(want, seq)

    return dict(
        repack_rows=512,
        in_proj_tm=rows(512),
        mlstm_chunks_per_step=min(4, seq // CHUNK),
        attn_q_rows=rows(256),
        out_proj_tm=rows(512),
        mlp_tm=rows(512), mlp_tf=1024,
        glu_tm=rows(1024), glu_tn=1024,
        conv_tm=rows(512),
    )


def _row(v):
    return v.reshape(1, -1).astype(F32)


def kernel(x, mixer_norm_g, mix_w_in, qk_conv_w, qk_conv_b, igate_b, fgate_b, mlstm_norm_g, rel_bias,
           mix_w_out, conv_pw1_w, conv_pw1_b, conv_dw_w, conv_dw_b, conv_ln_g, conv_ln_b, conv_pw2_w,
           conv_pw2_b, mlp_norm_g, mlp_w1, mlp_w2, final_norm_g):
    batch, seq, d = x.shape
    tokens = batch * seq
    n_chunks = seq // CHUNK
    dm = d // 2
    da = d - dm
    heads = MLSTM_HEADS
    tl = _tiles(seq)
    x2 = x.reshape(tokens, d)

    w_in_t = jnp.swapaxes(mix_w_in, 1, 2)
    gate_lo = 4 * dm
    gate_hi = gate_lo + 2 * heads
    w_main_t = _drop_rows_bf16(w_in_t, gate_lo, gate_hi, tl["repack_rows"])
    w_gate_t = jnp.pad(w_in_t[0, gate_lo:gate_hi], ((0, LANES - 2 * heads), (0, 0))).astype(BF16)
    proj, gates = _in_proj(x2, _row(mixer_norm_g[0]), w_main_t, w_gate_t, tl["in_proj_tm"],
                           w_main_t.shape[0] // 2)

    n_bh = batch * heads
    g8 = gates[:, :2 * heads].reshape(batch, n_chunks, CHUNK, 2, heads)
    g8 = jnp.transpose(g8, (3, 1, 0, 4, 2)).reshape(2, n_chunks * n_bh, CHUNK)
    g8 = jnp.pad(g8, ((0, 0), (0, 0), (0, LANES - CHUNK)))
    bias_rows = lambda bvec: jnp.broadcast_to(
        jnp.tile(bvec.astype(F32), batch * n_chunks)[:, None], (n_chunks * n_bh, LANES))
    r, mt, wi, et, ws, dc = _gate_prep(g8[0], g8[1], bias_rows(igate_b[0]), bias_rows(fgate_b[0]),
                                       n_chunks, n_bh)
    per_frame = jnp.stack([mt, wi, et, ws], axis=0)[:, :, :CHUNK]
    per_frame = per_frame.reshape(4, n_chunks, batch, heads, CHUNK)
    cols = jnp.transpose(per_frame, (2, 1, 4, 0, 3)).reshape(batch, n_chunks, CHUNK, 4 * heads)
    cols = jnp.pad(cols, ((0, 0), (0, 0), (0, 0), (0, LANES - 4 * heads)))
    rows = jnp.concatenate([r.reshape(n_chunks, batch, heads, LANES),
                            dc.reshape(n_chunks, batch, heads, LANES)], axis=2)
    rows = jnp.transpose(rows, (1, 0, 2, 3))

    h_a = _mlstm(proj, qk_conv_w[0].astype(F32), _row(qk_conv_b[0]), cols, rows,
                 _row(mlstm_norm_g[0]), batch, n_chunks, tl["mlstm_chunks_per_step"])

    tab = _bias_table(rel_bias[0].astype(F32), tl["attn_q_rows"])
    h_b, (w1_flat, w2_flat) = _attention(
        proj, tab, da, batch, seq, tl["attn_q_rows"],
        [mlp_w1.reshape(-1, mlp_w1.shape[2]), mlp_w2.reshape(-1, mlp_w2.shape[2])])
    mlp_w1_bf = w1_flat.reshape(mlp_w1.shape)
    mlp_w2_bf = w2_flat.reshape(mlp_w2.shape)

    h, u = _out_proj(h_a, h_b, mix_w_out.astype(BF16).reshape(2, dm, d), x2, _row(mlp_norm_g[0]),
                     tl["out_proj_tm"])

    h, u = _mlp(h, u, mlp_w1_bf, mlp_w2_bf, 0, _row(mixer_norm_g[1]), False, tl["mlp_tm"], tl["mlp_tf"])

    z = _glu(u, conv_pw1_w.astype(BF16), _row(conv_pw1_b[0]), tl["glu_tm"], tl["glu_tn"])
    h, u = _conv_tail(z, conv_dw_w[0].astype(F32), _row(conv_dw_b[0]), _row(conv_ln_g[0]),
                      _row(conv_ln_b[0]), conv_pw2_w.astype(BF16), _row(conv_pw2_b[0]), h,
                      _row(mlp_norm_g[1]), seq, tl["conv_tm"])

    (out,) = _mlp(h, u, mlp_w1_bf, mlp_w2_bf, 1, _row(final_norm_g), True, tl["mlp_tm"], tl["mlp_tf"])
    return out.reshape(batch, seq, d)
```

```python
import functools
import math

import jax
import jax.numpy as jnp
from jax import lax
from jax.experimental import pallas as pl
from jax.experimental.pallas import tpu as pltpu

CHUNK = 64
MLSTM_HEADS = 4
MLSTM_CONV = 4
ATTN_HEADS = 8
LEFT_CHUNKS = 8
MAX_REL = 256
CONV_WIDTH = 31
EPS = 1e-6

V7X_VMEM_BYTES = 64 * 1024 * 1024
LANES = 128
BF16_SUBLANES = 16

F32 = jnp.float32
BF16 = jnp.bfloat16
NEG_INF = float("-inf")
_LOG2_E = math.log2(math.e)

_NT = (((1,), (1,)), ((), ()))
_TN = (((0,), (0,)), ((), ()))


def _params(semantics, vmem_mib):
    assert vmem_mib * 1024 * 1024 <= V7X_VMEM_BYTES
    return pltpu.CompilerParams(dimension_semantics=semantics,
                                vmem_limit_bytes=vmem_mib * 1024 * 1024)


def _rmsnorm_f32(x, g):
    return x * lax.rsqrt(jnp.mean(x * x, axis=-1, keepdims=True) + EPS) * g


def _sigmoid(x):
    return 0.5 * jnp.tanh(0.5 * x) + 0.5


def _drop_rows_kernel(w_ref, o_ref):
    o_ref[...] = w_ref[0].astype(BF16)


def _drop_rows_bf16(w_t, lo, hi, rows):
    n_in, d = w_t.shape[1:]
    n = n_in - (hi - lo)
    assert lo % rows == 0 and n % rows == 0 and (hi - lo) % 8 == 0
    return pl.pallas_call(
        _drop_rows_kernel,
        out_shape=jax.ShapeDtypeStruct((n, d), BF16),
        grid=(n // rows,),
        in_specs=[pl.BlockSpec((pl.Element(1), pl.Element(rows), pl.Element(d)),
                               lambda i: (0, pl.multiple_of(jnp.where(i * rows < lo, i * rows, i * rows + (hi - lo)), 8), 0))],
        out_specs=pl.BlockSpec((rows, d), lambda i: (i, 0)),
        compiler_params=_params(("arbitrary",), 32),
        name="w_in_repack",
    )(w_t)


def _in_proj_kernel(x_ref, g_ref, w_ref, wg_ref, o_ref, og_ref, u_ref):
    @pl.when(pl.program_id(1) == 0)
    def _():
        u = _rmsnorm_f32(x_ref[...], g_ref[...]).astype(BF16)
        u_ref[...] = u
        og_ref[...] = lax.dot_general(u, wg_ref[...], _NT, preferred_element_type=F32)

    o_ref[...] = lax.dot_general(u_ref[...], w_ref[...], _NT, preferred_element_type=F32).astype(o_ref.dtype)


def _in_proj(x2, g, w_main_t, w_gate_t, tm, tn):
    t, d = x2.shape
    n = w_main_t.shape[0]
    return pl.pallas_call(
        _in_proj_kernel,
        out_shape=(jax.ShapeDtypeStruct((t, n), BF16), jax.ShapeDtypeStruct((t, LANES), F32)),
        grid=(t // tm, n // tn),
        in_specs=[
            pl.BlockSpec((tm, d), lambda i, j: (i, 0)),
            pl.BlockSpec((1, d), lambda i, j: (0, 0)),
            pl.BlockSpec((tn, d), lambda i, j: (j, 0)),
            pl.BlockSpec((LANES, d), lambda i, j: (0, 0)),
        ],
        out_specs=(pl.BlockSpec((tm, tn), lambda i, j: (i, j)),
                   pl.BlockSpec((tm, LANES), lambda i, j: (i, 0))),
        scratch_shapes=[pltpu.VMEM((tm, d), BF16)],
        compiler_params=_params(("arbitrary", "arbitrary"), 56),
        name="in_proj",
    )(x2, g, w_main_t, w_gate_t)


def _lane_prefix(x, op, ident, lane):
    shift = 1
    while shift < CHUNK:
        y = pltpu.roll(x, shift, axis=1)
        x = op(x, jnp.where(lane >= shift, y, ident))
        shift *= 2
    return x


def _gate_prep_kernel(gi_ref, gf_ref, ib_ref, fb_ref,
                      r_ref, mt_ref, wi_ref, et_ref, ws_ref, dc_ref,
                      g_sc, c_sc, m_sc, *, n_chunks, n_bh):
    shape = gi_ref.shape
    lane = lax.broadcasted_iota(jnp.int32, shape, 1)
    valid = lane < CHUNK
    i_pre = gi_ref[...] + ib_ref[...]
    f_pre = gf_ref[...] + fb_ref[...]
    lf = jnp.minimum(f_pre, 0.0) - jnp.log1p(jnp.exp(-jnp.abs(f_pre)))
    lf = jnp.where(valid, lf, 0.0)
    b = _lane_prefix(lf, jnp.add, 0.0, lane)
    r = jnp.where(valid, i_pre - b, NEG_INF)
    cm = _lane_prefix(r, jnp.maximum, NEG_INF, lane)
    g_tot = jnp.sum(jnp.where(lane == CHUNK - 1, b, 0.0), axis=1, keepdims=True)
    c_last = jnp.max(r, axis=1, keepdims=True)
    g_sc[...] = jnp.broadcast_to(g_tot, shape)
    c_sc[...] = jnp.broadcast_to(c_last, shape)

    def body(c, m):
        rows = pl.ds(pl.multiple_of(c * n_bh, n_bh), n_bh)
        m_sc[rows, :] = m
        return g_sc[rows, :] + jnp.maximum(m, c_sc[rows, :])

    lax.fori_loop(0, n_chunks, body, jnp.zeros((n_bh, shape[1]), F32))
    m = m_sc[...]
    big_m = jnp.maximum(m, cm)
    m_last = jnp.maximum(m, c_sc[...])
    r_ref[...] = jnp.where(valid, r, 0.0)
    mt_ref[...] = big_m
    wi_ref[...] = jnp.exp(m - big_m)
    et_ref[...] = jnp.exp(-b - big_m)
    ws_ref[...] = jnp.exp(r - m_last)
    dc_ref[...] = jnp.exp(m - m_last)


def _gate_prep(gi, gf, ib, fb, n_chunks, n_bh):
    shape = jax.ShapeDtypeStruct(gi.shape, F32)
    return pl.pallas_call(
        functools.partial(_gate_prep_kernel, n_chunks=n_chunks, n_bh=n_bh),
        out_shape=(shape,) * 6,
        scratch_shapes=[pltpu.VMEM(gi.shape, F32)] * 3,
        compiler_params=_params(None, 32),
        name="gate_prep",
    )(gi, gf, ib, fb)


_QK_HALO = BF16_SUBLANES


def _mlstm_kernel(qk_ref, halo_ref, v_ref, o_ref, cw_ref, cb_ref, cols_ref, rows_ref, ng_ref,
                  out_ref, xs_ref, c_ref, n_ref, *, heads, dh, chunks_per_step):
    group_id = pl.program_id(1)
    dm = heads * dh
    n_slabs = 2 * dm // LANES

    @pl.when(group_id == 0)
    def _():
        c_ref[...] = jnp.zeros(c_ref.shape, F32)
        n_ref[...] = jnp.zeros(n_ref.shape, F32)

    for s in range(n_slabs):
        lanes = slice(s * LANES, (s + 1) * LANES)
        xs_ref[s, 0:_QK_HALO, :] = jnp.where(group_id == 0, 0.0, halo_ref[:, lanes].astype(F32))
        xs_ref[s, _QK_HALO:, :] = qk_ref[:, lanes].astype(F32)

    def conv_silu(col0, row0):
        parts = []
        for s in range(col0 // LANES, (col0 + dh) // LANES):
            lanes = slice(s * LANES, (s + 1) * LANES)
            acc = cb_ref[:, lanes]
            for j in range(MLSTM_CONV):
                off = row0 + _QK_HALO - (MLSTM_CONV - 1) + j
                acc = acc + cw_ref[j:j + 1, lanes] * xs_ref[s, off:off + CHUNK, :]
            parts.append(acc * _sigmoid(acc))
        return jnp.concatenate(parts, axis=1)

    tri = (lax.broadcasted_iota(jnp.int32, (CHUNK, CHUNK), 0)
           >= lax.broadcasted_iota(jnp.int32, (CHUNK, CHUNK), 1))
    eye = (lax.broadcasted_iota(jnp.int32, (dh, dh), 0)
           == lax.broadcasted_iota(jnp.int32, (dh, dh), 1)).astype(BF16)
    for cc in range(chunks_per_step):
        row0 = cc * CHUNK
        rs = slice(row0, row0 + CHUNK)
        for h in range(heads):
            hs = slice(h * dh, (h + 1) * dh)
            q = conv_silu(h * dh, row0)
            k = conv_silu(dm + h * dh, row0) * (dh ** -0.5)
            qb = q.astype(BF16)
            kb = k.astype(BF16)
            v = v_ref[rs, hs]
            mt = cols_ref[cc, :, h:h + 1]
            wi = cols_ref[cc, :, heads + h:heads + h + 1]
            et = cols_ref[cc, :, 2 * heads + h:2 * heads + h + 1]
            ws = cols_ref[cc, :, 3 * heads + h:3 * heads + h + 1]
            r = rows_ref[cc, h:h + 1, 0:CHUNK]
            dc = rows_ref[cc, heads + h:heads + h + 1, 0:1]

            s = lax.dot_general(qb, kb, _NT, preferred_element_type=F32)
            sw = s * jnp.where(tri, jnp.exp(r - mt), 0.0)
            c_old = c_ref[h]
            inter = lax.dot_general(qb, c_old.astype(BF16), _NT, preferred_element_type=F32)
            intra = jnp.dot(sw.astype(BF16), v, preferred_element_type=F32)
            n_old = n_ref[h:h + 1, :]
            num = intra + wi * inter
            den = (jnp.sum(sw, axis=-1, keepdims=True)
                   + wi * jnp.sum(q * n_old, axis=-1, keepdims=True))
            hh = num / jnp.maximum(jnp.abs(den), et)
            hn = hh * lax.rsqrt(jnp.mean(hh * hh, axis=-1, keepdims=True) + EPS) * ng_ref[:, hs]
            out_ref[rs, hs] = (_sigmoid(o_ref[rs, hs].astype(F32)) * hn).astype(out_ref.dtype)

            vw = (v.astype(F32) * ws).astype(BF16)
            vw_t = lax.dot_general(eye, vw, _NT, preferred_element_type=F32).astype(BF16)
            upd = jnp.dot(vw_t, kb, preferred_element_type=F32)
            c_ref[h] = dc * c_old + upd
            n_ref[h:h + 1, :] = dc * n_old + jnp.sum(k * ws, axis=0, keepdims=True)


def _mlstm(proj, conv_w, conv_b, cols, rows, norm_g, batch, n_chunks, chunks_per_step):
    t = proj.shape[0]
    dm = norm_g.shape[1]
    dh = dm // MLSTM_HEADS
    n_groups = n_chunks // chunks_per_step
    rows_per_step = chunks_per_step * CHUNK
    halo_blocks = rows_per_step // _QK_HALO
    return pl.pallas_call(
        functools.partial(_mlstm_kernel, heads=MLSTM_HEADS, dh=dh, chunks_per_step=chunks_per_step),
        out_shape=jax.ShapeDtypeStruct((t, dm), BF16),
        grid=(batch, n_groups),
        in_specs=[
            pl.BlockSpec((rows_per_step, 2 * dm), lambda b, c: (b * n_groups + c, 0)),
            pl.BlockSpec((_QK_HALO, 2 * dm),
                         lambda b, c: (jnp.maximum((b * n_groups + c) * halo_blocks - 1, 0), 0)),
            pl.BlockSpec((rows_per_step, dm), lambda b, c: (b * n_groups + c, 2)),
            pl.BlockSpec((rows_per_step, dm), lambda b, c: (b * n_groups + c, 3)),
            pl.BlockSpec((MLSTM_CONV, 2 * dm), lambda b, c: (0, 0)),
            pl.BlockSpec((1, 2 * dm), lambda b, c: (0, 0)),
            pl.BlockSpec((None, chunks_per_step, CHUNK, LANES), lambda b, c: (b, c, 0, 0)),
            pl.BlockSpec((None, chunks_per_step, 2 * MLSTM_HEADS, LANES), lambda b, c: (b, c, 0, 0)),
            pl.BlockSpec((1, dm), lambda b, c: (0, 0)),
        ],
        out_specs=pl.BlockSpec((rows_per_step, dm), lambda b, c: (b * n_groups + c, 0)),
        scratch_shapes=[
            pltpu.VMEM((2 * dm // LANES, _QK_HALO + rows_per_step, LANES), F32),
            pltpu.VMEM((MLSTM_HEADS, dh, dh), F32),
            pltpu.VMEM((2 * MLSTM_HEADS, dh), F32),
        ],
        compiler_params=_params(("arbitrary", "arbitrary"), 32),
        name="mlstm",
    )(proj, proj, proj, proj, conv_w, conv_b, cols, rows, norm_g)


def _attn_kernel(*refs, heads, dh, n_pieces, n_cast):
    q_ref = refs[0]
    k_refs = refs[1:1 + n_pieces]
    v_refs = refs[1 + n_pieces:1 + 2 * n_pieces]
    tab_refs = refs[1 + 2 * n_pieces:1 + 3 * n_pieces]
    cast_in = refs[1 + 3 * n_pieces:1 + 3 * n_pieces + n_cast]
    out_ref = refs[1 + 3 * n_pieces + n_cast]
    cast_out = refs[2 + 3 * n_pieces + n_cast:]
    for src, dst in zip(cast_in, cast_out):
        dst[...] = src[...].astype(BF16)
    scale2 = (dh ** -0.5) * _LOG2_E
    for h in range(heads):
        hs = slice(h * dh, (h + 1) * dh)
        q = q_ref[:, hs]
        scores = [lax.dot_general(q, k_ref[:, hs], _NT, preferred_element_type=F32) * scale2 + tab_ref[h]
                  for k_ref, tab_ref in zip(k_refs, tab_refs)]
        m = jnp.max(functools.reduce(jnp.maximum, scores), axis=-1, keepdims=True)
        es = [jnp.exp2(s - m) for s in scores]
        denom = jnp.sum(functools.reduce(jnp.add, es), axis=-1, keepdims=True)
        acc = functools.reduce(jnp.add, [
            jnp.dot(e.astype(BF16), v_ref[:, hs], preferred_element_type=F32) for e, v_ref in zip(es, v_refs)])
        out_ref[:, hs] = (acc / denom).astype(out_ref.dtype)


def _attention(proj, tab, d_attn, batch, seq, q_rows, cast_weights):
    t = proj.shape[0]
    dh = d_attn // ATTN_HEADS
    n_q = seq // q_rows
    n_steps = batch * n_q
    pad_blocks = (LEFT_CHUNKS * CHUNK) // q_rows
    n_pieces = 1 + pad_blocks
    assert tab.shape == (n_pieces + 1, ATTN_HEADS, q_rows, q_rows)
    q_col = proj.shape[1] // d_attn - 3

    def window_spec(col_block, p):
        return pl.BlockSpec(
            (q_rows, d_attn),
            lambda b, qi: (b * n_q + jnp.maximum(qi + p - pad_blocks, 0), col_block))

    def tab_spec(p):
        return pl.BlockSpec(
            (None, ATTN_HEADS, q_rows, q_rows),
            lambda b, qi: (jnp.where(qi + p >= pad_blocks, p, n_pieces), 0, 0, 0))

    def cast_spec(w):
        assert w.shape[0] % (n_steps * BF16_SUBLANES) == 0
        return pl.BlockSpec((w.shape[0] // n_steps, w.shape[1]), lambda b, qi: (b * n_q + qi, 0))

    cast_specs = [cast_spec(w) for w in cast_weights]
    outs = pl.pallas_call(
        functools.partial(_attn_kernel, heads=ATTN_HEADS, dh=dh, n_pieces=n_pieces, n_cast=len(cast_weights)),
        out_shape=(jax.ShapeDtypeStruct((t, d_attn), BF16),
                   *[jax.ShapeDtypeStruct(w.shape, BF16) for w in cast_weights]),
        grid=(batch, n_q),
        in_specs=[pl.BlockSpec((q_rows, d_attn), lambda b, qi: (b * n_q + qi, q_col))]
        + [window_spec(q_col + 1, p) for p in range(n_pieces)]
        + [window_spec(q_col + 2, p) for p in range(n_pieces)]
        + [tab_spec(p) for p in range(n_pieces)]
        + cast_specs,
        out_specs=(pl.BlockSpec((q_rows, d_attn), lambda b, qi: (b * n_q + qi, 0)), *cast_specs),
        compiler_params=_params(("arbitrary", "arbitrary"), 48),
        name="chunk_attn",
    )(proj, *([proj] * (2 * n_pieces)), *([tab] * n_pieces), *cast_weights)
    return outs[0], outs[1:]


def _bias_table(rel, q_rows):
    heads = rel.shape[0]
    pad_rows = LEFT_CHUNKS * CHUNK
    width = q_rows + pad_rows
    d_lo, d_hi = pad_rows - width + 1, pad_rows + q_rows - 1
    mid = rel[:, max(d_lo, -MAX_REL) + MAX_REL:min(d_hi, MAX_REL) + MAX_REL + 1]
    left = jnp.repeat(rel[:, :1], max(0, -MAX_REL - d_lo), axis=1)
    right = jnp.repeat(rel[:, -1:], max(0, d_hi - MAX_REL), axis=1)
    by_dist = jnp.concatenate([left, mid, right], axis=1)[:, ::-1]
    length = by_dist.shape[1]
    ring = jnp.pad(by_dist, ((0, 0), (0, 1)))
    skew = jnp.tile(ring, (1, q_rows))[:, :q_rows * length].reshape(heads, q_rows, length)
    bias = skew[:, :, q_rows - 1:q_rows - 1 + width]
    q_chunk = jnp.arange(q_rows)[:, None] // CHUNK
    k_chunk = jnp.arange(width)[None, :] // CHUNK
    in_band = (k_chunk >= q_chunk) & (k_chunk <= q_chunk + LEFT_CHUNKS)
    tab = jnp.where(in_band[None], bias * _LOG2_E, NEG_INF)
    pieces = [tab[:, :, p * q_rows:(p + 1) * q_rows] for p in range(width // q_rows)]
    pieces.append(jnp.full((heads, q_rows, q_rows), NEG_INF, F32))
    return jnp.stack(pieces, axis=0)


def _out_proj_kernel(a_ref, b_ref, wa_ref, wb_ref, x_ref, g_ref, o_ref, u_ref):
    acc = jnp.dot(a_ref[...], wa_ref[...], preferred_element_type=F32)
    acc = acc + jnp.dot(b_ref[...], wb_ref[...], preferred_element_type=F32)
    h = x_ref[...] + acc
    o_ref[...] = h
    u_ref[...] = _rmsnorm_f32(h, g_ref[...]).astype(BF16)


def _out_proj(h_a, h_b, w, x2, g_next, tm):
    t, d = x2.shape
    ka, kb = h_a.shape[1], h_b.shape[1]
    assert ka == kb
    return pl.pallas_call(
        _out_proj_kernel,
        out_shape=(jax.ShapeDtypeStruct((t, d), F32), jax.ShapeDtypeStruct((t, d), BF16)),
        grid=(t // tm,),
        in_specs=[
            pl.BlockSpec((tm, ka), lambda i: (i, 0)),
            pl.BlockSpec((tm, kb), lambda i: (i, 0)),
            pl.BlockSpec((None, ka, d), lambda i: (0, 0, 0)),
            pl.BlockSpec((None, kb, d), lambda i: (1, 0, 0)),
            pl.BlockSpec((tm, d), lambda i: (i, 0)),
            pl.BlockSpec((1, d), lambda i: (0, 0)),
        ],
        out_specs=(pl.BlockSpec((tm, d), lambda i: (i, 0)), pl.BlockSpec((tm, d), lambda i: (i, 0))),
        compiler_params=_params(("arbitrary",), 56),
        name="out_proj",
    )(h_a, h_b, w, w, x2, g_next)


def _mlp_kernel(h_ref, u_ref, w1_ref, w2_ref, gp_ref, *out_refs, final_norm):
    o_ref = out_refs[0]
    j = pl.program_id(1)

    @pl.when(j == 0)
    def _():
        o_ref[...] = h_ref[...]

    a = jnp.dot(u_ref[...], w1_ref[...], preferred_element_type=F32)
    a = jnp.square(jnp.maximum(a, 0.0)).astype(BF16)
    o_ref[...] += jnp.dot(a, w2_ref[...], preferred_element_type=F32)

    @pl.when(j == pl.num_programs(1) - 1)
    def _():
        normed = _rmsnorm_f32(o_ref[...], gp_ref[...])
        if final_norm:
            o_ref[...] = normed
        else:
            out_refs[1][...] = normed.astype(BF16)


def _mlp(h2, u, w1, w2, layer, g_post, final_norm, tm, tf):
    t, d = h2.shape
    f = w1.shape[2]
    row_tile = pl.BlockSpec((tm, d), lambda i, j: (i, 0))
    out_shape = [jax.ShapeDtypeStruct((t, d), F32)]
    if not final_norm:
        out_shape.append(jax.ShapeDtypeStruct((t, d), BF16))
    return pl.pallas_call(
        functools.partial(_mlp_kernel, final_norm=final_norm),
        out_shape=tuple(out_shape),
        grid=(t // tm, f // tf),
        in_specs=[
            row_tile,
            row_tile,
            pl.BlockSpec((None, d, tf), lambda i, j: (layer, 0, j)),
            pl.BlockSpec((None, tf, d), lambda i, j: (layer, j, 0)),
            pl.BlockSpec((1, d), lambda i, j: (0, 0)),
        ],
        out_specs=tuple([row_tile] * len(out_shape)),
        compiler_params=_params(("arbitrary", "arbitrary"), 56),
        name="mlp_final" if final_norm else "mlp",
    )(h2, u, w1, w2, g_post)


def _glu_kernel(u_ref, wa_ref, wg_ref, ba_ref, bg_ref, o_ref):
    u = u_ref[...]
    a = jnp.dot(u, wa_ref[...], preferred_element_type=F32) + ba_ref[...]
    gate = jnp.dot(u, wg_ref[...], preferred_element_type=F32) + bg_ref[...]
    o_ref[...] = (a * _sigmoid(gate)).astype(o_ref.dtype)


def _glu(u, pw1, pw1_b, tm, tn):
    t, d = u.shape
    n = pw1.shape[2] // 2
    nb = n // tn
    return pl.pallas_call(
        _glu_kernel,
        out_shape=jax.ShapeDtypeStruct((t, n), BF16),
        grid=(t // tm, nb),
        in_specs=[
            pl.BlockSpec((tm, d), lambda i, j: (i, 0)),
            pl.BlockSpec((None, d, tn), lambda i, j: (0, 0, j)),
            pl.BlockSpec((None, d, tn), lambda i, j: (0, 0, j + nb)),
            pl.BlockSpec((1, tn), lambda i, j: (0, j)),
            pl.BlockSpec((1, tn), lambda i, j: (0, j + nb)),
        ],
        out_specs=pl.BlockSpec((tm, tn), lambda i, j: (i, j)),
        compiler_params=_params(("arbitrary", "arbitrary"), 56),
        name="conv_glu",
    )(u, pw1, pw1, pw1_b, pw1_b)


_DW_HALO = 2 * BF16_SUBLANES
_DW_ROWS = 64


def _conv_tail_kernel(z_ref, halo_ref, dw_ref, dwb_ref, lng_ref, lnb_ref, w2_ref, b2_ref, h_ref, g_ref,
                      o_ref, u_ref, zs_ref, y_ref, *, tm, tiles_per_seq):
    i = pl.program_id(0)
    n_slabs = z_ref.shape[1] // LANES
    seq_start = (i % tiles_per_seq) == 0
    for s in range(n_slabs):
        lanes = slice(s * LANES, (s + 1) * LANES)
        zs_ref[s, 0:_DW_HALO, :] = jnp.where(seq_start, 0.0, halo_ref[:, lanes].astype(F32))
        zs_ref[s, _DW_HALO:, :] = z_ref[:, lanes].astype(F32)
    first_tap = _DW_HALO - (CONV_WIDTH - 1)

    def slab_body(s, carry):
        lanes = pl.ds(pl.multiple_of(s * LANES, LANES), LANES)
        for rb in range(tm // _DW_ROWS):
            r0 = rb * _DW_ROWS
            acc = jnp.broadcast_to(dwb_ref[:, lanes], (_DW_ROWS, LANES))
            for k in range(CONV_WIDTH):
                acc = acc + dw_ref[k:k + 1, lanes] * zs_ref[s, r0 + first_tap + k:r0 + first_tap + k + _DW_ROWS, :]
            y_ref[r0:r0 + _DW_ROWS, lanes] = acc
        return carry

    lax.fori_loop(0, n_slabs, slab_body, 0)
    y = y_ref[...]
    mu = jnp.mean(y, axis=-1, keepdims=True)
    yc = y - mu
    var = jnp.mean(yc * yc, axis=-1, keepdims=True)
    yn = yc * lax.rsqrt(var + EPS) * lng_ref[...] + lnb_ref[...]
    a = (yn * _sigmoid(yn)).astype(BF16)
    h = h_ref[...] + jnp.dot(a, w2_ref[...], preferred_element_type=F32) + b2_ref[...]
    o_ref[...] = h
    u_ref[...] = _rmsnorm_f32(h, g_ref[...]).astype(BF16)


def _conv_tail(z, dw_w, dw_b, ln_g, ln_b, pw2, pw2_b, h2, g_next, seq, tm):
    t, d = h2.shape
    halo_blocks = tm // _DW_HALO
    row_tile = pl.BlockSpec((tm, d), lambda i: (i, 0))
    vec = pl.BlockSpec((1, d), lambda i: (0, 0))
    return pl.pallas_call(
        functools.partial(_conv_tail_kernel, tm=tm, tiles_per_seq=seq // tm),
        out_shape=(jax.ShapeDtypeStruct((t, d), F32), jax.ShapeDtypeStruct((t, d), BF16)),
        grid=(t // tm,),
        in_specs=[
            row_tile,
            pl.BlockSpec((_DW_HALO, d), lambda i: (jnp.maximum(i * halo_blocks - 1, 0), 0)),
            pl.BlockSpec(dw_w.shape, lambda i: (0, 0)),
            vec, vec, vec,
            pl.BlockSpec((None, d, d), lambda i: (0, 0, 0), pipeline_mode=pl.Buffered(1)),
            vec,
            row_tile,
            vec,
        ],
        out_specs=(row_tile, row_tile),
        scratch_shapes=[pltpu.VMEM((d // LANES, _DW_HALO + tm, LANES), F32), pltpu.VMEM((tm, d), F32)],
        compiler_params=_params(("arbitrary",), 56),
        name="conv_tail",
    )(z, z, dw_w, dw_b, ln_g, ln_b, pw2, pw2_b, h2, g_next)


def _tiles(seq):
    def rows(want):
        return min(want, seq)

    return dict(
        repack_rows=512,
        in_proj_tm=rows(512),
        mlstm_chunks_per_step=min(4, seq // CHUNK),
        attn_q_rows=rows(256),
        out_proj_tm=rows(512),
        mlp_tm=rows(512), mlp_tf=1024,
        glu_tm=rows(1024), glu_tn=1024,
        conv_tm=rows(512),
    )


def _row(v):
    return v.reshape(1, -1).astype(F32)


def kernel(x, mixer_norm_g, mix_w_in, qk_conv_w, qk_conv_b, igate_b, fgate_b, mlstm_norm_g, rel_bias,
           mix_w_out, conv_pw1_w, conv_pw1_b, conv_dw_w, conv_dw_b, conv_ln_g, conv_ln_b, conv_pw2_w,
           conv_pw2_b, mlp_norm_g, mlp_w1, mlp_w2, final_norm_g):
    batch, seq, d = x.shape
    tokens = batch * seq
    n_chunks = seq // CHUNK
    dm = d // 2
    da = d - dm
    heads = MLSTM_HEADS
    tl = _tiles(seq)
    x2 = x.reshape(tokens, d)

    w_in_t = jnp.swapaxes(mix_w_in, 1, 2)
    gate_lo = 4 * dm
    gate_hi = gate_lo + 2 * heads
    w_main_t = _drop_rows_bf16(w_in_t, gate_lo, gate_hi, tl["repack_rows"])
    w_gate_t = jnp.pad(w_in_t[0, gate_lo:gate_hi], ((0, LANES - 2 * heads), (0, 0))).astype(BF16)
    proj, gates = _in_proj(x2, _row(mixer_norm_g[0]), w_main_t, w_gate_t, tl["in_proj_tm"],
                           w_main_t.shape[0] // 2)

    n_bh = batch * heads
    g8 = gates[:, :2 * heads].reshape(batch, n_chunks, CHUNK, 2, heads)
    g8 = jnp.transpose(g8, (3, 1, 0, 4, 2)).reshape(2, n_chunks * n_bh, CHUNK)
    g8 = jnp.pad(g8, ((0, 0), (0, 0), (0, LANES - CHUNK)))
    bias_rows = lambda bvec: jnp.broadcast_to(
        jnp.tile(bvec.astype(F32), batch * n_chunks)[:, None], (n_chunks * n_bh, LANES))
    r, mt, wi, et, ws, dc = _gate_prep(g8[0], g8[1], bias_rows(igate_b[0]), bias_rows(fgate_b[0]),
                                       n_chunks, n_bh)
    per_frame = jnp.stack([mt, wi, et, ws], axis=0)[:, :, :CHUNK]
    per_frame = per_frame.reshape(4, n_chunks, batch, heads, CHUNK)
    cols = jnp.transpose(per_frame, (2, 1, 4, 0, 3)).reshape(batch, n_chunks, CHUNK, 4 * heads)
    cols = jnp.pad(cols, ((0, 0), (0, 0), (0, 0), (0, LANES - 4 * heads)))
    rows = jnp.concatenate([r.reshape(n_chunks, batch, heads, LANES),
                            dc.reshape(n_chunks, batch, heads, LANES)], axis=2)
    rows = jnp.transpose(rows, (1, 0, 2, 3))

    h_a = _mlstm(proj, qk_conv_w[0].astype(F32), _row(qk_conv_b[0]), cols, rows,
                 _row(mlstm_norm_g[0]), batch, n_chunks, tl["mlstm_chunks_per_step"])

    tab = _bias_table(rel_bias[0].astype(F32), tl["attn_q_rows"])
    later_weights = [mlp_w1, mlp_w2, mix_w_out, conv_pw1_w, conv_pw2_w]
    h_b, cast = _attention(proj, tab, da, batch, seq, tl["attn_q_rows"],
                           [w.reshape(-1, w.shape[2]) for w in later_weights])
    mlp_w1_bf, mlp_w2_bf, w_out_bf, pw1_bf, pw2_bf = [c.reshape(w.shape) for c, w in zip(cast, later_weights)]

    h, u = _out_proj(h_a, h_b, w_out_bf.reshape(2, dm, d), x2, _row(mlp_norm_g[0]), tl["out_proj_tm"])

    h, u = _mlp(h, u, mlp_w1_bf, mlp_w2_bf, 0, _row(mixer_norm_g[1]), False, tl["mlp_tm"], tl["mlp_tf"])

    z = _glu(u, pw1_bf, _row(conv_pw1_b[0]), tl["glu_tm"], tl["glu_tn"])
    h, u = _conv_tail(z, conv_dw_w[0].astype(F32), _row(conv_dw_b[0]), _row(conv_ln_g[0]),
                      _row(conv_ln_b[0]), pw2_bf, _row(conv_pw2_b[0]), h,
                      _row(mlp_norm_g[1]), seq, tl["conv_tm"])

    (out,) = _mlp(h, u, mlp_w1_bf, mlp_w2_bf, 1, _row(final_norm_g), True, tl["mlp_tm"], tl["mlp_tf"])
    return out.reshape(batch, seq, d)
```

```python
import functools
import math

import jax
import jax.numpy as jnp
from jax import lax
from jax.experimental import pallas as pl
from jax.experimental.pallas import tpu as pltpu

CHUNK = 64
MLSTM_HEADS = 4
MLSTM_CONV = 4
ATTN_HEADS = 8
LEFT_CHUNKS = 8
MAX_REL = 256
CONV_WIDTH = 31
EPS = 1e-6

V7X_VMEM_BYTES = 64 * 1024 * 1024
LANES = 128
BF16_SUBLANES = 16

F32 = jnp.float32
BF16 = jnp.bfloat16
NEG_INF = float("-inf")
_LOG2_E = math.log2(math.e)

_NT = (((1,), (1,)), ((), ()))
_TN = (((0,), (0,)), ((), ()))


def _params(semantics, vmem_mib):
    assert vmem_mib * 1024 * 1024 <= V7X_VMEM_BYTES
    return pltpu.CompilerParams(dimension_semantics=semantics,
                                vmem_limit_bytes=vmem_mib * 1024 * 1024)


def _rmsnorm_f32(x, g):
    return x * lax.rsqrt(jnp.mean(x * x, axis=-1, keepdims=True) + EPS) * g


def _sigmoid(x):
    return 0.5 * jnp.tanh(0.5 * x) + 0.5


def _drop_rows_kernel(w_ref, o_ref):
    o_ref[...] = w_ref[0].astype(BF16)


def _drop_rows_bf16(w_t, lo, hi, rows):
    n_in, d = w_t.shape[1:]
    n = n_in - (hi - lo)
    assert lo % rows == 0 and n % rows == 0 and (hi - lo) % 8 == 0
    return pl.pallas_call(
        _drop_rows_kernel,
        out_shape=jax.ShapeDtypeStruct((n, d), BF16),
        grid=(n // rows,),
        in_specs=[pl.BlockSpec((pl.Element(1), pl.Element(rows), pl.Element(d)),
                               lambda i: (0, pl.multiple_of(jnp.where(i * rows < lo, i * rows, i * rows + (hi - lo)), 8), 0))],
        out_specs=pl.BlockSpec((rows, d), lambda i: (i, 0)),
        compiler_params=_params(("arbitrary",), 32),
        name="w_in_repack",
    )(w_t)


def _in_proj_kernel(x_ref, g_ref, w_ref, wg_ref, o_ref, og_ref, u_ref):
    @pl.when(pl.program_id(1) == 0)
    def _():
        u = _rmsnorm_f32(x_ref[...], g_ref[...]).astype(BF16)
        u_ref[...] = u
        og_ref[...] = lax.dot_general(u, wg_ref[...], _NT, preferred_element_type=F32)

    o_ref[...] = lax.dot_general(u_ref[...], w_ref[...], _NT, preferred_element_type=F32).astype(o_ref.dtype)


def _in_proj(x2, g, w_main_t, w_gate_t, tm, tn):
    t, d = x2.shape
    n = w_main_t.shape[0]
    return pl.pallas_call(
        _in_proj_kernel,
        out_shape=(jax.ShapeDtypeStruct((t, n), BF16), jax.ShapeDtypeStruct((t, LANES), F32)),
        grid=(t // tm, n // tn),
        in_specs=[
            pl.BlockSpec((tm, d), lambda i, j: (i, 0)),
            pl.BlockSpec((1, d), lambda i, j: (0, 0)),
            pl.BlockSpec((tn, d), lambda i, j: (j, 0)),
            pl.BlockSpec((LANES, d), lambda i, j: (0, 0)),
        ],
        out_specs=(pl.BlockSpec((tm, tn), lambda i, j: (i, j)),
                   pl.BlockSpec((tm, LANES), lambda i, j: (i, 0))),
        scratch_shapes=[pltpu.VMEM((tm, d), BF16)],
        compiler_params=_params(("arbitrary", "arbitrary"), 56),
        name="in_proj",
    )(x2, g, w_main_t, w_gate_t)


def _lane_prefix(x, op, ident, lane):
    shift = 1
    while shift < CHUNK:
        y = pltpu.roll(x, shift, axis=1)
        x = op(x, jnp.where(lane >= shift, y, ident))
        shift *= 2
    return x


def _gate_prep_kernel(gi_ref, gf_ref, ib_ref, fb_ref,
                      r_ref, mt_ref, wi_ref, et_ref, ws_ref, dc_ref,
                      g_sc, c_sc, m_sc, *, n_chunks, n_bh):
    shape = gi_ref.shape
    lane = lax.broadcasted_iota(jnp.int32, shape, 1)
    valid = lane < CHUNK
    i_pre = gi_ref[...] + ib_ref[...]
    f_pre = gf_ref[...] + fb_ref[...]
    lf = jnp.minimum(f_pre, 0.0) - jnp.log1p(jnp.exp(-jnp.abs(f_pre)))
    lf = jnp.where(valid, lf, 0.0)
    b = _lane_prefix(lf, jnp.add, 0.0, lane)
    r = jnp.where(valid, i_pre - b, NEG_INF)
    cm = _lane_prefix(r, jnp.maximum, NEG_INF, lane)
    g_tot = jnp.sum(jnp.where(lane == CHUNK - 1, b, 0.0), axis=1, keepdims=True)
    c_last = jnp.max(r, axis=1, keepdims=True)
    g_sc[...] = jnp.broadcast_to(g_tot, shape)
    c_sc[...] = jnp.broadcast_to(c_last, shape)

    def body(c, m):
        rows = pl.ds(pl.multiple_of(c * n_bh, n_bh), n_bh)
        m_sc[rows, :] = m
        return g_sc[rows, :] + jnp.maximum(m, c_sc[rows, :])

    lax.fori_loop(0, n_chunks, body, jnp.zeros((n_bh, shape[1]), F32))
    m = m_sc[...]
    big_m = jnp.maximum(m, cm)
    m_last = jnp.maximum(m, c_sc[...])
    r_ref[...] = jnp.where(valid, r, 0.0)
    mt_ref[...] = big_m
    wi_ref[...] = jnp.exp(m - big_m)
    et_ref[...] = jnp.exp(-b - big_m)
    ws_ref[...] = jnp.exp(r - m_last)
    dc_ref[...] = jnp.exp(m - m_last)


def _gate_prep(gi, gf, ib, fb, n_chunks, n_bh):
    shape = jax.ShapeDtypeStruct(gi.shape, F32)
    return pl.pallas_call(
        functools.partial(_gate_prep_kernel, n_chunks=n_chunks, n_bh=n_bh),
        out_shape=(shape,) * 6,
        scratch_shapes=[pltpu.VMEM(gi.shape, F32)] * 3,
        compiler_params=_params(None, 32),
        name="gate_prep",
    )(gi, gf, ib, fb)


_QK_HALO = BF16_SUBLANES


def _mlstm_kernel(qk_ref, halo_ref, v_ref, o_ref, cw_ref, cb_ref, cols_ref, rows_ref, ng_ref,
                  out_ref, xs_ref, c_ref, n_ref, *, heads, dh, chunks_per_step):
    group_id = pl.program_id(1)
    dm = heads * dh
    n_slabs = 2 * dm // LANES

    @pl.when(group_id == 0)
    def _():
        c_ref[...] = jnp.zeros(c_ref.shape, F32)
        n_ref[...] = jnp.zeros(n_ref.shape, F32)

    for s in range(n_slabs):
        lanes = slice(s * LANES, (s + 1) * LANES)
        xs_ref[s, 0:_QK_HALO, :] = jnp.where(group_id == 0, 0.0, halo_ref[:, lanes].astype(F32))
        xs_ref[s, _QK_HALO:, :] = qk_ref[:, lanes].astype(F32)

    def conv_silu(col0, row0):
        parts = []
        for s in range(col0 // LANES, (col0 + dh) // LANES):
            lanes = slice(s * LANES, (s + 1) * LANES)
            acc = cb_ref[:, lanes]
            for j in range(MLSTM_CONV):
                off = row0 + _QK_HALO - (MLSTM_CONV - 1) + j
                acc = acc + cw_ref[j:j + 1, lanes] * xs_ref[s, off:off + CHUNK, :]
            parts.append(acc * _sigmoid(acc))
        return jnp.concatenate(parts, axis=1)

    tri = (lax.broadcasted_iota(jnp.int32, (CHUNK, CHUNK), 0)
           >= lax.broadcasted_iota(jnp.int32, (CHUNK, CHUNK), 1))
    eye = (lax.broadcasted_iota(jnp.int32, (dh, dh), 0)
           == lax.broadcasted_iota(jnp.int32, (dh, dh), 1)).astype(BF16)
    for cc in range(chunks_per_step):
        row0 = cc * CHUNK
        rs = slice(row0, row0 + CHUNK)
        for h in range(heads):
            hs = slice(h * dh, (h + 1) * dh)
            q = conv_silu(h * dh, row0)
            k = conv_silu(dm + h * dh, row0) * (dh ** -0.5)
            qb = q.astype(BF16)
            kb = k.astype(BF16)
            v = v_ref[rs, hs]
            mt = cols_ref[cc, :, h:h + 1]
            wi = cols_ref[cc, :, heads + h:heads + h + 1]
            et = cols_ref[cc, :, 2 * heads + h:2 * heads + h + 1]
            ws = cols_ref[cc, :, 3 * heads + h:3 * heads + h + 1]
            r = rows_ref[cc, h:h + 1, 0:CHUNK]
            dc = rows_ref[cc, heads + h:heads + h + 1, 0:1]

            s = lax.dot_general(qb, kb, _NT, preferred_element_type=F32)
            sw = s * jnp.where(tri, jnp.exp(r - mt), 0.0)
            c_old = c_ref[h]
            inter = lax.dot_general(qb, c_old.astype(BF16), _NT, preferred_element_type=F32)
            intra = jnp.dot(sw.astype(BF16), v, preferred_element_type=F32)
            n_old = n_ref[h:h + 1, :]
            num = intra + wi * inter
            den = (jnp.sum(sw, axis=-1, keepdims=True)
                   + wi * jnp.sum(q * n_old, axis=-1, keepdims=True))
            hh = num / jnp.maximum(jnp.abs(den), et)
            hn = hh * lax.rsqrt(jnp.mean(hh * hh, axis=-1, keepdims=True) + EPS) * ng_ref[:, hs]
            out_ref[rs, hs] = (_sigmoid(o_ref[rs, hs].astype(F32)) * hn).astype(out_ref.dtype)

            vw = (v.astype(F32) * ws).astype(BF16)
            vw_t = lax.dot_general(eye, vw, _NT, preferred_element_type=F32).astype(BF16)
            upd = jnp.dot(vw_t, kb, preferred_element_type=F32)
            c_ref[h] = dc * c_old + upd
            n_ref[h:h + 1, :] = dc * n_old + jnp.sum(k * ws, axis=0, keepdims=True)


def _mlstm(proj, conv_w, conv_b, cols, rows, norm_g, batch, n_chunks, chunks_per_step):
    t = proj.shape[0]
    dm = norm_g.shape[1]
    dh = dm // MLSTM_HEADS
    n_groups = n_chunks // chunks_per_step
    rows_per_step = chunks_per_step * CHUNK
    halo_blocks = rows_per_step // _QK_HALO
    return pl.pallas_call(
        functools.partial(_mlstm_kernel, heads=MLSTM_HEADS, dh=dh, chunks_per_step=chunks_per_step),
        out_shape=jax.ShapeDtypeStruct((t, dm), BF16),
        grid=(batch, n_groups),
        in_specs=[
            pl.BlockSpec((rows_per_step, 2 * dm), lambda b, c: (b * n_groups + c, 0)),
            pl.BlockSpec((_QK_HALO, 2 * dm),
                         lambda b, c: (jnp.maximum((b * n_groups + c) * halo_blocks - 1, 0), 0)),
            pl.BlockSpec((rows_per_step, dm), lambda b, c: (b * n_groups + c, 2)),
            pl.BlockSpec((rows_per_step, dm), lambda b, c: (b * n_groups + c, 3)),
            pl.BlockSpec((MLSTM_CONV, 2 * dm), lambda b, c: (0, 0)),
            pl.BlockSpec((1, 2 * dm), lambda b, c: (0, 0)),
            pl.BlockSpec((None, chunks_per_step, CHUNK, LANES), lambda b, c: (b, c, 0, 0)),
            pl.BlockSpec((None, chunks_per_step, 2 * MLSTM_HEADS, LANES), lambda b, c: (b, c, 0, 0)),
            pl.BlockSpec((1, dm), lambda b, c: (0, 0)),
        ],
        out_specs=pl.BlockSpec((rows_per_step, dm), lambda b, c: (b * n_groups + c, 0)),
        scratch_shapes=[
            pltpu.VMEM((2 * dm // LANES, _QK_HALO + rows_per_step, LANES), F32),
            pltpu.VMEM((MLSTM_HEADS, dh, dh), F32),
            pltpu.VMEM((2 * MLSTM_HEADS, dh), F32),
        ],
        compiler_params=_params(("arbitrary", "arbitrary"), 32),
        name="mlstm",
    )(proj, proj, proj, proj, conv_w, conv_b, cols, rows, norm_g)


def _attn_kernel(*refs, heads, dh, n_pieces, n_cast):
    q_ref = refs[0]
    k_refs = refs[1:1 + n_pieces]
    v_refs = refs[1 + n_pieces:1 + 2 * n_pieces]
    tab_refs = refs[1 + 2 * n_pieces:1 + 3 * n_pieces]
    cast_in = refs[1 + 3 * n_pieces:1 + 3 * n_pieces + n_cast]
    out_ref = refs[1 + 3 * n_pieces + n_cast]
    cast_out = refs[2 + 3 * n_pieces + n_cast:]
    for src, dst in zip(cast_in, cast_out):
        dst[...] = src[...].astype(BF16)
    scale2 = (dh ** -0.5) * _LOG2_E
    for h in range(heads):
        hs = slice(h * dh, (h + 1) * dh)
        q = q_ref[:, hs]
        scores = [lax.dot_general(q, k_ref[:, hs], _NT, preferred_element_type=F32) * scale2 + tab_ref[h]
                  for k_ref, tab_ref in zip(k_refs, tab_refs)]
        m = jnp.max(functools.reduce(jnp.maximum, scores), axis=-1, keepdims=True)
        es = [jnp.exp2(s - m) for s in scores]
        denom = jnp.sum(functools.reduce(jnp.add, es), axis=-1, keepdims=True)
        acc = functools.reduce(jnp.add, [
            jnp.dot(e.astype(BF16), v_ref[:, hs], preferred_element_type=F32) for e, v_ref in zip(es, v_refs)])
        out_ref[:, hs] = (acc / denom).astype(out_ref.dtype)


def _attention(proj, tab, d_attn, batch, seq, q_rows, cast_weights):
    t = proj.shape[0]
    dh = d_attn // ATTN_HEADS
    n_q = seq // q_rows
    n_steps = batch * n_q
    pad_blocks = (LEFT_CHUNKS * CHUNK) // q_rows
    n_pieces = 1 + pad_blocks
    assert tab.shape == (n_pieces + 1, ATTN_HEADS, q_rows, q_rows)
    q_col = proj.shape[1] // d_attn - 3

    def window_spec(col_block, p):
        return pl.BlockSpec(
            (q_rows, d_attn),
            lambda b, qi: (b * n_q + jnp.maximum(qi + p - pad_blocks, 0), col_block))

    def tab_spec(p):
        return pl.BlockSpec(
            (None, ATTN_HEADS, q_rows, q_rows),
            lambda b, qi: (jnp.where(qi + p >= pad_blocks, p, n_pieces), 0, 0, 0))

    def cast_spec(w):
        assert w.shape[0] % (n_steps * BF16_SUBLANES) == 0
        return pl.BlockSpec((w.shape[0] // n_steps, w.shape[1]), lambda b, qi: (b * n_q + qi, 0))

    cast_specs = [cast_spec(w) for w in cast_weights]
    outs = pl.pallas_call(
        functools.partial(_attn_kernel, heads=ATTN_HEADS, dh=dh, n_pieces=n_pieces, n_cast=len(cast_weights)),
        out_shape=(jax.ShapeDtypeStruct((t, d_attn), BF16),
                   *[jax.ShapeDtypeStruct(w.shape, BF16) for w in cast_weights]),
        grid=(batch, n_q),
        in_specs=[pl.BlockSpec((q_rows, d_attn), lambda b, qi: (b * n_q + qi, q_col))]
        + [window_spec(q_col + 1, p) for p in range(n_pieces)]
        + [window_spec(q_col + 2, p) for p in range(n_pieces)]
        + [tab_spec(p) for p in range(n_pieces)]
        + cast_specs,
        out_specs=(pl.BlockSpec((q_rows, d_attn), lambda b, qi: (b * n_q + qi, 0)), *cast_specs),
        compiler_params=_params(("arbitrary", "arbitrary"), 48),
        name="chunk_attn",
    )(proj, *([proj] * (2 * n_pieces)), *([tab] * n_pieces), *cast_weights)
    return outs[0], outs[1:]


def _bias_table(rel, q_rows):
    heads = rel.shape[0]
    pad_rows = LEFT_CHUNKS * CHUNK
    width = q_rows + pad_rows
    d_lo, d_hi = pad_rows - width + 1, pad_rows + q_rows - 1
    mid = rel[:, max(d_lo, -MAX_REL) + MAX_REL:min(d_hi, MAX_REL) + MAX_REL + 1]
    left = jnp.repeat(rel[:, :1], max(0, -MAX_REL - d_lo), axis=1)
    right = jnp.repeat(rel[:, -1:], max(0, d_hi - MAX_REL), axis=1)
    by_dist = jnp.concatenate([left, mid, right], axis=1)[:, ::-1]
    length = by_dist.shape[1]
    ring = jnp.pad(by_dist, ((0, 0), (0, 1)))
    skew = jnp.tile(ring, (1, q_rows))[:, :q_rows * length].reshape(heads, q_rows, length)
    bias = skew[:, :, q_rows - 1:q_rows - 1 + width]
    q_chunk = jnp.arange(q_rows)[:, None] // CHUNK
    k_chunk = jnp.arange(width)[None, :] // CHUNK
    in_band = (k_chunk >= q_chunk) & (k_chunk <= q_chunk + LEFT_CHUNKS)
    tab = jnp.where(in_band[None], bias * _LOG2_E, NEG_INF)
    pieces = [tab[:, :, p * q_rows:(p + 1) * q_rows] for p in range(width // q_rows)]
    pieces.append(jnp.full((heads, q_rows, q_rows), NEG_INF, F32))
    return jnp.stack(pieces, axis=0)


def _out_proj_kernel(a_ref, b_ref, wa_ref, wb_ref, x_ref, g_ref, *refs, n_cast):
    cast_in = refs[:n_cast]
    o_ref, u_ref = refs[n_cast:n_cast + 2]
    cast_out = refs[n_cast + 2:]
    for src, dst in zip(cast_in, cast_out):
        dst[...] = src[...].astype(BF16)
    acc = jnp.dot(a_ref[...], wa_ref[...], preferred_element_type=F32)
    acc = acc + jnp.dot(b_ref[...], wb_ref[...], preferred_element_type=F32)
    h = x_ref[...] + acc
    o_ref[...] = h
    u_ref[...] = _rmsnorm_f32(h, g_ref[...]).astype(BF16)


def _out_proj(h_a, h_b, w, x2, g_next, tm, cast_weights):
    t, d = x2.shape
    ka, kb = h_a.shape[1], h_b.shape[1]
    assert ka == kb
    n_steps = t // tm

    def cast_spec(wc):
        assert wc.shape[0] % (n_steps * BF16_SUBLANES) == 0
        return pl.BlockSpec((wc.shape[0] // n_steps, wc.shape[1]), lambda i: (i, 0))

    cast_specs = [cast_spec(wc) for wc in cast_weights]
    row_tile = pl.BlockSpec((tm, d), lambda i: (i, 0))
    outs = pl.pallas_call(
        functools.partial(_out_proj_kernel, n_cast=len(cast_weights)),
        out_shape=(jax.ShapeDtypeStruct((t, d), F32), jax.ShapeDtypeStruct((t, d), BF16),
                   *[jax.ShapeDtypeStruct(wc.shape, BF16) for wc in cast_weights]),
        grid=(n_steps,),
        in_specs=[
            pl.BlockSpec((tm, ka), lambda i: (i, 0)),
            pl.BlockSpec((tm, kb), lambda i: (i, 0)),
            pl.BlockSpec((None, ka, d), lambda i: (0, 0, 0)),
            pl.BlockSpec((None, kb, d), lambda i: (1, 0, 0)),
            row_tile,
            pl.BlockSpec((1, d), lambda i: (0, 0)),
            *cast_specs,
        ],
        out_specs=(row_tile, row_tile, *cast_specs),
        compiler_params=_params(("arbitrary",), 56),
        name="out_proj",
    )(h_a, h_b, w, w, x2, g_next, *cast_weights)
    return outs[0], outs[1], outs[2:]


def _mlp_kernel(h_ref, u_ref, w1_ref, w2_ref, gp_ref, *out_refs, final_norm):
    o_ref = out_refs[0]
    j = pl.program_id(1)

    @pl.when(j == 0)
    def _():
        o_ref[...] = h_ref[...]

    a = jnp.dot(u_ref[...], w1_ref[...], preferred_element_type=F32)
    a = jnp.square(jnp.maximum(a, 0.0)).astype(BF16)
    o_ref[...] += jnp.dot(a, w2_ref[...], preferred_element_type=F32)

    @pl.when(j == pl.num_programs(1) - 1)
    def _():
        normed = _rmsnorm_f32(o_ref[...], gp_ref[...])
        if final_norm:
            o_ref[...] = normed
        else:
            out_refs[1][...] = normed.astype(BF16)


def _mlp(h2, u, w1, w2, layer, g_post, final_norm, tm, tf):
    t, d = h2.shape
    f = w1.shape[2]
    row_tile = pl.BlockSpec((tm, d), lambda i, j: (i, 0))
    out_shape = [jax.ShapeDtypeStruct((t, d), F32)]
    if not final_norm:
        out_shape.append(jax.ShapeDtypeStruct((t, d), BF16))
    return pl.pallas_call(
        functools.partial(_mlp_kernel, final_norm=final_norm),
        out_shape=tuple(out_shape),
        grid=(t // tm, f // tf),
        in_specs=[
            row_tile,
            row_tile,
            pl.BlockSpec((None, d, tf), lambda i, j: (layer, 0, j)),
            pl.BlockSpec((None, tf, d), lambda i, j: (layer, j, 0)),
            pl.BlockSpec((1, d), lambda i, j: (0, 0)),
        ],
        out_specs=tuple([row_tile] * len(out_shape)),
        compiler_params=_params(("arbitrary", "arbitrary"), 56),
        name="mlp_final" if final_norm else "mlp",
    )(h2, u, w1, w2, g_post)


def _glu_kernel(u_ref, wa_ref, wg_ref, ba_ref, bg_ref, o_ref):
    u = u_ref[...]
    a = jnp.dot(u, wa_ref[...], preferred_element_type=F32) + ba_ref[...]
    gate = jnp.dot(u, wg_ref[...], preferred_element_type=F32) + bg_ref[...]
    o_ref[...] = (a * _sigmoid(gate)).astype(o_ref.dtype)


def _glu(u, pw1, pw1_b, tm, tn):
    t, d = u.shape
    n = pw1.shape[2] // 2
    nb = n // tn
    return pl.pallas_call(
        _glu_kernel,
        out_shape=jax.ShapeDtypeStruct((t, n), BF16),
        grid=(t // tm, nb),
        in_specs=[
            pl.BlockSpec((tm, d), lambda i, j: (i, 0)),
            pl.BlockSpec((None, d, tn), lambda i, j: (0, 0, j)),
            pl.BlockSpec((None, d, tn), lambda i, j: (0, 0, j + nb)),
            pl.BlockSpec((1, tn), lambda i, j: (0, j)),
            pl.BlockSpec((1, tn), lambda i, j: (0, j + nb)),
        ],
        out_specs=pl.BlockSpec((tm, tn), lambda i, j: (i, j)),
        compiler_params=_params(("arbitrary", "arbitrary"), 56),
        name="conv_glu",
    )(u, pw1, pw1, pw1_b, pw1_b)


_DW_HALO = 2 * BF16_SUBLANES
_DW_ROWS = 64


def _conv_tail_kernel(z_ref, halo_ref, dw_ref, dwb_ref, lng_ref, lnb_ref, w2_ref, b2_ref, h_ref, g_ref,
                      o_ref, u_ref, zs_ref, y_ref, *, tm, tiles_per_seq):
    i = pl.program_id(0)
    n_slabs = z_ref.shape[1] // LANES
    seq_start = (i % tiles_per_seq) == 0
    for s in range(n_slabs):
        lanes = slice(s * LANES, (s + 1) * LANES)
        zs_ref[s, 0:_DW_HALO, :] = jnp.where(seq_start, 0.0, halo_ref[:, lanes].astype(F32))
        zs_ref[s, _DW_HALO:, :] = z_ref[:, lanes].astype(F32)
    first_tap = _DW_HALO - (CONV_WIDTH - 1)

    def slab_body(s, carry):
        lanes = pl.ds(pl.multiple_of(s * LANES, LANES), LANES)
        for rb in range(tm // _DW_ROWS):
            r0 = rb * _DW_ROWS
            acc = jnp.broadcast_to(dwb_ref[:, lanes], (_DW_ROWS, LANES))
            for k in range(CONV_WIDTH):
                acc = acc + dw_ref[k:k + 1, lanes] * zs_ref[s, r0 + first_tap + k:r0 + first_tap + k + _DW_ROWS, :]
            y_ref[r0:r0 + _DW_ROWS, lanes] = acc
        return carry

    lax.fori_loop(0, n_slabs, slab_body, 0)
    y = y_ref[...]
    mu = jnp.mean(y, axis=-1, keepdims=True)
    yc = y - mu
    var = jnp.mean(yc * yc, axis=-1, keepdims=True)
    yn = yc * lax.rsqrt(var + EPS) * lng_ref[...] + lnb_ref[...]
    a = (yn * _sigmoid(yn)).astype(BF16)
    h = h_ref[...] + jnp.dot(a, w2_ref[...], preferred_element_type=F32) + b2_ref[...]
    o_ref[...] = h
    u_ref[...] = _rmsnorm_f32(h, g_ref[...]).astype(BF16)


def _conv_tail(z, dw_w, dw_b, ln_g, ln_b, pw2, pw2_b, h2, g_next, seq, tm):
    t, d = h2.shape
    halo_blocks = tm // _DW_HALO
    row_tile = pl.BlockSpec((tm, d), lambda i: (i, 0))
    vec = pl.BlockSpec((1, d), lambda i: (0, 0))
    return pl.pallas_call(
        functools.partial(_conv_tail_kernel, tm=tm, tiles_per_seq=seq // tm),
        out_shape=(jax.ShapeDtypeStruct((t, d), F32), jax.ShapeDtypeStruct((t, d), BF16)),
        grid=(t // tm,),
        in_specs=[
            row_tile,
            pl.BlockSpec((_DW_HALO, d), lambda i: (jnp.maximum(i * halo_blocks - 1, 0), 0)),
            pl.BlockSpec(dw_w.shape, lambda i: (0, 0)),
            vec, vec, vec,
            pl.BlockSpec((None, d, d), lambda i: (0, 0, 0), pipeline_mode=pl.Buffered(1)),
            vec,
            row_tile,
            vec,
        ],
        out_specs=(row_tile, row_tile),
        scratch_shapes=[pltpu.VMEM((d // LANES, _DW_HALO + tm, LANES), F32), pltpu.VMEM((tm, d), F32)],
        compiler_params=_params(("arbitrary",), 56),
        name="conv_tail",
    )(z, z, dw_w, dw_b, ln_g, ln_b, pw2, pw2_b, h2, g_next)


def _tiles(seq):
    def rows(want):
        return min(want, seq)

    return dict(
        repack_rows=512,
        in_proj_tm=rows(512),
        mlstm_chunks_per_step=min(4, seq // CHUNK),
        attn_q_rows=rows(256),
        out_proj_tm=rows(512),
        mlp_tm=rows(512), mlp_tf=1024,
        glu_tm=rows(1024), glu_tn=1024,
        conv_tm=rows(512),
    )


def _row(v):
    return v.reshape(1, -1).astype(F32)


def kernel(x, mixer_norm_g, mix_w_in, qk_conv_w, qk_conv_b, igate_b, fgate_b, mlstm_norm_g, rel_bias,
           mix_w_out, conv_pw1_w, conv_pw1_b, conv_dw_w, conv_dw_b, conv_ln_g, conv_ln_b, conv_pw2_w,
           conv_pw2_b, mlp_norm_g, mlp_w1, mlp_w2, final_norm_g):
    batch, seq, d = x.shape
    tokens = batch * seq
    n_chunks = seq // CHUNK
    dm = d // 2
    da = d - dm
    heads = MLSTM_HEADS
    tl = _tiles(seq)
    x2 = x.reshape(tokens, d)

    w_in_t = jnp.swapaxes(mix_w_in, 1, 2)
    gate_lo = 4 * dm
    gate_hi = gate_lo + 2 * heads
    w_main_t = _drop_rows_bf16(w_in_t, gate_lo, gate_hi, tl["repack_rows"])
    w_gate_t = jnp.pad(w_in_t[0, gate_lo:gate_hi], ((0, LANES - 2 * heads), (0, 0))).astype(BF16)
    proj, gates = _in_proj(x2, _row(mixer_norm_g[0]), w_main_t, w_gate_t, tl["in_proj_tm"],
                           w_main_t.shape[0] // 2)

    n_bh = batch * heads
    g8 = gates[:, :2 * heads].reshape(batch, n_chunks, CHUNK, 2, heads)
    g8 = jnp.transpose(g8, (3, 1, 0, 4, 2)).reshape(2, n_chunks * n_bh, CHUNK)
    g8 = jnp.pad(g8, ((0, 0), (0, 0), (0, LANES - CHUNK)))
    bias_rows = lambda bvec: jnp.broadcast_to(
        jnp.tile(bvec.astype(F32), batch * n_chunks)[:, None], (n_chunks * n_bh, LANES))
    r, mt, wi, et, ws, dc = _gate_prep(g8[0], g8[1], bias_rows(igate_b[0]), bias_rows(fgate_b[0]),
                                       n_chunks, n_bh)
    per_frame = jnp.stack([mt, wi, et, ws], axis=0)[:, :, :CHUNK]
    per_frame = per_frame.reshape(4, n_chunks, batch, heads, CHUNK)
    cols = jnp.transpose(per_frame, (2, 1, 4, 0, 3)).reshape(batch, n_chunks, CHUNK, 4 * heads)
    cols = jnp.pad(cols, ((0, 0), (0, 0), (0, 0), (0, LANES - 4 * heads)))
    rows = jnp.concatenate([r.reshape(n_chunks, batch, heads, LANES),
                            dc.reshape(n_chunks, batch, heads, LANES)], axis=2)
    rows = jnp.transpose(rows, (1, 0, 2, 3))

    h_a = _mlstm(proj, qk_conv_w[0].astype(F32), _row(qk_conv_b[0]), cols, rows,
                 _row(mlstm_norm_g[0]), batch, n_chunks, tl["mlstm_chunks_per_step"])

    tab = _bias_table(rel_bias[0].astype(F32), tl["attn_q_rows"])
    h_b, (w1_flat, w2_flat) = _attention(
        proj, tab, da, batch, seq, tl["attn_q_rows"],
        [mlp_w1.reshape(-1, mlp_w1.shape[2]), mlp_w2.reshape(-1, mlp_w2.shape[2])])
    mlp_w1_bf = w1_flat.reshape(mlp_w1.shape)
    mlp_w2_bf = w2_flat.reshape(mlp_w2.shape)

    h, u, (pw1_flat, pw2_flat) = _out_proj(
        h_a, h_b, mix_w_out.astype(BF16).reshape(2, dm, d), x2, _row(mlp_norm_g[0]), tl["out_proj_tm"],
        [conv_pw1_w.reshape(-1, conv_pw1_w.shape[2]), conv_pw2_w.reshape(-1, conv_pw2_w.shape[2])])
    pw1_bf = pw1_flat.reshape(conv_pw1_w.shape)
    pw2_bf = pw2_flat.reshape(conv_pw2_w.shape)

    h, u = _mlp(h, u, mlp_w1_bf, mlp_w2_bf, 0, _row(mixer_norm_g[1]), False, tl["mlp_tm"], tl["mlp_tf"])

    z = _glu(u, pw1_bf, _row(conv_pw1_b[0]), tl["glu_tm"], tl["glu_tn"])
    h, u = _conv_tail(z, conv_dw_w[0].astype(F32), _row(conv_dw_b[0]), _row(conv_ln_g[0]),
                      _row(conv_ln_b[0]), pw2_bf, _row(conv_pw2_b[0]), h,
                      _row(mlp_norm_g[1]), seq, tl["conv_tm"])

    (out,) = _mlp(h, u, mlp_w1_bf, mlp_w2_bf, 1, _row(final_norm_g), True, tl["mlp_tm"], tl["mlp_tf"])
    return out.reshape(batch, seq, d)
```

```python
import functools
import math

import jax
import jax.numpy as jnp
from jax import lax
from jax.experimental import pallas as pl
from jax.experimental.pallas import tpu as pltpu

CHUNK = 64
MLSTM_HEADS = 4
MLSTM_CONV = 4
ATTN_HEADS = 8
LEFT_CHUNKS = 8
MAX_REL = 256
CONV_WIDTH = 31
EPS = 1e-6

V7X_VMEM_BYTES = 64 * 1024 * 1024
LANES = 128
BF16_SUBLANES = 16

F32 = jnp.float32
BF16 = jnp.bfloat16
NEG_INF = float("-inf")
_LOG2_E = math.log2(math.e)

_NT = (((1,), (1,)), ((), ()))
_TN = (((0,), (0,)), ((), ()))


def _params(semantics, vmem_mib):
    assert vmem_mib * 1024 * 1024 <= V7X_VMEM_BYTES
    return pltpu.CompilerParams(dimension_semantics=semantics,
                                vmem_limit_bytes=vmem_mib * 1024 * 1024)


def _rmsnorm_f32(x, g):
    return x * lax.rsqrt(jnp.mean(x * x, axis=-1, keepdims=True) + EPS) * g


def _sigmoid(x):
    return 0.5 * jnp.tanh(0.5 * x) + 0.5


def _drop_rows_kernel(w_ref, o_ref):
    o_ref[...] = w_ref[0].astype(BF16)


def _drop_rows_bf16(w_t, lo, hi, rows):
    n_in, d = w_t.shape[1:]
    n = n_in - (hi - lo)
    assert lo % rows == 0 and n % rows == 0 and (hi - lo) % 8 == 0
    return pl.pallas_call(
        _drop_rows_kernel,
        out_shape=jax.ShapeDtypeStruct((n, d), BF16),
        grid=(n // rows,),
        in_specs=[pl.BlockSpec((pl.Element(1), pl.Element(rows), pl.Element(d)),
                               lambda i: (0, pl.multiple_of(jnp.where(i * rows < lo, i * rows, i * rows + (hi - lo)), 8), 0))],
        out_specs=pl.BlockSpec((rows, d), lambda i: (i, 0)),
        compiler_params=_params(("arbitrary",), 32),
        name="w_in_repack",
    )(w_t)


def _in_proj_kernel(x_ref, g_ref, w_ref, wg_ref, o_ref, og_ref, u_ref):
    @pl.when(pl.program_id(1) == 0)
    def _():
        u = _rmsnorm_f32(x_ref[...], g_ref[...]).astype(BF16)
        u_ref[...] = u
        og_ref[...] = lax.dot_general(u, wg_ref[...], _NT, preferred_element_type=F32)

    o_ref[...] = lax.dot_general(u_ref[...], w_ref[...], _NT, preferred_element_type=F32).astype(o_ref.dtype)


def _in_proj(x2, g, w_main_t, w_gate_t, tm, tn):
    t, d = x2.shape
    n = w_main_t.shape[0]
    return pl.pallas_call(
        _in_proj_kernel,
        out_shape=(jax.ShapeDtypeStruct((t, n), BF16), jax.ShapeDtypeStruct((t, LANES), F32)),
        grid=(t // tm, n // tn),
        in_specs=[
            pl.BlockSpec((tm, d), lambda i, j: (i, 0)),
            pl.BlockSpec((1, d), lambda i, j: (0, 0)),
            pl.BlockSpec((tn, d), lambda i, j: (j, 0)),
            pl.BlockSpec((LANES, d), lambda i, j: (0, 0)),
        ],
        out_specs=(pl.BlockSpec((tm, tn), lambda i, j: (i, j)),
                   pl.BlockSpec((tm, LANES), lambda i, j: (i, 0))),
        scratch_shapes=[pltpu.VMEM((tm, d), BF16)],
        compiler_params=_params(("arbitrary", "arbitrary"), 56),
        name="in_proj",
    )(x2, g, w_main_t, w_gate_t)


def _lane_prefix(x, op, ident, lane):
    shift = 1
    while shift < CHUNK:
        y = pltpu.roll(x, shift, axis=1)
        x = op(x, jnp.where(lane >= shift, y, ident))
        shift *= 2
    return x


def _gate_prep_kernel(gi_ref, gf_ref, ib_ref, fb_ref,
                      r_ref, mt_ref, wi_ref, et_ref, ws_ref, dc_ref,
                      g_sc, c_sc, m_sc, *, n_chunks, n_bh):
    shape = gi_ref.shape
    lane = lax.broadcasted_iota(jnp.int32, shape, 1)
    valid = lane < CHUNK
    i_pre = gi_ref[...] + ib_ref[...]
    f_pre = gf_ref[...] + fb_ref[...]
    lf = jnp.minimum(f_pre, 0.0) - jnp.log1p(jnp.exp(-jnp.abs(f_pre)))
    lf = jnp.where(valid, lf, 0.0)
    b = _lane_prefix(lf, jnp.add, 0.0, lane)
    r = jnp.where(valid, i_pre - b, NEG_INF)
    cm = _lane_prefix(r, jnp.maximum, NEG_INF, lane)
    g_tot = jnp.sum(jnp.where(lane == CHUNK - 1, b, 0.0), axis=1, keepdims=True)
    c_last = jnp.max(r, axis=1, keepdims=True)
    g_sc[...] = jnp.broadcast_to(g_tot, shape)
    c_sc[...] = jnp.broadcast_to(c_last, shape)

    def body(c, m):
        rows = pl.ds(pl.multiple_of(c * n_bh, n_bh), n_bh)
        m_sc[rows, :] = m
        return g_sc[rows, :] + jnp.maximum(m, c_sc[rows, :])

    lax.fori_loop(0, n_chunks, body, jnp.zeros((n_bh, shape[1]), F32))
    m = m_sc[...]
    big_m = jnp.maximum(m, cm)
    m_last = jnp.maximum(m, c_sc[...])
    r_ref[...] = jnp.where(valid, r, 0.0)
    mt_ref[...] = big_m
    wi_ref[...] = jnp.exp(m - big_m)
    et_ref[...] = jnp.exp(-b - big_m)
    ws_ref[...] = jnp.exp(r - m_last)
    dc_ref[...] = jnp.exp(m - m_last)


def _gate_prep(gi, gf, ib, fb, n_chunks, n_bh):
    shape = jax.ShapeDtypeStruct(gi.shape, F32)
    return pl.pallas_call(
        functools.partial(_gate_prep_kernel, n_chunks=n_chunks, n_bh=n_bh),
        out_shape=(shape,) * 6,
        scratch_shapes=[pltpu.VMEM(gi.shape, F32)] * 3,
        compiler_params=_params(None, 32),
        name="gate_prep",
    )(gi, gf, ib, fb)


_QK_HALO = BF16_SUBLANES


def _mlstm_kernel(qk_ref, halo_ref, v_ref, o_ref, cw_ref, cb_ref, cols_ref, rows_ref, ng_ref, *refs,
                  heads, dh, chunks_per_step, n_cast):
    cast_in = refs[:n_cast]
    out_ref = refs[n_cast]
    cast_out = refs[n_cast + 1:2 * n_cast + 1]
    xs_ref, c_ref, n_ref = refs[2 * n_cast + 1:]
    for src, dst in zip(cast_in, cast_out):
        dst[...] = src[...].astype(BF16)
    group_id = pl.program_id(1)
    dm = heads * dh
    n_slabs = 2 * dm // LANES

    @pl.when(group_id == 0)
    def _():
        c_ref[...] = jnp.zeros(c_ref.shape, F32)
        n_ref[...] = jnp.zeros(n_ref.shape, F32)

    for s in range(n_slabs):
        lanes = slice(s * LANES, (s + 1) * LANES)
        xs_ref[s, 0:_QK_HALO, :] = jnp.where(group_id == 0, 0.0, halo_ref[:, lanes].astype(F32))
        xs_ref[s, _QK_HALO:, :] = qk_ref[:, lanes].astype(F32)

    def conv_silu(col0, row0):
        parts = []
        for s in range(col0 // LANES, (col0 + dh) // LANES):
            lanes = slice(s * LANES, (s + 1) * LANES)
            acc = cb_ref[:, lanes]
            for j in range(MLSTM_CONV):
                off = row0 + _QK_HALO - (MLSTM_CONV - 1) + j
                acc = acc + cw_ref[j:j + 1, lanes] * xs_ref[s, off:off + CHUNK, :]
            parts.append(acc * _sigmoid(acc))
        return jnp.concatenate(parts, axis=1)

    tri = (lax.broadcasted_iota(jnp.int32, (CHUNK, CHUNK), 0)
           >= lax.broadcasted_iota(jnp.int32, (CHUNK, CHUNK), 1))
    eye = (lax.broadcasted_iota(jnp.int32, (dh, dh), 0)
           == lax.broadcasted_iota(jnp.int32, (dh, dh), 1)).astype(BF16)
    for cc in range(chunks_per_step):
        row0 = cc * CHUNK
        rs = slice(row0, row0 + CHUNK)
        for h in range(heads):
            hs = slice(h * dh, (h + 1) * dh)
            q = conv_silu(h * dh, row0)
            k = conv_silu(dm + h * dh, row0) * (dh ** -0.5)
            qb = q.astype(BF16)
            kb = k.astype(BF16)
            v = v_ref[rs, hs]
            mt = cols_ref[cc, :, h:h + 1]
            wi = cols_ref[cc, :, heads + h:heads + h + 1]
            et = cols_ref[cc, :, 2 * heads + h:2 * heads + h + 1]
            ws = cols_ref[cc, :, 3 * heads + h:3 * heads + h + 1]
            r = rows_ref[cc, h:h + 1, 0:CHUNK]
            dc = rows_ref[cc, heads + h:heads + h + 1, 0:1]

            s = lax.dot_general(qb, kb, _NT, preferred_element_type=F32)
            sw = s * jnp.where(tri, jnp.exp(r - mt), 0.0)
            c_old = c_ref[h]
            inter = lax.dot_general(qb, c_old.astype(BF16), _NT, preferred_element_type=F32)
            intra = jnp.dot(sw.astype(BF16), v, preferred_element_type=F32)
            n_old = n_ref[h:h + 1, :]
            num = intra + wi * inter
            den = (jnp.sum(sw, axis=-1, keepdims=True)
                   + wi * jnp.sum(q * n_old, axis=-1, keepdims=True))
            hh = num / jnp.maximum(jnp.abs(den), et)
            hn = hh * lax.rsqrt(jnp.mean(hh * hh, axis=-1, keepdims=True) + EPS) * ng_ref[:, hs]
            out_ref[rs, hs] = (_sigmoid(o_ref[rs, hs].astype(F32)) * hn).astype(out_ref.dtype)

            vw = (v.astype(F32) * ws).astype(BF16)
            vw_t = lax.dot_general(eye, vw, _NT, preferred_element_type=F32).astype(BF16)
            upd = jnp.dot(vw_t, kb, preferred_element_type=F32)
            c_ref[h] = dc * c_old + upd
            n_ref[h:h + 1, :] = dc * n_old + jnp.sum(k * ws, axis=0, keepdims=True)


def _mlstm(proj, conv_w, conv_b, cols, rows, norm_g, batch, n_chunks, chunks_per_step, cast_weights):
    t = proj.shape[0]
    dm = norm_g.shape[1]
    dh = dm // MLSTM_HEADS
    n_groups = n_chunks // chunks_per_step
    rows_per_step = chunks_per_step * CHUNK
    halo_blocks = rows_per_step // _QK_HALO
    n_steps = batch * n_groups

    def cast_spec(w):
        assert w.shape[0] % (n_steps * BF16_SUBLANES) == 0
        return pl.BlockSpec((w.shape[0] // n_steps, w.shape[1]), lambda b, c: (b * n_groups + c, 0))

    cast_specs = [cast_spec(w) for w in cast_weights]
    outs = pl.pallas_call(
        functools.partial(_mlstm_kernel, heads=MLSTM_HEADS, dh=dh, chunks_per_step=chunks_per_step,
                          n_cast=len(cast_weights)),
        out_shape=(jax.ShapeDtypeStruct((t, dm), BF16),
                   *[jax.ShapeDtypeStruct(w.shape, BF16) for w in cast_weights]),
        grid=(batch, n_groups),
        in_specs=[
            pl.BlockSpec((rows_per_step, 2 * dm), lambda b, c: (b * n_groups + c, 0)),
            pl.BlockSpec((_QK_HALO, 2 * dm),
                         lambda b, c: (jnp.maximum((b * n_groups + c) * halo_blocks - 1, 0), 0)),
            pl.BlockSpec((rows_per_step, dm), lambda b, c: (b * n_groups + c, 2)),
            pl.BlockSpec((rows_per_step, dm), lambda b, c: (b * n_groups + c, 3)),
            pl.BlockSpec((MLSTM_CONV, 2 * dm), lambda b, c: (0, 0)),
            pl.BlockSpec((1, 2 * dm), lambda b, c: (0, 0)),
            pl.BlockSpec((None, chunks_per_step, CHUNK, LANES), lambda b, c: (b, c, 0, 0)),
            pl.BlockSpec((None, chunks_per_step, 2 * MLSTM_HEADS, LANES), lambda b, c: (b, c, 0, 0)),
            pl.BlockSpec((1, dm), lambda b, c: (0, 0)),
            *cast_specs,
        ],
        out_specs=(pl.BlockSpec((rows_per_step, dm), lambda b, c: (b * n_groups + c, 0)), *cast_specs),
        scratch_shapes=[
            pltpu.VMEM((2 * dm // LANES, _QK_HALO + rows_per_step, LANES), F32),
            pltpu.VMEM((MLSTM_HEADS, dh, dh), F32),
            pltpu.VMEM((2 * MLSTM_HEADS, dh), F32),
        ],
        compiler_params=_params(("arbitrary", "arbitrary"), 40),
        name="mlstm",
    )(proj, proj, proj, proj, conv_w, conv_b, cols, rows, norm_g, *cast_weights)
    return outs[0], outs[1:]


def _attn_kernel(*refs, heads, dh, n_pieces, n_cast):
    q_ref = refs[0]
    k_refs = refs[1:1 + n_pieces]
    v_refs = refs[1 + n_pieces:1 + 2 * n_pieces]
    tab_refs = refs[1 + 2 * n_pieces:1 + 3 * n_pieces]
    cast_in = refs[1 + 3 * n_pieces:1 + 3 * n_pieces + n_cast]
    out_ref = refs[1 + 3 * n_pieces + n_cast]
    cast_out = refs[2 + 3 * n_pieces + n_cast:]
    for src, dst in zip(cast_in, cast_out):
        dst[...] = src[...].astype(BF16)
    scale2 = (dh ** -0.5) * _LOG2_E
    for h in range(heads):
        hs = slice(h * dh, (h + 1) * dh)
        q = q_ref[:, hs]
        scores = [lax.dot_general(q, k_ref[:, hs], _NT, preferred_element_type=F32) * scale2 + tab_ref[h]
                  for k_ref, tab_ref in zip(k_refs, tab_refs)]
        m = jnp.max(functools.reduce(jnp.maximum, scores), axis=-1, keepdims=True)
        es = [jnp.exp2(s - m) for s in scores]
        denom = jnp.sum(functools.reduce(jnp.add, es), axis=-1, keepdims=True)
        acc = functools.reduce(jnp.add, [
            jnp.dot(e.astype(BF16), v_ref[:, hs], preferred_element_type=F32) for e, v_ref in zip(es, v_refs)])
        out_ref[:, hs] = (acc / denom).astype(out_ref.dtype)


def _attention(proj, tab, d_attn, batch, seq, q_rows, cast_weights):
    t = proj.shape[0]
    dh = d_attn // ATTN_HEADS
    n_q = seq // q_rows
    n_steps = batch * n_q
    pad_blocks = (LEFT_CHUNKS * CHUNK) // q_rows
    n_pieces = 1 + pad_blocks
    assert tab.shape == (n_pieces + 1, ATTN_HEADS, q_rows, q_rows)
    q_col = proj.shape[1] // d_attn - 3

    def window_spec(col_block, p):
        return pl.BlockSpec(
            (q_rows, d_attn),
            lambda b, qi: (b * n_q + jnp.maximum(qi + p - pad_blocks, 0), col_block))

    def tab_spec(p):
        return pl.BlockSpec(
            (None, ATTN_HEADS, q_rows, q_rows),
            lambda b, qi: (jnp.where(qi + p >= pad_blocks, p, n_pieces), 0, 0, 0))

    def cast_spec(w):
        assert w.shape[0] % (n_steps * BF16_SUBLANES) == 0
        return pl.BlockSpec((w.shape[0] // n_steps, w.shape[1]), lambda b, qi: (b * n_q + qi, 0))

    cast_specs = [cast_spec(w) for w in cast_weights]
    outs = pl.pallas_call(
        functools.partial(_attn_kernel, heads=ATTN_HEADS, dh=dh, n_pieces=n_pieces, n_cast=len(cast_weights)),
        out_shape=(jax.ShapeDtypeStruct((t, d_attn), BF16),
                   *[jax.ShapeDtypeStruct(w.shape, BF16) for w in cast_weights]),
        grid=(batch, n_q),
        in_specs=[pl.BlockSpec((q_rows, d_attn), lambda b, qi: (b * n_q + qi, q_col))]
        + [window_spec(q_col + 1, p) for p in range(n_pieces)]
        + [window_spec(q_col + 2, p) for p in range(n_pieces)]
        + [tab_spec(p) for p in range(n_pieces)]
        + cast_specs,
        out_specs=(pl.BlockSpec((q_rows, d_attn), lambda b, qi: (b * n_q + qi, 0)), *cast_specs),
        compiler_params=_params(("arbitrary", "arbitrary"), 48),
        name="chunk_attn",
    )(proj, *([proj] * (2 * n_pieces)), *([tab] * n_pieces), *cast_weights)
    return outs[0], outs[1:]


def _bias_table(rel, q_rows):
    heads = rel.shape[0]
    pad_rows = LEFT_CHUNKS * CHUNK
    width = q_rows + pad_rows
    d_lo, d_hi = pad_rows - width + 1, pad_rows + q_rows - 1
    mid = rel[:, max(d_lo, -MAX_REL) + MAX_REL:min(d_hi, MAX_REL) + MAX_REL + 1]
    left = jnp.repeat(rel[:, :1], max(0, -MAX_REL - d_lo), axis=1)
    right = jnp.repeat(rel[:, -1:], max(0, d_hi - MAX_REL), axis=1)
    by_dist = jnp.concatenate([left, mid, right], axis=1)[:, ::-1]
    length = by_dist.shape[1]
    ring = jnp.pad(by_dist, ((0, 0), (0, 1)))
    skew = jnp.tile(ring, (1, q_rows))[:, :q_rows * length].reshape(heads, q_rows, length)
    bias = skew[:, :, q_rows - 1:q_rows - 1 + width]
    q_chunk = jnp.arange(q_rows)[:, None] // CHUNK
    k_chunk = jnp.arange(width)[None, :] // CHUNK
    in_band = (k_chunk >= q_chunk) & (k_chunk <= q_chunk + LEFT_CHUNKS)
    tab = jnp.where(in_band[None], bias * _LOG2_E, NEG_INF)
    pieces = [tab[:, :, p * q_rows:(p + 1) * q_rows] for p in range(width // q_rows)]
    pieces.append(jnp.full((heads, q_rows, q_rows), NEG_INF, F32))
    return jnp.stack(pieces, axis=0)


def _out_proj_kernel(a_ref, b_ref, wa_ref, wb_ref, x_ref, g_ref, o_ref, u_ref):
    acc = jnp.dot(a_ref[...], wa_ref[...], preferred_element_type=F32)
    acc = acc + jnp.dot(b_ref[...], wb_ref[...], preferred_element_type=F32)
    h = x_ref[...] + acc
    o_ref[...] = h
    u_ref[...] = _rmsnorm_f32(h, g_ref[...]).astype(BF16)


def _out_proj(h_a, h_b, w, x2, g_next, tm):
    t, d = x2.shape
    ka, kb = h_a.shape[1], h_b.shape[1]
    assert ka == kb
    return pl.pallas_call(
        _out_proj_kernel,
        out_shape=(jax.ShapeDtypeStruct((t, d), F32), jax.ShapeDtypeStruct((t, d), BF16)),
        grid=(t // tm,),
        in_specs=[
            pl.BlockSpec((tm, ka), lambda i: (i, 0)),
            pl.BlockSpec((tm, kb), lambda i: (i, 0)),
            pl.BlockSpec((None, ka, d), lambda i: (0, 0, 0)),
            pl.BlockSpec((None, kb, d), lambda i: (1, 0, 0)),
            pl.BlockSpec((tm, d), lambda i: (i, 0)),
            pl.BlockSpec((1, d), lambda i: (0, 0)),
        ],
        out_specs=(pl.BlockSpec((tm, d), lambda i: (i, 0)), pl.BlockSpec((tm, d), lambda i: (i, 0))),
        compiler_params=_params(("arbitrary",), 56),
        name="out_proj",
    )(h_a, h_b, w, w, x2, g_next)


def _mlp_kernel(h_ref, u_ref, w1_ref, w2_ref, gp_ref, *out_refs, final_norm):
    o_ref = out_refs[0]
    j = pl.program_id(1)

    @pl.when(j == 0)
    def _():
        o_ref[...] = h_ref[...]

    a = jnp.dot(u_ref[...], w1_ref[...], preferred_element_type=F32)
    a = jnp.square(jnp.maximum(a, 0.0)).astype(BF16)
    o_ref[...] += jnp.dot(a, w2_ref[...], preferred_element_type=F32)

    @pl.when(j == pl.num_programs(1) - 1)
    def _():
        normed = _rmsnorm_f32(o_ref[...], gp_ref[...])
        if final_norm:
            o_ref[...] = normed
        else:
            out_refs[1][...] = normed.astype(BF16)


def _mlp(h2, u, w1, w2, layer, g_post, final_norm, tm, tf):
    t, d = h2.shape
    f = w1.shape[2]
    row_tile = pl.BlockSpec((tm, d), lambda i, j: (i, 0))
    out_shape = [jax.ShapeDtypeStruct((t, d), F32)]
    if not final_norm:
        out_shape.append(jax.ShapeDtypeStruct((t, d), BF16))
    return pl.pallas_call(
        functools.partial(_mlp_kernel, final_norm=final_norm),
        out_shape=tuple(out_shape),
        grid=(t // tm, f // tf),
        in_specs=[
            row_tile,
            row_tile,
            pl.BlockSpec((None, d, tf), lambda i, j: (layer, 0, j)),
            pl.BlockSpec((None, tf, d), lambda i, j: (layer, j, 0)),
            pl.BlockSpec((1, d), lambda i, j: (0, 0)),
        ],
        out_specs=tuple([row_tile] * len(out_shape)),
        compiler_params=_params(("arbitrary", "arbitrary"), 56),
        name="mlp_final" if final_norm else "mlp",
    )(h2, u, w1, w2, g_post)


def _glu_kernel(u_ref, wa_ref, wg_ref, ba_ref, bg_ref, o_ref):
    u = u_ref[...]
    a = jnp.dot(u, wa_ref[...], preferred_element_type=F32) + ba_ref[...]
    gate = jnp.dot(u, wg_ref[...], preferred_element_type=F32) + bg_ref[...]
    o_ref[...] = (a * _sigmoid(gate)).astype(o_ref.dtype)


def _glu(u, pw1, pw1_b, tm, tn):
    t, d = u.shape
    n = pw1.shape[2] // 2
    nb = n // tn
    return pl.pallas_call(
        _glu_kernel,
        out_shape=jax.ShapeDtypeStruct((t, n), BF16),
        grid=(t // tm, nb),
        in_specs=[
            pl.BlockSpec((tm, d), lambda i, j: (i, 0)),
            pl.BlockSpec((None, d, tn), lambda i, j: (0, 0, j)),
            pl.BlockSpec((None, d, tn), lambda i, j: (0, 0, j + nb)),
            pl.BlockSpec((1, tn), lambda i, j: (0, j)),
            pl.BlockSpec((1, tn), lambda i, j: (0, j + nb)),
        ],
        out_specs=pl.BlockSpec((tm, tn), lambda i, j: (i, j)),
        compiler_params=_params(("arbitrary", "arbitrary"), 56),
        name="conv_glu",
    )(u, pw1, pw1, pw1_b, pw1_b)


_DW_HALO = 2 * BF16_SUBLANES
_DW_ROWS = 64


def _conv_tail_kernel(z_ref, halo_ref, dw_ref, dwb_ref, lng_ref, lnb_ref, w2_ref, b2_ref, h_ref, g_ref,
                      o_ref, u_ref, zs_ref, y_ref, *, tm, tiles_per_seq):
    i = pl.program_id(0)
    n_slabs = z_ref.shape[1] // LANES
    seq_start = (i % tiles_per_seq) == 0
    for s in range(n_slabs):
        lanes = slice(s * LANES, (s + 1) * LANES)
        zs_ref[s, 0:_DW_HALO, :] = jnp.where(seq_start, 0.0, halo_ref[:, lanes].astype(F32))
        zs_ref[s, _DW_HALO:, :] = z_ref[:, lanes].astype(F32)
    first_tap = _DW_HALO - (CONV_WIDTH - 1)

    def slab_body(s, carry):
        lanes = pl.ds(pl.multiple_of(s * LANES, LANES), LANES)
        for rb in range(tm // _DW_ROWS):
            r0 = rb * _DW_ROWS
            acc = jnp.broadcast_to(dwb_ref[:, lanes], (_DW_ROWS, LANES))
            for k in range(CONV_WIDTH):
                acc = acc + dw_ref[k:k + 1, lanes] * zs_ref[s, r0 + first_tap + k:r0 + first_tap + k + _DW_ROWS, :]
            y_ref[r0:r0 + _DW_ROWS, lanes] = acc
        return carry

    lax.fori_loop(0, n_slabs, slab_body, 0)
    y = y_ref[...]
    mu = jnp.mean(y, axis=-1, keepdims=True)
    yc = y - mu
    var = jnp.mean(yc * yc, axis=-1, keepdims=True)
    yn = yc * lax.rsqrt(var + EPS) * lng_ref[...] + lnb_ref[...]
    a = (yn * _sigmoid(yn)).astype(BF16)
    h = h_ref[...] + jnp.dot(a, w2_ref[...], preferred_element_type=F32) + b2_ref[...]
    o_ref[...] = h
    u_ref[...] = _rmsnorm_f32(h, g_ref[...]).astype(BF16)


def _conv_tail(z, dw_w, dw_b, ln_g, ln_b, pw2, pw2_b, h2, g_next, seq, tm):
    t, d = h2.shape
    halo_blocks = tm // _DW_HALO
    row_tile = pl.BlockSpec((tm, d), lambda i: (i, 0))
    vec = pl.BlockSpec((1, d), lambda i: (0, 0))
    return pl.pallas_call(
        functools.partial(_conv_tail_kernel, tm=tm, tiles_per_seq=seq // tm),
        out_shape=(jax.ShapeDtypeStruct((t, d), F32), jax.ShapeDtypeStruct((t, d), BF16)),
        grid=(t // tm,),
        in_specs=[
            row_tile,
            pl.BlockSpec((_DW_HALO, d), lambda i: (jnp.maximum(i * halo_blocks - 1, 0), 0)),
            pl.BlockSpec(dw_w.shape, lambda i: (0, 0)),
            vec, vec, vec,
            pl.BlockSpec((None, d, d), lambda i: (0, 0, 0), pipeline_mode=pl.Buffered(1)),
            vec,
            row_tile,
            vec,
        ],
        out_specs=(row_tile, row_tile),
        scratch_shapes=[pltpu.VMEM((d // LANES, _DW_HALO + tm, LANES), F32), pltpu.VMEM((tm, d), F32)],
        compiler_params=_params(("arbitrary",), 56),
        name="conv_tail",
    )(z, z, dw_w, dw_b, ln_g, ln_b, pw2, pw2_b, h2, g_next)


def _tiles(seq):
    def rows(want):
        return min(want, seq)

    return dict(
        repack_rows=512,
        in_proj_tm=rows(512),
        mlstm_chunks_per_step=min(4, seq // CHUNK),
        attn_q_rows=rows(256),
        out_proj_tm=rows(512),
        mlp_tm=rows(512), mlp_tf=1024,
        glu_tm=rows(1024), glu_tn=1024,
        conv_tm=rows(512),
    )


def _row(v):
    return v.reshape(1, -1).astype(F32)


def kernel(x, mixer_norm_g, mix_w_in, qk_conv_w, qk_conv_b, igate_b, fgate_b, mlstm_norm_g, rel_bias,
           mix_w_out, conv_pw1_w, conv_pw1_b, conv_dw_w, conv_dw_b, conv_ln_g, conv_ln_b, conv_pw2_w,
           conv_pw2_b, mlp_norm_g, mlp_w1, mlp_w2, final_norm_g):
    batch, seq, d = x.shape
    tokens = batch * seq
    n_chunks = seq // CHUNK
    dm = d // 2
    da = d - dm
    heads = MLSTM_HEADS
    tl = _tiles(seq)
    x2 = x.reshape(tokens, d)

    w_in_t = jnp.swapaxes(mix_w_in, 1, 2)
    gate_lo = 4 * dm
    gate_hi = gate_lo + 2 * heads
    w_main_t = _drop_rows_bf16(w_in_t, gate_lo, gate_hi, tl["repack_rows"])
    w_gate_t = jnp.pad(w_in_t[0, gate_lo:gate_hi], ((0, LANES - 2 * heads), (0, 0))).astype(BF16)
    proj, gates = _in_proj(x2, _row(mixer_norm_g[0]), w_main_t, w_gate_t, tl["in_proj_tm"],
                           w_main_t.shape[0] // 2)

    n_bh = batch * heads
    g8 = gates[:, :2 * heads].reshape(batch, n_chunks, CHUNK, 2, heads)
    g8 = jnp.transpose(g8, (3, 1, 0, 4, 2)).reshape(2, n_chunks * n_bh, CHUNK)
    g8 = jnp.pad(g8, ((0, 0), (0, 0), (0, LANES - CHUNK)))
    bias_rows = lambda bvec: jnp.broadcast_to(
        jnp.tile(bvec.astype(F32), batch * n_chunks)[:, None], (n_chunks * n_bh, LANES))
    r, mt, wi, et, ws, dc = _gate_prep(g8[0], g8[1], bias_rows(igate_b[0]), bias_rows(fgate_b[0]),
                                       n_chunks, n_bh)
    per_frame = jnp.stack([mt, wi, et, ws], axis=0)[:, :, :CHUNK]
    per_frame = per_frame.reshape(4, n_chunks, batch, heads, CHUNK)
    cols = jnp.transpose(per_frame, (2, 1, 4, 0, 3)).reshape(batch, n_chunks, CHUNK, 4 * heads)
    cols = jnp.pad(cols, ((0, 0), (0, 0), (0, 0), (0, LANES - 4 * heads)))
    rows = jnp.concatenate([r.reshape(n_chunks, batch, heads, LANES),
                            dc.reshape(n_chunks, batch, heads, LANES)], axis=2)
    rows = jnp.transpose(rows, (1, 0, 2, 3))

    later_weights = [mlp_w1, mlp_w2, mix_w_out, conv_pw1_w, conv_pw2_w]
    h_a, cast = _mlstm(proj, qk_conv_w[0].astype(F32), _row(qk_conv_b[0]), cols, rows,
                       _row(mlstm_norm_g[0]), batch, n_chunks, tl["mlstm_chunks_per_step"],
                       [w.reshape(-1, w.shape[2]) for w in later_weights])
    mlp_w1_bf, mlp_w2_bf, w_out_bf, pw1_bf, pw2_bf = [c.reshape(w.shape) for c, w in zip(cast, later_weights)]

    tab = _bias_table(rel_bias[0].astype(F32), tl["attn_q_rows"])
    h_b, _ = _attention(proj, tab, da, batch, seq, tl["attn_q_rows"], [])

    h, u = _out_proj(h_a, h_b, w_out_bf.reshape(2, dm, d), x2, _row(mlp_norm_g[0]), tl["out_proj_tm"])

    h, u = _mlp(h, u, mlp_w1_bf, mlp_w2_bf, 0, _row(mixer_norm_g[1]), False, tl["mlp_tm"], tl["mlp_tf"])

    z = _glu(u, pw1_bf, _row(conv_pw1_b[0]), tl["glu_tm"], tl["glu_tn"])
    h, u = _conv_tail(z, conv_dw_w[0].astype(F32), _row(conv_dw_b[0]), _row(conv_ln_g[0]),
                      _row(conv_ln_b[0]), pw2_bf, _row(conv_pw2_b[0]), h,
                      _row(mlp_norm_g[1]), seq, tl["conv_tm"])

    (out,) = _mlp(h, u, mlp_w1_bf, mlp_w2_bf, 1, _row(final_norm_g), True, tl["mlp_tm"], tl["mlp_tf"])
    return out.reshape(batch, seq, d)
```

```python
import functools
import math

import jax
import jax.numpy as jnp
from jax import lax
from jax.experimental import pallas as pl
from jax.experimental.pallas import tpu as pltpu

CHUNK = 64
MLSTM_HEADS = 4
MLSTM_CONV = 4
ATTN_HEADS = 8
LEFT_CHUNKS = 8
MAX_REL = 256
CONV_WIDTH = 31
EPS = 1e-6

V7X_VMEM_BYTES = 64 * 1024 * 1024
LANES = 128
BF16_SUBLANES = 16

F32 = jnp.float32
BF16 = jnp.bfloat16
NEG_INF = float("-inf")
_LOG2_E = math.log2(math.e)

_NT = (((1,), (1,)), ((), ()))
_TN = (((0,), (0,)), ((), ()))


def _params(semantics, vmem_mib):
    assert vmem_mib * 1024 * 1024 <= V7X_VMEM_BYTES
    return pltpu.CompilerParams(dimension_semantics=semantics,
                                vmem_limit_bytes=vmem_mib * 1024 * 1024)


def _rmsnorm_f32(x, g):
    return x * lax.rsqrt(jnp.mean(x * x, axis=-1, keepdims=True) + EPS) * g


def _sigmoid(x):
    return 0.5 * jnp.tanh(0.5 * x) + 0.5


def _drop_rows_kernel(w_ref, o_ref):
    o_ref[...] = w_ref[0].astype(BF16)


def _drop_rows_bf16(w_t, lo, hi, rows):
    n_in, d = w_t.shape[1:]
    n = n_in - (hi - lo)
    assert lo % rows == 0 and n % rows == 0 and (hi - lo) % 8 == 0
    return pl.pallas_call(
        _drop_rows_kernel,
        out_shape=jax.ShapeDtypeStruct((n, d), BF16),
        grid=(n // rows,),
        in_specs=[pl.BlockSpec((pl.Element(1), pl.Element(rows), pl.Element(d)),
                               lambda i: (0, pl.multiple_of(jnp.where(i * rows < lo, i * rows, i * rows + (hi - lo)), 8), 0))],
        out_specs=pl.BlockSpec((rows, d), lambda i: (i, 0)),
        compiler_params=_params(("arbitrary",), 32),
        name="w_in_repack",
    )(w_t)


def _in_proj_kernel(x_ref, g_ref, w_ref, wg_ref, o_ref, og_ref, u_ref):
    @pl.when(pl.program_id(1) == 0)
    def _():
        u = _rmsnorm_f32(x_ref[...], g_ref[...]).astype(BF16)
        u_ref[...] = u
        og_ref[...] = lax.dot_general(u, wg_ref[...], _NT, preferred_element_type=F32)

    o_ref[...] = lax.dot_general(u_ref[...], w_ref[...], _NT, preferred_element_type=F32).astype(o_ref.dtype)


def _in_proj(x2, g, w_main_t, w_gate_t, tm, tn):
    t, d = x2.shape
    n = w_main_t.shape[0]
    return pl.pallas_call(
        _in_proj_kernel,
        out_shape=(jax.ShapeDtypeStruct((t, n), BF16), jax.ShapeDtypeStruct((t, LANES), F32)),
        grid=(t // tm, n // tn),
        in_specs=[
            pl.BlockSpec((tm, d), lambda i, j: (i, 0)),
            pl.BlockSpec((1, d), lambda i, j: (0, 0)),
            pl.BlockSpec((tn, d), lambda i, j: (j, 0)),
            pl.BlockSpec((LANES, d), lambda i, j: (0, 0)),
        ],
        out_specs=(pl.BlockSpec((tm, tn), lambda i, j: (i, j)),
                   pl.BlockSpec((tm, LANES), lambda i, j: (i, 0))),
        scratch_shapes=[pltpu.VMEM((tm, d), BF16)],
        compiler_params=_params(("arbitrary", "arbitrary"), 56),
        name="in_proj",
    )(x2, g, w_main_t, w_gate_t)


def _lane_prefix(x, op, ident, lane):
    shift = 1
    while shift < CHUNK:
        y = pltpu.roll(x, shift, axis=1)
        x = op(x, jnp.where(lane >= shift, y, ident))
        shift *= 2
    return x


def _gate_prep_kernel(gi_ref, gf_ref, ib_ref, fb_ref,
                      r_ref, mt_ref, wi_ref, et_ref, ws_ref, dc_ref,
                      g_sc, c_sc, m_sc, *, n_chunks, n_bh):
    shape = gi_ref.shape
    lane = lax.broadcasted_iota(jnp.int32, shape, 1)
    valid = lane < CHUNK
    i_pre = gi_ref[...] + ib_ref[...]
    f_pre = gf_ref[...] + fb_ref[...]
    lf = jnp.minimum(f_pre, 0.0) - jnp.log1p(jnp.exp(-jnp.abs(f_pre)))
    lf = jnp.where(valid, lf, 0.0)
    b = _lane_prefix(lf, jnp.add, 0.0, lane)
    r = jnp.where(valid, i_pre - b, NEG_INF)
    cm = _lane_prefix(r, jnp.maximum, NEG_INF, lane)
    g_tot = jnp.sum(jnp.where(lane == CHUNK - 1, b, 0.0), axis=1, keepdims=True)
    c_last = jnp.max(r, axis=1, keepdims=True)
    g_sc[...] = jnp.broadcast_to(g_tot, shape)
    c_sc[...] = jnp.broadcast_to(c_last, shape)

    def body(c, m):
        rows = pl.ds(pl.multiple_of(c * n_bh, n_bh), n_bh)
        m_sc[rows, :] = m
        return g_sc[rows, :] + jnp.maximum(m, c_sc[rows, :])

    lax.fori_loop(0, n_chunks, body, jnp.zeros((n_bh, shape[1]), F32))
    m = m_sc[...]
    big_m = jnp.maximum(m, cm)
    m_last = jnp.maximum(m, c_sc[...])
    r_ref[...] = jnp.where(valid, r, 0.0)
    mt_ref[...] = big_m
    wi_ref[...] = jnp.exp(m - big_m)
    et_ref[...] = jnp.exp(-b - big_m)
    ws_ref[...] = jnp.exp(r - m_last)
    dc_ref[...] = jnp.exp(m - m_last)


def _gate_prep(gi, gf, ib, fb, n_chunks, n_bh):
    shape = jax.ShapeDtypeStruct(gi.shape, F32)
    return pl.pallas_call(
        functools.partial(_gate_prep_kernel, n_chunks=n_chunks, n_bh=n_bh),
        out_shape=(shape,) * 6,
        scratch_shapes=[pltpu.VMEM(gi.shape, F32)] * 3,
        compiler_params=_params(None, 32),
        name="gate_prep",
    )(gi, gf, ib, fb)


_QK_HALO = BF16_SUBLANES


def _mlstm_kernel(qk_ref, halo_ref, v_ref, o_ref, cw_ref, cb_ref, cols_ref, rows_ref, ng_ref, *refs,
                  heads, dh, chunks_per_step, n_cast):
    cast_in = refs[:n_cast]
    out_ref = refs[n_cast]
    cast_out = refs[n_cast + 1:2 * n_cast + 1]
    xs_ref, c_ref, n_ref = refs[2 * n_cast + 1:]
    for src, dst in zip(cast_in, cast_out):
        dst[...] = src[...].astype(BF16)
    group_id = pl.program_id(1)
    dm = heads * dh
    n_slabs = 2 * dm // LANES

    @pl.when(group_id == 0)
    def _():
        c_ref[...] = jnp.zeros(c_ref.shape, F32)
        n_ref[...] = jnp.zeros(n_ref.shape, F32)

    for s in range(n_slabs):
        lanes = slice(s * LANES, (s + 1) * LANES)
        xs_ref[s, 0:_QK_HALO, :] = jnp.where(group_id == 0, 0.0, halo_ref[:, lanes].astype(F32))
        xs_ref[s, _QK_HALO:, :] = qk_ref[:, lanes].astype(F32)

    def conv_silu(col0, row0):
        parts = []
        for s in range(col0 // LANES, (col0 + dh) // LANES):
            lanes = slice(s * LANES, (s + 1) * LANES)
            acc = cb_ref[:, lanes]
            for j in range(MLSTM_CONV):
                off = row0 + _QK_HALO - (MLSTM_CONV - 1) + j
                acc = acc + cw_ref[j:j + 1, lanes] * xs_ref[s, off:off + CHUNK, :]
            parts.append(acc * _sigmoid(acc))
        return jnp.concatenate(parts, axis=1)

    tri = (lax.broadcasted_iota(jnp.int32, (CHUNK, CHUNK), 0)
           >= lax.broadcasted_iota(jnp.int32, (CHUNK, CHUNK), 1))
    eye = (lax.broadcasted_iota(jnp.int32, (dh, dh), 0)
           == lax.broadcasted_iota(jnp.int32, (dh, dh), 1)).astype(BF16)
    for cc in range(chunks_per_step):
        row0 = cc * CHUNK
        rs = slice(row0, row0 + CHUNK)
        for h in range(heads):
            hs = slice(h * dh, (h + 1) * dh)
            q = conv_silu(h * dh, row0)
            k = conv_silu(dm + h * dh, row0) * (dh ** -0.5)
            qb = q.astype(BF16)
            kb = k.astype(BF16)
            v = v_ref[rs, hs]
            mt = cols_ref[cc, :, h:h + 1]
            wi = cols_ref[cc, :, heads + h:heads + h + 1]
            et = cols_ref[cc, :, 2 * heads + h:2 * heads + h + 1]
            ws = cols_ref[cc, :, 3 * heads + h:3 * heads + h + 1]
            r = rows_ref[cc, h:h + 1, 0:CHUNK]
            dc = rows_ref[cc, heads + h:heads + h + 1, 0:1]

            s = lax.dot_general(qb, kb, _NT, preferred_element_type=F32)
            sw = s * jnp.where(tri, jnp.exp(r - mt), 0.0)
            c_old = c_ref[h]
            inter = lax.dot_general(qb, c_old.astype(BF16), _NT, preferred_element_type=F32)
            intra = jnp.dot(sw.astype(BF16), v, preferred_element_type=F32)
            n_old = n_ref[h:h + 1, :]
            num = intra + wi * inter
            den = (jnp.sum(sw, axis=-1, keepdims=True)
                   + wi * jnp.sum(q * n_old, axis=-1, keepdims=True))
            hh = num / jnp.maximum(jnp.abs(den), et)
            hn = hh * lax.rsqrt(jnp.mean(hh * hh, axis=-1, keepdims=True) + EPS) * ng_ref[:, hs]
            out_ref[rs, hs] = (_sigmoid(o_ref[rs, hs].astype(F32)) * hn).astype(out_ref.dtype)

            vw = (v.astype(F32) * ws).astype(BF16)
            vw_t = lax.dot_general(eye, vw, _NT, preferred_element_type=F32).astype(BF16)
            upd = jnp.dot(vw_t, kb, preferred_element_type=F32)
            c_ref[h] = dc * c_old + upd
            n_ref[h:h + 1, :] = dc * n_old + jnp.sum(k * ws, axis=0, keepdims=True)


def _mlstm(proj, conv_w, conv_b, cols, rows, norm_g, batch, n_chunks, chunks_per_step, cast_weights):
    t = proj.shape[0]
    dm = norm_g.shape[1]
    dh = dm // MLSTM_HEADS
    n_groups = n_chunks // chunks_per_step
    rows_per_step = chunks_per_step * CHUNK
    halo_blocks = rows_per_step // _QK_HALO
    n_steps = batch * n_groups

    def cast_spec(w):
        assert w.shape[0] % (n_steps * BF16_SUBLANES) == 0
        return pl.BlockSpec((w.shape[0] // n_steps, w.shape[1]), lambda b, c: (b * n_groups + c, 0))

    cast_specs = [cast_spec(w) for w in cast_weights]
    outs = pl.pallas_call(
        functools.partial(_mlstm_kernel, heads=MLSTM_HEADS, dh=dh, chunks_per_step=chunks_per_step,
                          n_cast=len(cast_weights)),
        out_shape=(jax.ShapeDtypeStruct((t, dm), BF16),
                   *[jax.ShapeDtypeStruct(w.shape, BF16) for w in cast_weights]),
        grid=(batch, n_groups),
        in_specs=[
            pl.BlockSpec((rows_per_step, 2 * dm), lambda b, c: (b * n_groups + c, 0)),
            pl.BlockSpec((_QK_HALO, 2 * dm),
                         lambda b, c: (jnp.maximum((b * n_groups + c) * halo_blocks - 1, 0), 0)),
            pl.BlockSpec((rows_per_step, dm), lambda b, c: (b * n_groups + c, 2)),
            pl.BlockSpec((rows_per_step, dm), lambda b, c: (b * n_groups + c, 3)),
            pl.BlockSpec((MLSTM_CONV, 2 * dm), lambda b, c: (0, 0)),
            pl.BlockSpec((1, 2 * dm), lambda b, c: (0, 0)),
            pl.BlockSpec((None, chunks_per_step, CHUNK, LANES), lambda b, c: (b, c, 0, 0)),
            pl.BlockSpec((None, chunks_per_step, 2 * MLSTM_HEADS, LANES), lambda b, c: (b, c, 0, 0)),
            pl.BlockSpec((1, dm), lambda b, c: (0, 0)),
            *cast_specs,
        ],
        out_specs=(pl.BlockSpec((rows_per_step, dm), lambda b, c: (b * n_groups + c, 0)), *cast_specs),
        scratch_shapes=[
            pltpu.VMEM((2 * dm // LANES, _QK_HALO + rows_per_step, LANES), F32),
            pltpu.VMEM((MLSTM_HEADS, dh, dh), F32),
            pltpu.VMEM((2 * MLSTM_HEADS, dh), F32),
        ],
        compiler_params=_params(("arbitrary", "arbitrary"), 40),
        name="mlstm",
    )(proj, proj, proj, proj, conv_w, conv_b, cols, rows, norm_g, *cast_weights)
    return outs[0], outs[1:]


def _attn_kernel(*refs, heads, dh, n_pieces, n_cast):
    q_ref = refs[0]
    k_refs = refs[1:1 + n_pieces]
    v_refs = refs[1 + n_pieces:1 + 2 * n_pieces]
    tab_refs = refs[1 + 2 * n_pieces:1 + 3 * n_pieces]
    cast_in = refs[1 + 3 * n_pieces:1 + 3 * n_pieces + n_cast]
    out_ref = refs[1 + 3 * n_pieces + n_cast]
    cast_out = refs[2 + 3 * n_pieces + n_cast:]
    for src, dst in zip(cast_in, cast_out):
        dst[...] = src[...].astype(BF16)
    scale2 = (dh ** -0.5) * _LOG2_E
    for h in range(heads):
        hs = slice(h * dh, (h + 1) * dh)
        q = q_ref[:, hs]
        scores = [lax.dot_general(q, k_ref[:, hs], _NT, preferred_element_type=F32) * scale2 + tab_ref[h]
                  for k_ref, tab_ref in zip(k_refs, tab_refs)]
        m = jnp.max(functools.reduce(jnp.maximum, scores), axis=-1, keepdims=True)
        es = [jnp.exp2(s - m) for s in scores]
        denom = jnp.sum(functools.reduce(jnp.add, es), axis=-1, keepdims=True)
        acc = functools.reduce(jnp.add, [
            jnp.dot(e.astype(BF16), v_ref[:, hs], preferred_element_type=F32) for e, v_ref in zip(es, v_refs)])
        out_ref[:, hs] = (acc / denom).astype(out_ref.dtype)


def _attention(proj, tab, d_attn, batch, seq, q_rows, cast_weights):
    t = proj.shape[0]
    dh = d_attn // ATTN_HEADS
    n_q = seq // q_rows
    n_steps = batch * n_q
    pad_blocks = (LEFT_CHUNKS * CHUNK) // q_rows
    n_pieces = 1 + pad_blocks
    assert tab.shape == (n_pieces + 1, ATTN_HEADS, q_rows, q_rows)
    q_col = proj.shape[1] // d_attn - 3

    def window_spec(col_block, p):
        return pl.BlockSpec(
            (q_rows, d_attn),
            lambda b, qi: (b * n_q + jnp.maximum(qi + p - pad_blocks, 0), col_block))

    def tab_spec(p):
        return pl.BlockSpec(
            (None, ATTN_HEADS, q_rows, q_rows),
            lambda b, qi: (jnp.where(qi + p >= pad_blocks, p, n_pieces), 0, 0, 0))

    def cast_spec(w):
        assert w.shape[0] % (n_steps * BF16_SUBLANES) == 0
        return pl.BlockSpec((w.shape[0] // n_steps, w.shape[1]), lambda b, qi: (b * n_q + qi, 0))

    cast_specs = [cast_spec(w) for w in cast_weights]
    outs = pl.pallas_call(
        functools.partial(_attn_kernel, heads=ATTN_HEADS, dh=dh, n_pieces=n_pieces, n_cast=len(cast_weights)),
        out_shape=(jax.ShapeDtypeStruct((t, d_attn), BF16),
                   *[jax.ShapeDtypeStruct(w.shape, BF16) for w in cast_weights]),
        grid=(batch, n_q),
        in_specs=[pl.BlockSpec((q_rows, d_attn), lambda b, qi: (b * n_q + qi, q_col))]
        + [window_spec(q_col + 1, p) for p in range(n_pieces)]
        + [window_spec(q_col + 2, p) for p in range(n_pieces)]
        + [tab_spec(p) for p in range(n_pieces)]
        + cast_specs,
        out_specs=(pl.BlockSpec((q_rows, d_attn), lambda b, qi: (b * n_q + qi, 0)), *cast_specs),
        compiler_params=_params(("arbitrary", "arbitrary"), 48),
        name="chunk_attn",
    )(proj, *([proj] * (2 * n_pieces)), *([tab] * n_pieces), *cast_weights)
    return outs[0], outs[1:]


def _bias_table(rel, q_rows):
    heads = rel.shape[0]
    pad_rows = LEFT_CHUNKS * CHUNK
    width = q_rows + pad_rows
    d_lo, d_hi = pad_rows - width + 1, pad_rows + q_rows - 1
    mid = rel[:, max(d_lo, -MAX_REL) + MAX_REL:min(d_hi, MAX_REL) + MAX_REL + 1]
    left = jnp.repeat(rel[:, :1], max(0, -MAX_REL - d_lo), axis=1)
    right = jnp.repeat(rel[:, -1:], max(0, d_hi - MAX_REL), axis=1)
    by_dist = jnp.concatenate([left, mid, right], axis=1)[:, ::-1]
    length = by_dist.shape[1]
    ring = jnp.pad(by_dist, ((0, 0), (0, 1)))
    skew = jnp.tile(ring, (1, q_rows))[:, :q_rows * length].reshape(heads, q_rows, length)
    bias = skew[:, :, q_rows - 1:q_rows - 1 + width]
    q_chunk = jnp.arange(q_rows)[:, None] // CHUNK
    k_chunk = jnp.arange(width)[None, :] // CHUNK
    in_band = (k_chunk >= q_chunk) & (k_chunk <= q_chunk + LEFT_CHUNKS)
    tab = jnp.where(in_band[None], bias * _LOG2_E, NEG_INF)
    pieces = [tab[:, :, p * q_rows:(p + 1) * q_rows] for p in range(width // q_rows)]
    pieces.append(jnp.full((heads, q_rows, q_rows), NEG_INF, F32))
    return jnp.stack(pieces, axis=0)


def _out_proj_kernel(a_ref, b_ref, wa_ref, wb_ref, x_ref, g_ref, o_ref, u_ref):
    acc = jnp.dot(a_ref[...], wa_ref[...], preferred_element_type=F32)
    acc = acc + jnp.dot(b_ref[...], wb_ref[...], preferred_element_type=F32)
    h = x_ref[...] + acc
    o_ref[...] = h
    u_ref[...] = _rmsnorm_f32(h, g_ref[...]).astype(BF16)


def _out_proj(h_a, h_b, w, x2, g_next, tm):
    t, d = x2.shape
    ka, kb = h_a.shape[1], h_b.shape[1]
    assert ka == kb
    return pl.pallas_call(
        _out_proj_kernel,
        out_shape=(jax.ShapeDtypeStruct((t, d), F32), jax.ShapeDtypeStruct((t, d), BF16)),
        grid=(t // tm,),
        in_specs=[
            pl.BlockSpec((tm, ka), lambda i: (i, 0)),
            pl.BlockSpec((tm, kb), lambda i: (i, 0)),
            pl.BlockSpec((None, ka, d), lambda i: (0, 0, 0)),
            pl.BlockSpec((None, kb, d), lambda i: (1, 0, 0)),
            pl.BlockSpec((tm, d), lambda i: (i, 0)),
            pl.BlockSpec((1, d), lambda i: (0, 0)),
        ],
        out_specs=(pl.BlockSpec((tm, d), lambda i: (i, 0)), pl.BlockSpec((tm, d), lambda i: (i, 0))),
        compiler_params=_params(("arbitrary",), 56),
        name="out_proj",
    )(h_a, h_b, w, w, x2, g_next)


def _mlp_kernel(h_ref, u_ref, w1_ref, w2_ref, gp_ref, *out_refs, final_norm):
    o_ref = out_refs[0]
    j = pl.program_id(1)

    @pl.when(j == 0)
    def _():
        o_ref[...] = h_ref[...]

    a = jnp.dot(u_ref[...], w1_ref[...], preferred_element_type=F32)
    a = jnp.square(jnp.maximum(a, 0.0)).astype(BF16)
    o_ref[...] += jnp.dot(a, w2_ref[...], preferred_element_type=F32)

    @pl.when(j == pl.num_programs(1) - 1)
    def _():
        normed = _rmsnorm_f32(o_ref[...], gp_ref[...])
        if final_norm:
            o_ref[...] = normed
        else:
            out_refs[1][...] = normed.astype(BF16)


def _mlp(h2, u, w1, w2, layer, g_post, final_norm, tm, tf):
    t, d = h2.shape
    f = w1.shape[2]
    row_tile = pl.BlockSpec((tm, d), lambda i, j: (i, 0))
    out_shape = [jax.ShapeDtypeStruct((t, d), F32)]
    if not final_norm:
        out_shape.append(jax.ShapeDtypeStruct((t, d), BF16))
    return pl.pallas_call(
        functools.partial(_mlp_kernel, final_norm=final_norm),
        out_shape=tuple(out_shape),
        grid=(t // tm, f // tf),
        in_specs=[
            row_tile,
            row_tile,
            pl.BlockSpec((None, d, tf), lambda i, j: (layer, 0, j)),
            pl.BlockSpec((None, tf, d), lambda i, j: (layer, j, 0)),
            pl.BlockSpec((1, d), lambda i, j: (0, 0)),
        ],
        out_specs=tuple([row_tile] * len(out_shape)),
        compiler_params=_params(("arbitrary", "arbitrary"), 62),
        name="mlp_final" if final_norm else "mlp",
    )(h2, u, w1, w2, g_post)


def _glu_kernel(u_ref, wa_ref, wg_ref, ba_ref, bg_ref, o_ref):
    u = u_ref[...]
    a = jnp.dot(u, wa_ref[...], preferred_element_type=F32) + ba_ref[...]
    gate = jnp.dot(u, wg_ref[...], preferred_element_type=F32) + bg_ref[...]
    o_ref[...] = (a * _sigmoid(gate)).astype(o_ref.dtype)


def _glu(u, pw1, pw1_b, tm, tn):
    t, d = u.shape
    n = pw1.shape[2] // 2
    nb = n // tn
    return pl.pallas_call(
        _glu_kernel,
        out_shape=jax.ShapeDtypeStruct((t, n), BF16),
        grid=(t // tm, nb),
        in_specs=[
            pl.BlockSpec((tm, d), lambda i, j: (i, 0)),
            pl.BlockSpec((None, d, tn), lambda i, j: (0, 0, j)),
            pl.BlockSpec((None, d, tn), lambda i, j: (0, 0, j + nb)),
            pl.BlockSpec((1, tn), lambda i, j: (0, j)),
            pl.BlockSpec((1, tn), lambda i, j: (0, j + nb)),
        ],
        out_specs=pl.BlockSpec((tm, tn), lambda i, j: (i, j)),
        compiler_params=_params(("arbitrary", "arbitrary"), 56),
        name="conv_glu",
    )(u, pw1, pw1, pw1_b, pw1_b)


_DW_HALO = 2 * BF16_SUBLANES
_DW_ROWS = 64


def _conv_tail_kernel(z_ref, halo_ref, dw_ref, dwb_ref, lng_ref, lnb_ref, w2_ref, b2_ref, h_ref, g_ref,
                      o_ref, u_ref, zs_ref, y_ref, *, tm, tiles_per_seq):
    i = pl.program_id(0)
    n_slabs = z_ref.shape[1] // LANES
    seq_start = (i % tiles_per_seq) == 0
    for s in range(n_slabs):
        lanes = slice(s * LANES, (s + 1) * LANES)
        zs_ref[s, 0:_DW_HALO, :] = jnp.where(seq_start, 0.0, halo_ref[:, lanes].astype(F32))
        zs_ref[s, _DW_HALO:, :] = z_ref[:, lanes].astype(F32)
    first_tap = _DW_HALO - (CONV_WIDTH - 1)

    def slab_body(s, carry):
        lanes = pl.ds(pl.multiple_of(s * LANES, LANES), LANES)
        for rb in range(tm // _DW_ROWS):
            r0 = rb * _DW_ROWS
            acc = jnp.broadcast_to(dwb_ref[:, lanes], (_DW_ROWS, LANES))
            for k in range(CONV_WIDTH):
                acc = acc + dw_ref[k:k + 1, lanes] * zs_ref[s, r0 + first_tap + k:r0 + first_tap + k + _DW_ROWS, :]
            y_ref[r0:r0 + _DW_ROWS, lanes] = acc
        return carry

    lax.fori_loop(0, n_slabs, slab_body, 0)
    y = y_ref[...]
    mu = jnp.mean(y, axis=-1, keepdims=True)
    yc = y - mu
    var = jnp.mean(yc * yc, axis=-1, keepdims=True)
    yn = yc * lax.rsqrt(var + EPS) * lng_ref[...] + lnb_ref[...]
    a = (yn * _sigmoid(yn)).astype(BF16)
    h = h_ref[...] + jnp.dot(a, w2_ref[...], preferred_element_type=F32) + b2_ref[...]
    o_ref[...] = h
    u_ref[...] = _rmsnorm_f32(h, g_ref[...]).astype(BF16)


def _conv_tail(z, dw_w, dw_b, ln_g, ln_b, pw2, pw2_b, h2, g_next, seq, tm):
    t, d = h2.shape
    halo_blocks = tm // _DW_HALO
    row_tile = pl.BlockSpec((tm, d), lambda i: (i, 0))
    vec = pl.BlockSpec((1, d), lambda i: (0, 0))
    return pl.pallas_call(
        functools.partial(_conv_tail_kernel, tm=tm, tiles_per_seq=seq // tm),
        out_shape=(jax.ShapeDtypeStruct((t, d), F32), jax.ShapeDtypeStruct((t, d), BF16)),
        grid=(t // tm,),
        in_specs=[
            row_tile,
            pl.BlockSpec((_DW_HALO, d), lambda i: (jnp.maximum(i * halo_blocks - 1, 0), 0)),
            pl.BlockSpec(dw_w.shape, lambda i: (0, 0)),
            vec, vec, vec,
            pl.BlockSpec((None, d, d), lambda i: (0, 0, 0), pipeline_mode=pl.Buffered(1)),
            vec,
            row_tile,
            vec,
        ],
        out_specs=(row_tile, row_tile),
        scratch_shapes=[pltpu.VMEM((d // LANES, _DW_HALO + tm, LANES), F32), pltpu.VMEM((tm, d), F32)],
        compiler_params=_params(("arbitrary",), 56),
        name="conv_tail",
    )(z, z, dw_w, dw_b, ln_g, ln_b, pw2, pw2_b, h2, g_next)


def _tiles(seq):
    def rows(want):
        return min(want, seq)

    return dict(
        repack_rows=512,
        in_proj_tm=rows(512),
        mlstm_chunks_per_step=min(4, seq // CHUNK),
        attn_q_rows=rows(256),
        out_proj_tm=rows(512),
        mlp_tm=rows(512), mlp_tf=2048,
        glu_tm=rows(1024), glu_tn=1024,
        conv_tm=rows(512),
    )


def _row(v):
    return v.reshape(1, -1).astype(F32)


def kernel(x, mixer_norm_g, mix_w_in, qk_conv_w, qk_conv_b, igate_b, fgate_b, mlstm_norm_g, rel_bias,
           mix_w_out, conv_pw1_w, conv_pw1_b, conv_dw_w, conv_dw_b, conv_ln_g, conv_ln_b, conv_pw2_w,
           conv_pw2_b, mlp_norm_g, mlp_w1, mlp_w2, final_norm_g):
    batch, seq, d = x.shape
    tokens = batch * seq
    n_chunks = seq // CHUNK
    dm = d // 2
    da = d - dm
    heads = MLSTM_HEADS
    tl = _tiles(seq)
    x2 = x.reshape(tokens, d)

    w_in_t = jnp.swapaxes(mix_w_in, 1, 2)
    gate_lo = 4 * dm
    gate_hi = gate_lo + 2 * heads
    w_main_t = _drop_rows_bf16(w_in_t, gate_lo, gate_hi, tl["repack_rows"])
    w_gate_t = jnp.pad(w_in_t[0, gate_lo:gate_hi], ((0, LANES - 2 * heads), (0, 0))).astype(BF16)
    proj, gates = _in_proj(x2, _row(mixer_norm_g[0]), w_main_t, w_gate_t, tl["in_proj_tm"],
                           w_main_t.shape[0] // 2)

    n_bh = batch * heads
    g8 = gates[:, :2 * heads].reshape(batch, n_chunks, CHUNK, 2, heads)
    g8 = jnp.transpose(g8, (3, 1, 0, 4, 2)).reshape(2, n_chunks * n_bh, CHUNK)
    g8 = jnp.pad(g8, ((0, 0), (0, 0), (0, LANES - CHUNK)))
    bias_rows = lambda bvec: jnp.broadcast_to(
        jnp.tile(bvec.astype(F32), batch * n_chunks)[:, None], (n_chunks * n_bh, LANES))
    r, mt, wi, et, ws, dc = _gate_prep(g8[0], g8[1], bias_rows(igate_b[0]), bias_rows(fgate_b[0]),
                                       n_chunks, n_bh)
    per_frame = jnp.stack([mt, wi, et, ws], axis=0)[:, :, :CHUNK]
    per_frame = per_frame.reshape(4, n_chunks, batch, heads, CHUNK)
    cols = jnp.transpose(per_frame, (2, 1, 4, 0, 3)).reshape(batch, n_chunks, CHUNK, 4 * heads)
    cols = jnp.pad(cols, ((0, 0), (0, 0), (0, 0), (0, LANES - 4 * heads)))
    rows = jnp.concatenate([r.reshape(n_chunks, batch, heads, LANES),
                            dc.reshape(n_chunks, batch, heads, LANES)], axis=2)
    rows = jnp.transpose(rows, (1, 0, 2, 3))

    later_weights = [mlp_w1, mlp_w2, mix_w_out, conv_pw1_w, conv_pw2_w]
    h_a, cast = _mlstm(proj, qk_conv_w[0].astype(F32), _row(qk_conv_b[0]), cols, rows,
                       _row(mlstm_norm_g[0]), batch, n_chunks, tl["mlstm_chunks_per_step"],
                       [w.reshape(-1, w.shape[2]) for w in later_weights])
    mlp_w1_bf, mlp_w2_bf, w_out_bf, pw1_bf, pw2_bf = [c.reshape(w.shape) for c, w in zip(cast, later_weights)]

    tab = _bias_table(rel_bias[0].astype(F32), tl["attn_q_rows"])
    h_b, _ = _attention(proj, tab, da, batch, seq, tl["attn_q_rows"], [])

    h, u = _out_proj(h_a, h_b, w_out_bf.reshape(2, dm, d), x2, _row(mlp_norm_g[0]), tl["out_proj_tm"])

    h, u = _mlp(h, u, mlp_w1_bf, mlp_w2_bf, 0, _row(mixer_norm_g[1]), False, tl["mlp_tm"], tl["mlp_tf"])

    z = _glu(u, pw1_bf, _row(conv_pw1_b[0]), tl["glu_tm"], tl["glu_tn"])
    h, u = _conv_tail(z, conv_dw_w[0].astype(F32), _row(conv_dw_b[0]), _row(conv_ln_g[0]),
                      _row(conv_ln_b[0]), pw2_bf, _row(conv_pw2_b[0]), h,
                      _row(mlp_norm_g[1]), seq, tl["conv_tm"])

    (out,) = _mlp(h, u, mlp_w1_bf, mlp_w2_bf, 1, _row(final_norm_g), True, tl["mlp_tm"], tl["mlp_tf"])
    return out.reshape(batch, seq, d)
```

```python
import functools
import math

import jax
import jax.numpy as jnp
from jax import lax
from jax.experimental import pallas as pl
from jax.experimental.pallas import tpu as pltpu

CHUNK = 64
MLSTM_HEADS = 4
MLSTM_CONV = 4
ATTN_HEADS = 8
LEFT_CHUNKS = 8
MAX_REL = 256
CONV_WIDTH = 31
EPS = 1e-6

V7X_VMEM_BYTES = 64 * 1024 * 1024
LANES = 128
BF16_SUBLANES = 16

F32 = jnp.float32
BF16 = jnp.bfloat16
NEG_INF = float("-inf")
_LOG2_E = math.log2(math.e)

_NT = (((1,), (1,)), ((), ()))
_TN = (((0,), (0,)), ((), ()))


def _params(semantics, vmem_mib):
    assert vmem_mib * 1024 * 1024 <= V7X_VMEM_BYTES
    return pltpu.CompilerParams(dimension_semantics=semantics,
                                vmem_limit_bytes=vmem_mib * 1024 * 1024)


def _rmsnorm_f32(x, g):
    return x * lax.rsqrt(jnp.mean(x * x, axis=-1, keepdims=True) + EPS) * g


def _sigmoid(x):
    return 0.5 * jnp.tanh(0.5 * x) + 0.5


def _drop_rows_kernel(w_ref, o_ref):
    o_ref[...] = w_ref[0].astype(BF16)


def _drop_rows_bf16(w_t, lo, hi, rows):
    n_in, d = w_t.shape[1:]
    n = n_in - (hi - lo)
    assert lo % rows == 0 and n % rows == 0 and (hi - lo) % 8 == 0
    return pl.pallas_call(
        _drop_rows_kernel,
        out_shape=jax.ShapeDtypeStruct((n, d), BF16),
        grid=(n // rows,),
        in_specs=[pl.BlockSpec((pl.Element(1), pl.Element(rows), pl.Element(d)),
                               lambda i: (0, pl.multiple_of(jnp.where(i * rows < lo, i * rows, i * rows + (hi - lo)), 8), 0))],
        out_specs=pl.BlockSpec((rows, d), lambda i: (i, 0)),
        compiler_params=_params(("arbitrary",), 32),
        name="w_in_repack",
    )(w_t)


def _in_proj_kernel(x_ref, g_ref, w_ref, wg_ref, o_ref, og_ref):
    u = _rmsnorm_f32(x_ref[...], g_ref[...]).astype(BF16)
    og_ref[...] = lax.dot_general(u, wg_ref[...], _NT, preferred_element_type=F32)
    o_ref[...] = lax.dot_general(u, w_ref[...], _NT, preferred_element_type=F32).astype(o_ref.dtype)


def _in_proj(x2, g, w_main_t, w_gate_t, tm):
    t, d = x2.shape
    n = w_main_t.shape[0]
    resident = pl.Buffered(1)
    return pl.pallas_call(
        _in_proj_kernel,
        out_shape=(jax.ShapeDtypeStruct((t, n), BF16), jax.ShapeDtypeStruct((t, LANES), F32)),
        grid=(t // tm,),
        in_specs=[
            pl.BlockSpec((tm, d), lambda i: (i, 0)),
            pl.BlockSpec((1, d), lambda i: (0, 0)),
            pl.BlockSpec((n, d), lambda i: (0, 0), pipeline_mode=resident),
            pl.BlockSpec((LANES, d), lambda i: (0, 0), pipeline_mode=resident),
        ],
        out_specs=(pl.BlockSpec((tm, n), lambda i: (i, 0)),
                   pl.BlockSpec((tm, LANES), lambda i: (i, 0))),
        compiler_params=_params(("arbitrary",), 62),
        name="in_proj",
    )(x2, g, w_main_t, w_gate_t)


def _lane_prefix(x, op, ident, lane):
    shift = 1
    while shift < CHUNK:
        y = pltpu.roll(x, shift, axis=1)
        x = op(x, jnp.where(lane >= shift, y, ident))
        shift *= 2
    return x


def _gate_prep_kernel(gi_ref, gf_ref, ib_ref, fb_ref,
                      r_ref, mt_ref, wi_ref, et_ref, ws_ref, dc_ref,
                      g_sc, c_sc, m_sc, *, n_chunks, n_bh):
    shape = gi_ref.shape
    lane = lax.broadcasted_iota(jnp.int32, shape, 1)
    valid = lane < CHUNK
    i_pre = gi_ref[...] + ib_ref[...]
    f_pre = gf_ref[...] + fb_ref[...]
    lf = jnp.minimum(f_pre, 0.0) - jnp.log1p(jnp.exp(-jnp.abs(f_pre)))
    lf = jnp.where(valid, lf, 0.0)
    b = _lane_prefix(lf, jnp.add, 0.0, lane)
    r = jnp.where(valid, i_pre - b, NEG_INF)
    cm = _lane_prefix(r, jnp.maximum, NEG_INF, lane)
    g_tot = jnp.sum(jnp.where(lane == CHUNK - 1, b, 0.0), axis=1, keepdims=True)
    c_last = jnp.max(r, axis=1, keepdims=True)
    g_sc[...] = jnp.broadcast_to(g_tot, shape)
    c_sc[...] = jnp.broadcast_to(c_last, shape)

    def body(c, m):
        rows = pl.ds(pl.multiple_of(c * n_bh, n_bh), n_bh)
        m_sc[rows, :] = m
        return g_sc[rows, :] + jnp.maximum(m, c_sc[rows, :])

    lax.fori_loop(0, n_chunks, body, jnp.zeros((n_bh, shape[1]), F32))
    m = m_sc[...]
    big_m = jnp.maximum(m, cm)
    m_last = jnp.maximum(m, c_sc[...])
    r_ref[...] = jnp.where(valid, r, 0.0)
    mt_ref[...] = big_m
    wi_ref[...] = jnp.exp(m - big_m)
    et_ref[...] = jnp.exp(-b - big_m)
    ws_ref[...] = jnp.exp(r - m_last)
    dc_ref[...] = jnp.exp(m - m_last)


def _gate_prep(gi, gf, ib, fb, n_chunks, n_bh):
    shape = jax.ShapeDtypeStruct(gi.shape, F32)
    return pl.pallas_call(
        functools.partial(_gate_prep_kernel, n_chunks=n_chunks, n_bh=n_bh),
        out_shape=(shape,) * 6,
        scratch_shapes=[pltpu.VMEM(gi.shape, F32)] * 3,
        compiler_params=_params(None, 32),
        name="gate_prep",
    )(gi, gf, ib, fb)


_QK_HALO = BF16_SUBLANES


def _mlstm_kernel(qk_ref, halo_ref, v_ref, o_ref, cw_ref, cb_ref, cols_ref, rows_ref, ng_ref, *refs,
                  heads, dh, chunks_per_step, n_cast):
    cast_in = refs[:n_cast]
    out_ref = refs[n_cast]
    cast_out = refs[n_cast + 1:2 * n_cast + 1]
    xs_ref, c_ref, n_ref = refs[2 * n_cast + 1:]
    for src, dst in zip(cast_in, cast_out):
        dst[...] = src[...].astype(BF16)
    group_id = pl.program_id(1)
    dm = heads * dh
    n_slabs = 2 * dm // LANES

    @pl.when(group_id == 0)
    def _():
        c_ref[...] = jnp.zeros(c_ref.shape, F32)
        n_ref[...] = jnp.zeros(n_ref.shape, F32)

    for s in range(n_slabs):
        lanes = slice(s * LANES, (s + 1) * LANES)
        xs_ref[s, 0:_QK_HALO, :] = jnp.where(group_id == 0, 0.0, halo_ref[:, lanes].astype(F32))
        xs_ref[s, _QK_HALO:, :] = qk_ref[:, lanes].astype(F32)

    def conv_silu(col0, row0):
        parts = []
        for s in range(col0 // LANES, (col0 + dh) // LANES):
            lanes = slice(s * LANES, (s + 1) * LANES)
            acc = cb_ref[:, lanes]
            for j in range(MLSTM_CONV):
                off = row0 + _QK_HALO - (MLSTM_CONV - 1) + j
                acc = acc + cw_ref[j:j + 1, lanes] * xs_ref[s, off:off + CHUNK, :]
            parts.append(acc * _sigmoid(acc))
        return jnp.concatenate(parts, axis=1)

    tri = (lax.broadcasted_iota(jnp.int32, (CHUNK, CHUNK), 0)
           >= lax.broadcasted_iota(jnp.int32, (CHUNK, CHUNK), 1))
    eye = (lax.broadcasted_iota(jnp.int32, (dh, dh), 0)
           == lax.broadcasted_iota(jnp.int32, (dh, dh), 1)).astype(BF16)
    for cc in range(chunks_per_step):
        row0 = cc * CHUNK
        rs = slice(row0, row0 + CHUNK)
        for h in range(heads):
            hs = slice(h * dh, (h + 1) * dh)
            q = conv_silu(h * dh, row0)
            k = conv_silu(dm + h * dh, row0) * (dh ** -0.5)
            qb = q.astype(BF16)
            kb = k.astype(BF16)
            v = v_ref[rs, hs]
            mt = cols_ref[cc, :, h:h + 1]
            wi = cols_ref[cc, :, heads + h:heads + h + 1]
            et = cols_ref[cc, :, 2 * heads + h:2 * heads + h + 1]
            ws = cols_ref[cc, :, 3 * heads + h:3 * heads + h + 1]
            r = rows_ref[cc, h:h + 1, 0:CHUNK]
            dc = rows_ref[cc, heads + h:heads + h + 1, 0:1]

            s = lax.dot_general(qb, kb, _NT, preferred_element_type=F32)
            sw = s * jnp.where(tri, jnp.exp(r - mt), 0.0)
            c_old = c_ref[h]
            inter = lax.dot_general(qb, c_old.astype(BF16), _NT, preferred_element_type=F32)
            intra = jnp.dot(sw.astype(BF16), v, preferred_element_type=F32)
            n_old = n_ref[h:h + 1, :]
            num = intra + wi * inter
            den = (jnp.sum(sw, axis=-1, keepdims=True)
                   + wi * jnp.sum(q * n_old, axis=-1, keepdims=True))
            hh = num / jnp.maximum(jnp.abs(den), et)
            hn = hh * lax.rsqrt(jnp.mean(hh * hh, axis=-1, keepdims=True) + EPS) * ng_ref[:, hs]
            out_ref[rs, hs] = (_sigmoid(o_ref[rs, hs].astype(F32)) * hn).astype(out_ref.dtype)

            vw = (v.astype(F32) * ws).astype(BF16)
            vw_t = lax.dot_general(eye, vw, _NT, preferred_element_type=F32).astype(BF16)
            upd = jnp.dot(vw_t, kb, preferred_element_type=F32)
            c_ref[h] = dc * c_old + upd
            n_ref[h:h + 1, :] = dc * n_old + jnp.sum(k * ws, axis=0, keepdims=True)


def _mlstm(proj, conv_w, conv_b, cols, rows, norm_g, batch, n_chunks, chunks_per_step, cast_weights):
    t = proj.shape[0]
    dm = norm_g.shape[1]
    dh = dm // MLSTM_HEADS
    n_groups = n_chunks // chunks_per_step
    rows_per_step = chunks_per_step * CHUNK
    halo_blocks = rows_per_step // _QK_HALO
    n_steps = batch * n_groups

    def cast_spec(w):
        assert w.shape[0] % (n_steps * BF16_SUBLANES) == 0
        return pl.BlockSpec((w.shape[0] // n_steps, w.shape[1]), lambda b, c: (b * n_groups + c, 0))

    cast_specs = [cast_spec(w) for w in cast_weights]
    outs = pl.pallas_call(
        functools.partial(_mlstm_kernel, heads=MLSTM_HEADS, dh=dh, chunks_per_step=chunks_per_step,
                          n_cast=len(cast_weights)),
        out_shape=(jax.ShapeDtypeStruct((t, dm), BF16),
                   *[jax.ShapeDtypeStruct(w.shape, BF16) for w in cast_weights]),
        grid=(batch, n_groups),
        in_specs=[
            pl.BlockSpec((rows_per_step, 2 * dm), lambda b, c: (b * n_groups + c, 0)),
            pl.BlockSpec((_QK_HALO, 2 * dm),
                         lambda b, c: (jnp.maximum((b * n_groups + c) * halo_blocks - 1, 0), 0)),
            pl.BlockSpec((rows_per_step, dm), lambda b, c: (b * n_groups + c, 2)),
            pl.BlockSpec((rows_per_step, dm), lambda b, c: (b * n_groups + c, 3)),
            pl.BlockSpec((MLSTM_CONV, 2 * dm), lambda b, c: (0, 0)),
            pl.BlockSpec((1, 2 * dm), lambda b, c: (0, 0)),
            pl.BlockSpec((None, chunks_per_step, CHUNK, LANES), lambda b, c: (b, c, 0, 0)),
            pl.BlockSpec((None, chunks_per_step, 2 * MLSTM_HEADS, LANES), lambda b, c: (b, c, 0, 0)),
            pl.BlockSpec((1, dm), lambda b, c: (0, 0)),
            *cast_specs,
        ],
        out_specs=(pl.BlockSpec((rows_per_step, dm), lambda b, c: (b * n_groups + c, 0)), *cast_specs),
        scratch_shapes=[
            pltpu.VMEM((2 * dm // LANES, _QK_HALO + rows_per_step, LANES), F32),
            pltpu.VMEM((MLSTM_HEADS, dh, dh), F32),
            pltpu.VMEM((2 * MLSTM_HEADS, dh), F32),
        ],
        compiler_params=_params(("arbitrary", "arbitrary"), 40),
        name="mlstm",
    )(proj, proj, proj, proj, conv_w, conv_b, cols, rows, norm_g, *cast_weights)
    return outs[0], outs[1:]


def _attn_kernel(*refs, heads, dh, n_pieces, n_cast):
    q_ref = refs[0]
    k_refs = refs[1:1 + n_pieces]
    v_refs = refs[1 + n_pieces:1 + 2 * n_pieces]
    tab_refs = refs[1 + 2 * n_pieces:1 + 3 * n_pieces]
    cast_in = refs[1 + 3 * n_pieces:1 + 3 * n_pieces + n_cast]
    out_ref = refs[1 + 3 * n_pieces + n_cast]
    cast_out = refs[2 + 3 * n_pieces + n_cast:]
    for src, dst in zip(cast_in, cast_out):
        dst[...] = src[...].astype(BF16)
    scale2 = (dh ** -0.5) * _LOG2_E
    for h in range(heads):
        hs = slice(h * dh, (h + 1) * dh)
        q = q_ref[:, hs]
        scores = [lax.dot_general(q, k_ref[:, hs], _NT, preferred_element_type=F32) * scale2 + tab_ref[h]
                  for k_ref, tab_ref in zip(k_refs, tab_refs)]
        m = jnp.max(functools.reduce(jnp.maximum, scores), axis=-1, keepdims=True)
        es = [jnp.exp2(s - m) for s in scores]
        denom = jnp.sum(functools.reduce(jnp.add, es), axis=-1, keepdims=True)
        acc = functools.reduce(jnp.add, [
            jnp.dot(e.astype(BF16), v_ref[:, hs], preferred_element_type=F32) for e, v_ref in zip(es, v_refs)])
        out_ref[:, hs] = (acc / denom).astype(out_ref.dtype)


def _attention(proj, tab, d_attn, batch, seq, q_rows, cast_weights):
    t = proj.shape[0]
    dh = d_attn // ATTN_HEADS
    n_q = seq // q_rows
    n_steps = batch * n_q
    pad_blocks = (LEFT_CHUNKS * CHUNK) // q_rows
    n_pieces = 1 + pad_blocks
    assert tab.shape == (n_pieces + 1, ATTN_HEADS, q_rows, q_rows)
    q_col = proj.shape[1] // d_attn - 3

    def window_spec(col_block, p):
        return pl.BlockSpec(
            (q_rows, d_attn),
            lambda b, qi: (b * n_q + jnp.maximum(qi + p - pad_blocks, 0), col_block))

    def tab_spec(p):
        return pl.BlockSpec(
            (None, ATTN_HEADS, q_rows, q_rows),
            lambda b, qi: (jnp.where(qi + p >= pad_blocks, p, n_pieces), 0, 0, 0))

    def cast_spec(w):
        assert w.shape[0] % (n_steps * BF16_SUBLANES) == 0
        return pl.BlockSpec((w.shape[0] // n_steps, w.shape[1]), lambda b, qi: (b * n_q + qi, 0))

    cast_specs = [cast_spec(w) for w in cast_weights]
    outs = pl.pallas_call(
        functools.partial(_attn_kernel, heads=ATTN_HEADS, dh=dh, n_pieces=n_pieces, n_cast=len(cast_weights)),
        out_shape=(jax.ShapeDtypeStruct((t, d_attn), BF16),
                   *[jax.ShapeDtypeStruct(w.shape, BF16) for w in cast_weights]),
        grid=(batch, n_q),
        in_specs=[pl.BlockSpec((q_rows, d_attn), lambda b, qi: (b * n_q + qi, q_col))]
        + [window_spec(q_col + 1, p) for p in range(n_pieces)]
        + [window_spec(q_col + 2, p) for p in range(n_pieces)]
        + [tab_spec(p) for p in range(n_pieces)]
        + cast_specs,
        out_specs=(pl.BlockSpec((q_rows, d_attn), lambda b, qi: (b * n_q + qi, 0)), *cast_specs),
        compiler_params=_params(("arbitrary", "arbitrary"), 48),
        name="chunk_attn",
    )(proj, *([proj] * (2 * n_pieces)), *([tab] * n_pieces), *cast_weights)
    return outs[0], outs[1:]


def _bias_table(rel, q_rows):
    heads = rel.shape[0]
    pad_rows = LEFT_CHUNKS * CHUNK
    width = q_rows + pad_rows
    d_lo, d_hi = pad_rows - width + 1, pad_rows + q_rows - 1
    mid = rel[:, max(d_lo, -MAX_REL) + MAX_REL:min(d_hi, MAX_REL) + MAX_REL + 1]
    left = jnp.repeat(rel[:, :1], max(0, -MAX_REL - d_lo), axis=1)
    right = jnp.repeat(rel[:, -1:], max(0, d_hi - MAX_REL), axis=1)
    by_dist = jnp.concatenate([left, mid, right], axis=1)[:, ::-1]
    length = by_dist.shape[1]
    ring = jnp.pad(by_dist, ((0, 0), (0, 1)))
    skew = jnp.tile(ring, (1, q_rows))[:, :q_rows * length].reshape(heads, q_rows, length)
    bias = skew[:, :, q_rows - 1:q_rows - 1 + width]
    q_chunk = jnp.arange(q_rows)[:, None] // CHUNK
    k_chunk = jnp.arange(width)[None, :] // CHUNK
    in_band = (k_chunk >= q_chunk) & (k_chunk <= q_chunk + LEFT_CHUNKS)
    tab = jnp.where(in_band[None], bias * _LOG2_E, NEG_INF)
    pieces = [tab[:, :, p * q_rows:(p + 1) * q_rows] for p in range(width // q_rows)]
    pieces.append(jnp.full((heads, q_rows, q_rows), NEG_INF, F32))
    return jnp.stack(pieces, axis=0)


def _out_proj_kernel(a_ref, b_ref, wa_ref, wb_ref, x_ref, g_ref, o_ref, u_ref):
    acc = jnp.dot(a_ref[...], wa_ref[...], preferred_element_type=F32)
    acc = acc + jnp.dot(b_ref[...], wb_ref[...], preferred_element_type=F32)
    h = x_ref[...] + acc
    o_ref[...] = h
    u_ref[...] = _rmsnorm_f32(h, g_ref[...]).astype(BF16)


def _out_proj(h_a, h_b, w, x2, g_next, tm):
    t, d = x2.shape
    ka, kb = h_a.shape[1], h_b.shape[1]
    assert ka == kb
    return pl.pallas_call(
        _out_proj_kernel,
        out_shape=(jax.ShapeDtypeStruct((t, d), F32), jax.ShapeDtypeStruct((t, d), BF16)),
        grid=(t // tm,),
        in_specs=[
            pl.BlockSpec((tm, ka), lambda i: (i, 0)),
            pl.BlockSpec((tm, kb), lambda i: (i, 0)),
            pl.BlockSpec((None, ka, d), lambda i: (0, 0, 0)),
            pl.BlockSpec((None, kb, d), lambda i: (1, 0, 0)),
            pl.BlockSpec((tm, d), lambda i: (i, 0)),
            pl.BlockSpec((1, d), lambda i: (0, 0)),
        ],
        out_specs=(pl.BlockSpec((tm, d), lambda i: (i, 0)), pl.BlockSpec((tm, d), lambda i: (i, 0))),
        compiler_params=_params(("arbitrary",), 56),
        name="out_proj",
    )(h_a, h_b, w, w, x2, g_next)


def _mlp_kernel(h_ref, u_ref, w1_ref, w2_ref, gp_ref, *out_refs, final_norm):
    o_ref = out_refs[0]
    j = pl.program_id(1)

    @pl.when(j == 0)
    def _():
        o_ref[...] = h_ref[...]

    a = jnp.dot(u_ref[...], w1_ref[...], preferred_element_type=F32)
    a = jnp.square(jnp.maximum(a, 0.0)).astype(BF16)
    o_ref[...] += jnp.dot(a, w2_ref[...], preferred_element_type=F32)

    @pl.when(j == pl.num_programs(1) - 1)
    def _():
        normed = _rmsnorm_f32(o_ref[...], gp_ref[...])
        if final_norm:
            o_ref[...] = normed
        else:
            out_refs[1][...] = normed.astype(BF16)


def _mlp(h2, u, w1, w2, layer, g_post, final_norm, tm, tf):
    t, d = h2.shape
    f = w1.shape[2]
    row_tile = pl.BlockSpec((tm, d), lambda i, j: (i, 0))
    out_shape = [jax.ShapeDtypeStruct((t, d), F32)]
    if not final_norm:
        out_shape.append(jax.ShapeDtypeStruct((t, d), BF16))
    return pl.pallas_call(
        functools.partial(_mlp_kernel, final_norm=final_norm),
        out_shape=tuple(out_shape),
        grid=(t // tm, f // tf),
        in_specs=[
            row_tile,
            row_tile,
            pl.BlockSpec((None, d, tf), lambda i, j: (layer, 0, j)),
            pl.BlockSpec((None, tf, d), lambda i, j: (layer, j, 0)),
            pl.BlockSpec((1, d), lambda i, j: (0, 0)),
        ],
        out_specs=tuple([row_tile] * len(out_shape)),
        compiler_params=_params(("arbitrary", "arbitrary"), 62),
        name="mlp_final" if final_norm else "mlp",
    )(h2, u, w1, w2, g_post)


def _glu_kernel(u_ref, wa_ref, wg_ref, ba_ref, bg_ref, o_ref):
    u = u_ref[...]
    a = jnp.dot(u, wa_ref[...], preferred_element_type=F32) + ba_ref[...]
    gate = jnp.dot(u, wg_ref[...], preferred_element_type=F32) + bg_ref[...]
    o_ref[...] = (a * _sigmoid(gate)).astype(o_ref.dtype)


def _glu(u, pw1, pw1_b, tm, tn):
    t, d = u.shape
    n = pw1.shape[2] // 2
    nb = n // tn
    return pl.pallas_call(
        _glu_kernel,
        out_shape=jax.ShapeDtypeStruct((t, n), BF16),
        grid=(t // tm, nb),
        in_specs=[
            pl.BlockSpec((tm, d), lambda i, j: (i, 0)),
            pl.BlockSpec((None, d, tn), lambda i, j: (0, 0, j)),
            pl.BlockSpec((None, d, tn), lambda i, j: (0, 0, j + nb)),
            pl.BlockSpec((1, tn), lambda i, j: (0, j)),
            pl.BlockSpec((1, tn), lambda i, j: (0, j + nb)),
        ],
        out_specs=pl.BlockSpec((tm, tn), lambda i, j: (i, j)),
        compiler_params=_params(("arbitrary", "arbitrary"), 56),
        name="conv_glu",
    )(u, pw1, pw1, pw1_b, pw1_b)


_DW_HALO = 2 * BF16_SUBLANES
_DW_ROWS = 64


def _conv_tail_kernel(z_ref, halo_ref, dw_ref, dwb_ref, lng_ref, lnb_ref, w2_ref, b2_ref, h_ref, g_ref,
                      o_ref, u_ref, zs_ref, y_ref, *, tm, tiles_per_seq):
    i = pl.program_id(0)
    n_slabs = z_ref.shape[1] // LANES
    seq_start = (i % tiles_per_seq) == 0
    for s in range(n_slabs):
        lanes = slice(s * LANES, (s + 1) * LANES)
        zs_ref[s, 0:_DW_HALO, :] = jnp.where(seq_start, 0.0, halo_ref[:, lanes].astype(F32))
        zs_ref[s, _DW_HALO:, :] = z_ref[:, lanes].astype(F32)
    first_tap = _DW_HALO - (CONV_WIDTH - 1)

    def slab_body(s, carry):
        lanes = pl.ds(pl.multiple_of(s * LANES, LANES), LANES)
        for rb in range(tm // _DW_ROWS):
            r0 = rb * _DW_ROWS
            acc = jnp.broadcast_to(dwb_ref[:, lanes], (_DW_ROWS, LANES))
            for k in range(CONV_WIDTH):
                acc = acc + dw_ref[k:k + 1, lanes] * zs_ref[s, r0 + first_tap + k:r0 + first_tap + k + _DW_ROWS, :]
            y_ref[r0:r0 + _DW_ROWS, lanes] = acc
        return carry

    lax.fori_loop(0, n_slabs, slab_body, 0)
    y = y_ref[...]
    mu = jnp.mean(y, axis=-1, keepdims=True)
    yc = y - mu
    var = jnp.mean(yc * yc, axis=-1, keepdims=True)
    yn = yc * lax.rsqrt(var + EPS) * lng_ref[...] + lnb_ref[...]
    a = (yn * _sigmoid(yn)).astype(BF16)
    h = h_ref[...] + jnp.dot(a, w2_ref[...], preferred_element_type=F32) + b2_ref[...]
    o_ref[...] = h
    u_ref[...] = _rmsnorm_f32(h, g_ref[...]).astype(BF16)


def _conv_tail(z, dw_w, dw_b, ln_g, ln_b, pw2, pw2_b, h2, g_next, seq, tm):
    t, d = h2.shape
    halo_blocks = tm // _DW_HALO
    row_tile = pl.BlockSpec((tm, d), lambda i: (i, 0))
    vec = pl.BlockSpec((1, d), lambda i: (0, 0))
    return pl.pallas_call(
        functools.partial(_conv_tail_kernel, tm=tm, tiles_per_seq=seq // tm),
        out_shape=(jax.ShapeDtypeStruct((t, d), F32), jax.ShapeDtypeStruct((t, d), BF16)),
        grid=(t // tm,),
        in_specs=[
            row_tile,
            pl.BlockSpec((_DW_HALO, d), lambda i: (jnp.maximum(i * halo_blocks - 1, 0), 0)),
            pl.BlockSpec(dw_w.shape, lambda i: (0, 0)),
            vec, vec, vec,
            pl.BlockSpec((None, d, d), lambda i: (0, 0, 0), pipeline_mode=pl.Buffered(1)),
            vec,
            row_tile,
            vec,
        ],
        out_specs=(row_tile, row_tile),
        scratch_shapes=[pltpu.VMEM((d // LANES, _DW_HALO + tm, LANES), F32), pltpu.VMEM((tm, d), F32)],
        compiler_params=_params(("arbitrary",), 56),
        name="conv_tail",
    )(z, z, dw_w, dw_b, ln_g, ln_b, pw2, pw2_b, h2, g_next)


def _tiles(seq):
    def rows(want):
        return min(want, seq)

    return dict(
        repack_rows=512,
        in_proj_tm=rows(512),
        mlstm_chunks_per_step=min(4, seq // CHUNK),
        attn_q_rows=rows(256),
        out_proj_tm=rows(512),
        mlp_tm=rows(512), mlp_tf=2048,
        glu_tm=rows(1024), glu_tn=1024,
        conv_tm=rows(512),
    )


def _row(v):
    return v.reshape(1, -1).astype(F32)


def kernel(x, mixer_norm_g, mix_w_in, qk_conv_w, qk_conv_b, igate_b, fgate_b, mlstm_norm_g, rel_bias,
           mix_w_out, conv_pw1_w, conv_pw1_b, conv_dw_w, conv_dw_b, conv_ln_g, conv_ln_b, conv_pw2_w,
           conv_pw2_b, mlp_norm_g, mlp_w1, mlp_w2, final_norm_g):
    batch, seq, d = x.shape
    tokens = batch * seq
    n_chunks = seq // CHUNK
    dm = d // 2
    da = d - dm
    heads = MLSTM_HEADS
    tl = _tiles(seq)
    x2 = x.reshape(tokens, d)

    w_in_t = jnp.swapaxes(mix_w_in, 1, 2)
    gate_lo = 4 * dm
    gate_hi = gate_lo + 2 * heads
    w_main_t = _drop_rows_bf16(w_in_t, gate_lo, gate_hi, tl["repack_rows"])
    w_gate_t = jnp.pad(w_in_t[0, gate_lo:gate_hi], ((0, LANES - 2 * heads), (0, 0))).astype(BF16)
    proj, gates = _in_proj(x2, _row(mixer_norm_g[0]), w_main_t, w_gate_t, tl["in_proj_tm"])

    n_bh = batch * heads
    g8 = gates[:, :2 * heads].reshape(batch, n_chunks, CHUNK, 2, heads)
    g8 = jnp.transpose(g8, (3, 1, 0, 4, 2)).reshape(2, n_chunks * n_bh, CHUNK)
    g8 = jnp.pad(g8, ((0, 0), (0, 0), (0, LANES - CHUNK)))
    bias_rows = lambda bvec: jnp.broadcast_to(
        jnp.tile(bvec.astype(F32), batch * n_chunks)[:, None], (n_chunks * n_bh, LANES))
    r, mt, wi, et, ws, dc = _gate_prep(g8[0], g8[1], bias_rows(igate_b[0]), bias_rows(fgate_b[0]),
                                       n_chunks, n_bh)
    per_frame = jnp.stack([mt, wi, et, ws], axis=0)[:, :, :CHUNK]
    per_frame = per_frame.reshape(4, n_chunks, batch, heads, CHUNK)
    cols = jnp.transpose(per_frame, (2, 1, 4, 0, 3)).reshape(batch, n_chunks, CHUNK, 4 * heads)
    cols = jnp.pad(cols, ((0, 0), (0, 0), (0, 0), (0, LANES - 4 * heads)))
    rows = jnp.concatenate([r.reshape(n_chunks, batch, heads, LANES),
                            dc.reshape(n_chunks, batch, heads, LANES)], axis=2)
    rows = jnp.transpose(rows, (1, 0, 2, 3))

    later_weights = [mlp_w1, mlp_w2, mix_w_out, conv_pw1_w, conv_pw2_w]
    h_a, cast = _mlstm(proj, qk_conv_w[0].astype(F32), _row(qk_conv_b[0]), cols, rows,
                       _row(mlstm_norm_g[0]), batch, n_chunks, tl["mlstm_chunks_per_step"],
                       [w.reshape(-1, w.shape[2]) for w in later_weights])
    mlp_w1_bf, mlp_w2_bf, w_out_bf, pw1_bf, pw2_bf = [c.reshape(w.shape) for c, w in zip(cast, later_weights)]

    tab = _bias_table(rel_bias[0].astype(F32), tl["attn_q_rows"])
    h_b, _ = _attention(proj, tab, da, batch, seq, tl["attn_q_rows"], [])

    h, u = _out_proj(h_a, h_b, w_out_bf.reshape(2, dm, d), x2, _row(mlp_norm_g[0]), tl["out_proj_tm"])

    h, u = _mlp(h, u, mlp_w1_bf, mlp_w2_bf, 0, _row(mixer_norm_g[1]), False, tl["mlp_tm"], tl["mlp_tf"])

    z = _glu(u, pw1_bf, _row(conv_pw1_b[0]), tl["glu_tm"], tl["glu_tn"])
    h, u = _conv_tail(z, conv_dw_w[0].astype(F32), _row(conv_dw_b[0]), _row(conv_ln_g[0]),
                      _row(conv_ln_b[0]), pw2_bf, _row(conv_pw2_b[0]), h,
                      _row(mlp_norm_g[1]), seq, tl["conv_tm"])

    (out,) = _mlp(h, u, mlp_w1_bf, mlp_w2_bf, 1, _row(final_norm_g), True, tl["mlp_tm"], tl["mlp_tf"])
    return out.reshape(batch, seq, d)
```

```python
import functools
import math

import jax
import jax.numpy as jnp
from jax import lax
from jax.experimental import pallas as pl
from jax.experimental.pallas import tpu as pltpu

CHUNK = 64
MLSTM_HEADS = 4
MLSTM_CONV = 4
ATTN_HEADS = 8
LEFT_CHUNKS = 8
MAX_REL = 256
CONV_WIDTH = 31
EPS = 1e-6

V7X_VMEM_BYTES = 64 * 1024 * 1024
LANES = 128
BF16_SUBLANES = 16

F32 = jnp.float32
BF16 = jnp.bfloat16
NEG_INF = float("-inf")
_LOG2_E = math.log2(math.e)

_NT = (((1,), (1,)), ((), ()))
_TN = (((0,), (0,)), ((), ()))


def _params(semantics, vmem_mib):
    assert vmem_mib * 1024 * 1024 <= V7X_VMEM_BYTES
    return pltpu.CompilerParams(dimension_semantics=semantics,
                                vmem_limit_bytes=vmem_mib * 1024 * 1024)


def _rmsnorm_f32(x, g):
    return x * lax.rsqrt(jnp.mean(x * x, axis=-1, keepdims=True) + EPS) * g


def _sigmoid(x):
    return 0.5 * jnp.tanh(0.5 * x) + 0.5


def _drop_rows_kernel(w_ref, o_ref):
    o_ref[...] = w_ref[0].astype(BF16)


def _drop_rows_bf16(w_t, lo, hi, rows):
    n_in, d = w_t.shape[1:]
    n = n_in - (hi - lo)
    assert lo % rows == 0 and n % rows == 0 and (hi - lo) % 8 == 0
    return pl.pallas_call(
        _drop_rows_kernel,
        out_shape=jax.ShapeDtypeStruct((n, d), BF16),
        grid=(n // rows,),
        in_specs=[pl.BlockSpec((pl.Element(1), pl.Element(rows), pl.Element(d)),
                               lambda i: (0, pl.multiple_of(jnp.where(i * rows < lo, i * rows, i * rows + (hi - lo)), 8), 0))],
        out_specs=pl.BlockSpec((rows, d), lambda i: (i, 0)),
        compiler_params=_params(("arbitrary",), 32),
        name="w_in_repack",
    )(w_t)


def _in_proj_kernel(x_ref, g_ref, w_ref, wg_ref, o_ref, og_ref):
    u = _rmsnorm_f32(x_ref[...], g_ref[...]).astype(BF16)
    og_ref[...] = lax.dot_general(u, wg_ref[...], _NT, preferred_element_type=F32)
    o_ref[...] = lax.dot_general(u, w_ref[...], _NT, preferred_element_type=F32).astype(o_ref.dtype)


def _in_proj(x2, g, w_main_t, w_gate_t, tm):
    t, d = x2.shape
    n = w_main_t.shape[0]
    resident = pl.Buffered(1)
    return pl.pallas_call(
        _in_proj_kernel,
        out_shape=(jax.ShapeDtypeStruct((t, n), BF16), jax.ShapeDtypeStruct((t, LANES), F32)),
        grid=(t // tm,),
        in_specs=[
            pl.BlockSpec((tm, d), lambda i: (i, 0)),
            pl.BlockSpec((1, d), lambda i: (0, 0)),
            pl.BlockSpec((n, d), lambda i: (0, 0), pipeline_mode=resident),
            pl.BlockSpec((LANES, d), lambda i: (0, 0), pipeline_mode=resident),
        ],
        out_specs=(pl.BlockSpec((tm, n), lambda i: (i, 0)),
                   pl.BlockSpec((tm, LANES), lambda i: (i, 0))),
        compiler_params=_params(("arbitrary",), 62),
        name="in_proj",
    )(x2, g, w_main_t, w_gate_t)


def _lane_prefix(x, op, ident, lane):
    shift = 1
    while shift < CHUNK:
        y = pltpu.roll(x, shift, axis=1)
        x = op(x, jnp.where(lane >= shift, y, ident))
        shift *= 2
    return x


def _gate_prep_kernel(gi_ref, gf_ref, ib_ref, fb_ref,
                      r_ref, mt_ref, wi_ref, et_ref, ws_ref, dc_ref,
                      g_sc, c_sc, m_sc, *, n_chunks, n_bh):
    shape = gi_ref.shape
    lane = lax.broadcasted_iota(jnp.int32, shape, 1)
    valid = lane < CHUNK
    i_pre = gi_ref[...] + ib_ref[...]
    f_pre = gf_ref[...] + fb_ref[...]
    lf = jnp.minimum(f_pre, 0.0) - jnp.log1p(jnp.exp(-jnp.abs(f_pre)))
    lf = jnp.where(valid, lf, 0.0)
    b = _lane_prefix(lf, jnp.add, 0.0, lane)
    r = jnp.where(valid, i_pre - b, NEG_INF)
    cm = _lane_prefix(r, jnp.maximum, NEG_INF, lane)
    g_tot = jnp.sum(jnp.where(lane == CHUNK - 1, b, 0.0), axis=1, keepdims=True)
    c_last = jnp.max(r, axis=1, keepdims=True)
    g_sc[...] = jnp.broadcast_to(g_tot, shape)
    c_sc[...] = jnp.broadcast_to(c_last, shape)

    def body(c, m):
        rows = pl.ds(pl.multiple_of(c * n_bh, n_bh), n_bh)
        m_sc[rows, :] = m
        return g_sc[rows, :] + jnp.maximum(m, c_sc[rows, :])

    lax.fori_loop(0, n_chunks, body, jnp.zeros((n_bh, shape[1]), F32))
    m = m_sc[...]
    big_m = jnp.maximum(m, cm)
    m_last = jnp.maximum(m, c_sc[...])
    r_ref[...] = jnp.where(valid, r, 0.0)
    mt_ref[...] = big_m
    wi_ref[...] = jnp.exp(m - big_m)
    et_ref[...] = jnp.exp(-b - big_m)
    ws_ref[...] = jnp.exp(r - m_last)
    dc_ref[...] = jnp.exp(m - m_last)


def _gate_prep(gi, gf, ib, fb, n_chunks, n_bh):
    shape = jax.ShapeDtypeStruct(gi.shape, F32)
    return pl.pallas_call(
        functools.partial(_gate_prep_kernel, n_chunks=n_chunks, n_bh=n_bh),
        out_shape=(shape,) * 6,
        scratch_shapes=[pltpu.VMEM(gi.shape, F32)] * 3,
        compiler_params=_params(None, 32),
        name="gate_prep",
    )(gi, gf, ib, fb)


_QK_HALO = BF16_SUBLANES


def _mlstm_kernel(qk_ref, halo_ref, v_ref, o_ref, cw_ref, cb_ref, cols_ref, rows_ref, ng_ref, *refs,
                  heads, dh, chunks_per_step, n_cast):
    cast_in = refs[:n_cast]
    out_ref = refs[n_cast]
    cast_out = refs[n_cast + 1:2 * n_cast + 1]
    xs_ref, c_ref, n_ref = refs[2 * n_cast + 1:]
    for src, dst in zip(cast_in, cast_out):
        dst[...] = src[...].astype(BF16)
    group_id = pl.program_id(1)
    dm = heads * dh
    n_slabs = 2 * dm // LANES

    @pl.when(group_id == 0)
    def _():
        c_ref[...] = jnp.zeros(c_ref.shape, F32)
        n_ref[...] = jnp.zeros(n_ref.shape, F32)

    for s in range(n_slabs):
        lanes = slice(s * LANES, (s + 1) * LANES)
        xs_ref[s, 0:_QK_HALO, :] = jnp.where(group_id == 0, 0.0, halo_ref[:, lanes].astype(F32))
        xs_ref[s, _QK_HALO:, :] = qk_ref[:, lanes].astype(F32)

    def conv_silu(col0, row0):
        parts = []
        for s in range(col0 // LANES, (col0 + dh) // LANES):
            lanes = slice(s * LANES, (s + 1) * LANES)
            acc = cb_ref[:, lanes]
            for j in range(MLSTM_CONV):
                off = row0 + _QK_HALO - (MLSTM_CONV - 1) + j
                acc = acc + cw_ref[j:j + 1, lanes] * xs_ref[s, off:off + CHUNK, :]
            parts.append(acc * _sigmoid(acc))
        return jnp.concatenate(parts, axis=1)

    tri = (lax.broadcasted_iota(jnp.int32, (CHUNK, CHUNK), 0)
           >= lax.broadcasted_iota(jnp.int32, (CHUNK, CHUNK), 1))
    eye = (lax.broadcasted_iota(jnp.int32, (dh, dh), 0)
           == lax.broadcasted_iota(jnp.int32, (dh, dh), 1)).astype(BF16)
    for cc in range(chunks_per_step):
        row0 = cc * CHUNK
        rs = slice(row0, row0 + CHUNK)
        for h in range(heads):
            hs = slice(h * dh, (h + 1) * dh)
            q = conv_silu(h * dh, row0)
            k = conv_silu(dm + h * dh, row0) * (dh ** -0.5)
            qb = q.astype(BF16)
            kb = k.astype(BF16)
            v = v_ref[rs, hs]
            mt = cols_ref[cc, :, h:h + 1]
            wi = cols_ref[cc, :, heads + h:heads + h + 1]
            et = cols_ref[cc, :, 2 * heads + h:2 * heads + h + 1]
            ws = cols_ref[cc, :, 3 * heads + h:3 * heads + h + 1]
            r = rows_ref[cc, h:h + 1, 0:CHUNK]
            dc = rows_ref[cc, heads + h:heads + h + 1, 0:1]

            s = lax.dot_general(qb, kb, _NT, preferred_element_type=F32)
            sw = s * jnp.where(tri, jnp.exp(r - mt), 0.0)
            c_old = c_ref[h]
            inter = lax.dot_general(qb, c_old.astype(BF16), _NT, preferred_element_type=F32)
            intra = jnp.dot(sw.astype(BF16), v, preferred_element_type=F32)
            n_old = n_ref[h:h + 1, :]
            num = intra + wi * inter
            den = (jnp.sum(sw, axis=-1, keepdims=True)
                   + wi * jnp.sum(q * n_old, axis=-1, keepdims=True))
            hh = num / jnp.maximum(jnp.abs(den), et)
            hn = hh * lax.rsqrt(jnp.mean(hh * hh, axis=-1, keepdims=True) + EPS) * ng_ref[:, hs]
            out_ref[rs, hs] = (_sigmoid(o_ref[rs, hs].astype(F32)) * hn).astype(out_ref.dtype)

            vw = (v.astype(F32) * ws).astype(BF16)
            vw_t = lax.dot_general(eye, vw, _NT, preferred_element_type=F32).astype(BF16)
            upd = jnp.dot(vw_t, kb, preferred_element_type=F32)
            c_ref[h] = dc * c_old + upd
            n_ref[h:h + 1, :] = dc * n_old + jnp.sum(k * ws, axis=0, keepdims=True)


def _mlstm(proj, conv_w, conv_b, cols, rows, norm_g, batch, n_chunks, chunks_per_step, cast_weights):
    t = proj.shape[0]
    dm = norm_g.shape[1]
    dh = dm // MLSTM_HEADS
    n_groups = n_chunks // chunks_per_step
    rows_per_step = chunks_per_step * CHUNK
    halo_blocks = rows_per_step // _QK_HALO
    n_steps = batch * n_groups

    def cast_spec(w):
        assert w.shape[0] % (n_steps * BF16_SUBLANES) == 0
        return pl.BlockSpec((w.shape[0] // n_steps, w.shape[1]), lambda b, c: (b * n_groups + c, 0))

    cast_specs = [cast_spec(w) for w in cast_weights]
    outs = pl.pallas_call(
        functools.partial(_mlstm_kernel, heads=MLSTM_HEADS, dh=dh, chunks_per_step=chunks_per_step,
                          n_cast=len(cast_weights)),
        out_shape=(jax.ShapeDtypeStruct((t, dm), BF16),
                   *[jax.ShapeDtypeStruct(w.shape, BF16) for w in cast_weights]),
        grid=(batch, n_groups),
        in_specs=[
            pl.BlockSpec((rows_per_step, 2 * dm), lambda b, c: (b * n_groups + c, 0)),
            pl.BlockSpec((_QK_HALO, 2 * dm),
                         lambda b, c: (jnp.maximum((b * n_groups + c) * halo_blocks - 1, 0), 0)),
            pl.BlockSpec((rows_per_step, dm), lambda b, c: (b * n_groups + c, 2)),
            pl.BlockSpec((rows_per_step, dm), lambda b, c: (b * n_groups + c, 3)),
            pl.BlockSpec((MLSTM_CONV, 2 * dm), lambda b, c: (0, 0)),
            pl.BlockSpec((1, 2 * dm), lambda b, c: (0, 0)),
            pl.BlockSpec((None, chunks_per_step, CHUNK, LANES), lambda b, c: (b, c, 0, 0)),
            pl.BlockSpec((None, chunks_per_step, 2 * MLSTM_HEADS, LANES), lambda b, c: (b, c, 0, 0)),
            pl.BlockSpec((1, dm), lambda b, c: (0, 0)),
            *cast_specs,
        ],
        out_specs=(pl.BlockSpec((rows_per_step, dm), lambda b, c: (b * n_groups + c, 0)), *cast_specs),
        scratch_shapes=[
            pltpu.VMEM((2 * dm // LANES, _QK_HALO + rows_per_step, LANES), F32),
            pltpu.VMEM((MLSTM_HEADS, dh, dh), F32),
            pltpu.VMEM((2 * MLSTM_HEADS, dh), F32),
        ],
        compiler_params=_params(("arbitrary", "arbitrary"), 40),
        name="mlstm",
    )(proj, proj, proj, proj, conv_w, conv_b, cols, rows, norm_g, *cast_weights)
    return outs[0], outs[1:]


def _attn_kernel(*refs, heads, dh, n_pieces, n_cast):
    q_ref = refs[0]
    k_refs = refs[1:1 + n_pieces]
    v_refs = refs[1 + n_pieces:1 + 2 * n_pieces]
    tab_refs = refs[1 + 2 * n_pieces:1 + 3 * n_pieces]
    cast_in = refs[1 + 3 * n_pieces:1 + 3 * n_pieces + n_cast]
    out_ref = refs[1 + 3 * n_pieces + n_cast]
    cast_out = refs[2 + 3 * n_pieces + n_cast:]
    for src, dst in zip(cast_in, cast_out):
        dst[...] = src[...].astype(BF16)
    scale2 = (dh ** -0.5) * _LOG2_E
    for h in range(heads):
        hs = slice(h * dh, (h + 1) * dh)
        q = q_ref[:, hs]
        scores = [lax.dot_general(q, k_ref[:, hs], _NT, preferred_element_type=F32) * scale2 + tab_ref[h]
                  for k_ref, tab_ref in zip(k_refs, tab_refs)]
        m = jnp.max(functools.reduce(jnp.maximum, scores), axis=-1, keepdims=True)
        es = [jnp.exp2(s - m) for s in scores]
        denom = jnp.sum(functools.reduce(jnp.add, es), axis=-1, keepdims=True)
        acc = functools.reduce(jnp.add, [
            jnp.dot(e.astype(BF16), v_ref[:, hs], preferred_element_type=F32) for e, v_ref in zip(es, v_refs)])
        out_ref[:, hs] = (acc / denom).astype(out_ref.dtype)


def _attention(proj, tab, d_attn, batch, seq, q_rows, cast_weights):
    t = proj.shape[0]
    dh = d_attn // ATTN_HEADS
    n_q = seq // q_rows
    n_steps = batch * n_q
    pad_blocks = (LEFT_CHUNKS * CHUNK) // q_rows
    n_pieces = 1 + pad_blocks
    assert tab.shape == (n_pieces + 1, ATTN_HEADS, q_rows, q_rows)
    q_col = proj.shape[1] // d_attn - 3

    def window_spec(col_block, p):
        return pl.BlockSpec(
            (q_rows, d_attn),
            lambda b, qi: (b * n_q + jnp.maximum(qi + p - pad_blocks, 0), col_block))

    def tab_spec(p):
        return pl.BlockSpec(
            (None, ATTN_HEADS, q_rows, q_rows),
            lambda b, qi: (jnp.where(qi + p >= pad_blocks, p, n_pieces), 0, 0, 0))

    def cast_spec(w):
        assert w.shape[0] % (n_steps * BF16_SUBLANES) == 0
        return pl.BlockSpec((w.shape[0] // n_steps, w.shape[1]), lambda b, qi: (b * n_q + qi, 0))

    cast_specs = [cast_spec(w) for w in cast_weights]
    outs = pl.pallas_call(
        functools.partial(_attn_kernel, heads=ATTN_HEADS, dh=dh, n_pieces=n_pieces, n_cast=len(cast_weights)),
        out_shape=(jax.ShapeDtypeStruct((t, d_attn), BF16),
                   *[jax.ShapeDtypeStruct(w.shape, BF16) for w in cast_weights]),
        grid=(batch, n_q),
        in_specs=[pl.BlockSpec((q_rows, d_attn), lambda b, qi: (b * n_q + qi, q_col))]
        + [window_spec(q_col + 1, p) for p in range(n_pieces)]
        + [window_spec(q_col + 2, p) for p in range(n_pieces)]
        + [tab_spec(p) for p in range(n_pieces)]
        + cast_specs,
        out_specs=(pl.BlockSpec((q_rows, d_attn), lambda b, qi: (b * n_q + qi, 0)), *cast_specs),
        compiler_params=_params(("arbitrary", "arbitrary"), 48),
        name="chunk_attn",
    )(proj, *([proj] * (2 * n_pieces)), *([tab] * n_pieces), *cast_weights)
    return outs[0], outs[1:]


def _bias_table(rel, q_rows):
    heads = rel.shape[0]
    pad_rows = LEFT_CHUNKS * CHUNK
    width = q_rows + pad_rows
    d_lo, d_hi = pad_rows - width + 1, pad_rows + q_rows - 1
    mid = rel[:, max(d_lo, -MAX_REL) + MAX_REL:min(d_hi, MAX_REL) + MAX_REL + 1]
    left = jnp.repeat(rel[:, :1], max(0, -MAX_REL - d_lo), axis=1)
    right = jnp.repeat(rel[:, -1:], max(0, d_hi - MAX_REL), axis=1)
    by_dist = jnp.concatenate([left, mid, right], axis=1)[:, ::-1]
    length = by_dist.shape[1]
    ring = jnp.pad(by_dist, ((0, 0), (0, 1)))
    skew = jnp.tile(ring, (1, q_rows))[:, :q_rows * length].reshape(heads, q_rows, length)
    bias = skew[:, :, q_rows - 1:q_rows - 1 + width]
    q_chunk = jnp.arange(q_rows)[:, None] // CHUNK
    k_chunk = jnp.arange(width)[None, :] // CHUNK
    in_band = (k_chunk >= q_chunk) & (k_chunk <= q_chunk + LEFT_CHUNKS)
    tab = jnp.where(in_band[None], bias * _LOG2_E, NEG_INF)
    pieces = [tab[:, :, p * q_rows:(p + 1) * q_rows] for p in range(width // q_rows)]
    pieces.append(jnp.full((heads, q_rows, q_rows), NEG_INF, F32))
    return jnp.stack(pieces, axis=0)


def _out_proj_kernel(a_ref, b_ref, wa_ref, wb_ref, x_ref, g_ref, o_ref, u_ref):
    acc = jnp.dot(a_ref[...], wa_ref[...], preferred_element_type=F32)
    acc = acc + jnp.dot(b_ref[...], wb_ref[...], preferred_element_type=F32)
    h = x_ref[...] + acc
    o_ref[...] = h
    u_ref[...] = _rmsnorm_f32(h, g_ref[...]).astype(BF16)


def _out_proj(h_a, h_b, w, x2, g_next, tm):
    t, d = x2.shape
    ka, kb = h_a.shape[1], h_b.shape[1]
    assert ka == kb
    return pl.pallas_call(
        _out_proj_kernel,
        out_shape=(jax.ShapeDtypeStruct((t, d), F32), jax.ShapeDtypeStruct((t, d), BF16)),
        grid=(t // tm,),
        in_specs=[
            pl.BlockSpec((tm, ka), lambda i: (i, 0)),
            pl.BlockSpec((tm, kb), lambda i: (i, 0)),
            pl.BlockSpec((None, ka, d), lambda i: (0, 0, 0)),
            pl.BlockSpec((None, kb, d), lambda i: (1, 0, 0)),
            pl.BlockSpec((tm, d), lambda i: (i, 0)),
            pl.BlockSpec((1, d), lambda i: (0, 0)),
        ],
        out_specs=(pl.BlockSpec((tm, d), lambda i: (i, 0)), pl.BlockSpec((tm, d), lambda i: (i, 0))),
        compiler_params=_params(("arbitrary",), 56),
        name="out_proj",
    )(h_a, h_b, w, w, x2, g_next)


def _mlp_kernel(h_ref, u_ref, w1_ref, w2_ref, gp_ref, *out_refs, final_norm):
    o_ref = out_refs[0]
    j = pl.program_id(1)

    @pl.when(j == 0)
    def _():
        o_ref[...] = h_ref[...]

    a = jnp.dot(u_ref[...], w1_ref[...], preferred_element_type=F32)
    a = jnp.square(jnp.maximum(a, 0.0)).astype(BF16)
    o_ref[...] += jnp.dot(a, w2_ref[...], preferred_element_type=F32)

    @pl.when(j == pl.num_programs(1) - 1)
    def _():
        normed = _rmsnorm_f32(o_ref[...], gp_ref[...])
        if final_norm:
            o_ref[...] = normed
        else:
            out_refs[1][...] = normed.astype(BF16)


def _mlp(h2, u, w1, w2, layer, g_post, final_norm, tm, tf):
    t, d = h2.shape
    f = w1.shape[2]
    row_tile = pl.BlockSpec((tm, d), lambda i, j: (i, 0))
    out_shape = [jax.ShapeDtypeStruct((t, d), F32)]
    if not final_norm:
        out_shape.append(jax.ShapeDtypeStruct((t, d), BF16))
    return pl.pallas_call(
        functools.partial(_mlp_kernel, final_norm=final_norm),
        out_shape=tuple(out_shape),
        grid=(t // tm, f // tf),
        in_specs=[
            row_tile,
            row_tile,
            pl.BlockSpec((None, d, tf), lambda i, j: (layer, 0, j)),
            pl.BlockSpec((None, tf, d), lambda i, j: (layer, j, 0)),
            pl.BlockSpec((1, d), lambda i, j: (0, 0)),
        ],
        out_specs=tuple([row_tile] * len(out_shape)),
        compiler_params=_params(("arbitrary", "arbitrary"), 62),
        name="mlp_final" if final_norm else "mlp",
    )(h2, u, w1, w2, g_post)


def _glu_kernel(u_ref, wa_ref, wg_ref, ba_ref, bg_ref, o_ref):
    u = u_ref[...]
    a = jnp.dot(u, wa_ref[...], preferred_element_type=F32) + ba_ref[...]
    gate = jnp.dot(u, wg_ref[...], preferred_element_type=F32) + bg_ref[...]
    o_ref[...] = (a * _sigmoid(gate)).astype(o_ref.dtype)


def _glu(u, pw1, pw1_b, tm, tn):
    t, d = u.shape
    n = pw1.shape[2] // 2
    nb = n // tn
    return pl.pallas_call(
        _glu_kernel,
        out_shape=jax.ShapeDtypeStruct((t, n), BF16),
        grid=(t // tm, nb),
        in_specs=[
            pl.BlockSpec((tm, d), lambda i, j: (i, 0)),
            pl.BlockSpec((None, d, tn), lambda i, j: (0, 0, j)),
            pl.BlockSpec((None, d, tn), lambda i, j: (0, 0, j + nb)),
            pl.BlockSpec((1, tn), lambda i, j: (0, j)),
            pl.BlockSpec((1, tn), lambda i, j: (0, j + nb)),
        ],
        out_specs=pl.BlockSpec((tm, tn), lambda i, j: (i, j)),
        compiler_params=_params(("arbitrary", "arbitrary"), 56),
        name="conv_glu",
    )(u, pw1, pw1, pw1_b, pw1_b)


_DW_HALO = 2 * BF16_SUBLANES
_DW_ROWS = 64


def _conv_tail_kernel(z_ref, halo_ref, dw_ref, dwb_ref, lng_ref, lnb_ref, w2_ref, b2_ref, h_ref, g_ref,
                      o_ref, u_ref, zs_ref, y_ref, *, tm, tiles_per_seq):
    i = pl.program_id(0)
    n_slabs = z_ref.shape[1] // LANES
    seq_start = (i % tiles_per_seq) == 0
    for s in range(n_slabs):
        lanes = slice(s * LANES, (s + 1) * LANES)
        zs_ref[s, 0:_DW_HALO, :] = jnp.where(seq_start, 0.0, halo_ref[:, lanes].astype(F32))
        zs_ref[s, _DW_HALO:, :] = z_ref[:, lanes].astype(F32)
    first_tap = _DW_HALO - (CONV_WIDTH - 1)

    def slab_body(s, carry):
        lanes = pl.ds(pl.multiple_of(s * LANES, LANES), LANES)
        for rb in range(tm // _DW_ROWS):
            r0 = rb * _DW_ROWS
            acc = jnp.broadcast_to(dwb_ref[:, lanes], (_DW_ROWS, LANES))
            for k in range(CONV_WIDTH):
                acc = acc + dw_ref[k:k + 1, lanes] * zs_ref[s, r0 + first_tap + k:r0 + first_tap + k + _DW_ROWS, :]
            y_ref[r0:r0 + _DW_ROWS, lanes] = acc
        return carry

    lax.fori_loop(0, n_slabs, slab_body, 0)
    y = y_ref[...]
    mu = jnp.mean(y, axis=-1, keepdims=True)
    yc = y - mu
    var = jnp.mean(yc * yc, axis=-1, keepdims=True)
    yn = yc * lax.rsqrt(var + EPS) * lng_ref[...] + lnb_ref[...]
    a = (yn * _sigmoid(yn)).astype(BF16)
    h = h_ref[...] + jnp.dot(a, w2_ref[...], preferred_element_type=F32) + b2_ref[...]
    o_ref[...] = h
    u_ref[...] = _rmsnorm_f32(h, g_ref[...]).astype(BF16)


def _conv_tail(z, dw_w, dw_b, ln_g, ln_b, pw2, pw2_b, h2, g_next, seq, tm):
    t, d = h2.shape
    halo_blocks = tm // _DW_HALO
    row_tile = pl.BlockSpec((tm, d), lambda i: (i, 0))
    vec = pl.BlockSpec((1, d), lambda i: (0, 0))
    return pl.pallas_call(
        functools.partial(_conv_tail_kernel, tm=tm, tiles_per_seq=seq // tm),
        out_shape=(jax.ShapeDtypeStruct((t, d), F32), jax.ShapeDtypeStruct((t, d), BF16)),
        grid=(t // tm,),
        in_specs=[
            row_tile,
            pl.BlockSpec((_DW_HALO, d), lambda i: (jnp.maximum(i * halo_blocks - 1, 0), 0)),
            pl.BlockSpec(dw_w.shape, lambda i: (0, 0)),
            vec, vec, vec,
            pl.BlockSpec((None, d, d), lambda i: (0, 0, 0), pipeline_mode=pl.Buffered(1)),
            vec,
            row_tile,
            vec,
        ],
        out_specs=(row_tile, row_tile),
        scratch_shapes=[pltpu.VMEM((d // LANES, _DW_HALO + tm, LANES), F32), pltpu.VMEM((tm, d), F32)],
        compiler_params=_params(("arbitrary",), 56),
        name="conv_tail",
    )(z, z, dw_w, dw_b, ln_g, ln_b, pw2, pw2_b, h2, g_next)


def _tiles(seq):
    def rows(want):
        return min(want, seq)

    return dict(
        repack_rows=512,
        in_proj_tm=rows(512),
        mlstm_chunks_per_step=min(4, seq // CHUNK),
        attn_q_rows=rows(256),
        out_proj_tm=rows(512),
        mlp_tm=rows(512), mlp_tf=2048,
        glu_tm=rows(1024), glu_tn=2048,
        conv_tm=rows(512),
    )


def _row(v):
    return v.reshape(1, -1).astype(F32)


def kernel(x, mixer_norm_g, mix_w_in, qk_conv_w, qk_conv_b, igate_b, fgate_b, mlstm_norm_g, rel_bias,
           mix_w_out, conv_pw1_w, conv_pw1_b, conv_dw_w, conv_dw_b, conv_ln_g, conv_ln_b, conv_pw2_w,
           conv_pw2_b, mlp_norm_g, mlp_w1, mlp_w2, final_norm_g):
    batch, seq, d = x.shape
    tokens = batch * seq
    n_chunks = seq // CHUNK
    dm = d // 2
    da = d - dm
    heads = MLSTM_HEADS
    tl = _tiles(seq)
    x2 = x.reshape(tokens, d)

    w_in_t = jnp.swapaxes(mix_w_in, 1, 2)
    gate_lo = 4 * dm
    gate_hi = gate_lo + 2 * heads
    w_main_t = _drop_rows_bf16(w_in_t, gate_lo, gate_hi, tl["repack_rows"])
    w_gate_t = jnp.pad(w_in_t[0, gate_lo:gate_hi], ((0, LANES - 2 * heads), (0, 0))).astype(BF16)
    proj, gates = _in_proj(x2, _row(mixer_norm_g[0]), w_main_t, w_gate_t, tl["in_proj_tm"])

    n_bh = batch * heads
    g8 = gates[:, :2 * heads].reshape(batch, n_chunks, CHUNK, 2, heads)
    g8 = jnp.transpose(g8, (3, 1, 0, 4, 2)).reshape(2, n_chunks * n_bh, CHUNK)
    g8 = jnp.pad(g8, ((0, 0), (0, 0), (0, LANES - CHUNK)))
    bias_rows = lambda bvec: jnp.broadcast_to(
        jnp.tile(bvec.astype(F32), batch * n_chunks)[:, None], (n_chunks * n_bh, LANES))
    r, mt, wi, et, ws, dc = _gate_prep(g8[0], g8[1], bias_rows(igate_b[0]), bias_rows(fgate_b[0]),
                                       n_chunks, n_bh)
    per_frame = jnp.stack([mt, wi, et, ws], axis=0)[:, :, :CHUNK]
    per_frame = per_frame.reshape(4, n_chunks, batch, heads, CHUNK)
    cols = jnp.transpose(per_frame, (2, 1, 4, 0, 3)).reshape(batch, n_chunks, CHUNK, 4 * heads)
    cols = jnp.pad(cols, ((0, 0), (0, 0), (0, 0), (0, LANES - 4 * heads)))
    rows = jnp.concatenate([r.reshape(n_chunks, batch, heads, LANES),
                            dc.reshape(n_chunks, batch, heads, LANES)], axis=2)
    rows = jnp.transpose(rows, (1, 0, 2, 3))

    later_weights = [mlp_w1, mlp_w2, mix_w_out, conv_pw1_w, conv_pw2_w]
    h_a, cast = _mlstm(proj, qk_conv_w[0].astype(F32), _row(qk_conv_b[0]), cols, rows,
                       _row(mlstm_norm_g[0]), batch, n_chunks, tl["mlstm_chunks_per_step"],
                       [w.reshape(-1, w.shape[2]) for w in later_weights])
    mlp_w1_bf, mlp_w2_bf, w_out_bf, pw1_bf, pw2_bf = [c.reshape(w.shape) for c, w in zip(cast, later_weights)]

    tab = _bias_table(rel_bias[0].astype(F32), tl["attn_q_rows"])
    h_b, _ = _attention(proj, tab, da, batch, seq, tl["attn_q_rows"], [])

    h, u = _out_proj(h_a, h_b, w_out_bf.reshape(2, dm, d), x2, _row(mlp_norm_g[0]), tl["out_proj_tm"])

    h, u = _mlp(h, u, mlp_w1_bf, mlp_w2_bf, 0, _row(mixer_norm_g[1]), False, tl["mlp_tm"], tl["mlp_tf"])

    z = _glu(u, pw1_bf, _row(conv_pw1_b[0]), tl["glu_tm"], tl["glu_tn"])
    h, u = _conv_tail(z, conv_dw_w[0].astype(F32), _row(conv_dw_b[0]), _row(conv_ln_g[0]),
                      _row(conv_ln_b[0]), pw2_bf, _row(conv_pw2_b[0]), h,
                      _row(mlp_norm_g[1]), seq, tl["conv_tm"])

    (out,) = _mlp(h, u, mlp_w1_bf, mlp_w2_bf, 1, _row(final_norm_g), True, tl["mlp_tm"], tl["mlp_tf"])
    return out.reshape(batch, seq, d)
```

```python
import functools
import math

import jax
import jax.numpy as jnp
from jax import lax
from jax.experimental import pallas as pl
from jax.experimental.pallas import tpu as pltpu

CHUNK = 64
MLSTM_HEADS = 4
MLSTM_CONV = 4
ATTN_HEADS = 8
LEFT_CHUNKS = 8
MAX_REL = 256
CONV_WIDTH = 31
EPS = 1e-6

V7X_VMEM_BYTES = 64 * 1024 * 1024
LANES = 128
BF16_SUBLANES = 16

F32 = jnp.float32
BF16 = jnp.bfloat16
NEG_INF = float("-inf")
_LOG2_E = math.log2(math.e)

_NT = (((1,), (1,)), ((), ()))
_TN = (((0,), (0,)), ((), ()))


def _params(semantics, vmem_mib):
    assert vmem_mib * 1024 * 1024 <= V7X_VMEM_BYTES
    return pltpu.CompilerParams(dimension_semantics=semantics,
                                vmem_limit_bytes=vmem_mib * 1024 * 1024)


def _rmsnorm_f32(x, g):
    return x * lax.rsqrt(jnp.mean(x * x, axis=-1, keepdims=True) + EPS) * g


def _sigmoid(x):
    return 0.5 * jnp.tanh(0.5 * x) + 0.5


def _drop_rows_kernel(w_ref, o_ref):
    o_ref[...] = w_ref[0].astype(BF16)


def _drop_rows_bf16(w_t, lo, hi, rows):
    n_in, d = w_t.shape[1:]
    n = n_in - (hi - lo)
    assert lo % rows == 0 and n % rows == 0 and (hi - lo) % 8 == 0
    return pl.pallas_call(
        _drop_rows_kernel,
        out_shape=jax.ShapeDtypeStruct((n, d), BF16),
        grid=(n // rows,),
        in_specs=[pl.BlockSpec((pl.Element(1), pl.Element(rows), pl.Element(d)),
                               lambda i: (0, pl.multiple_of(jnp.where(i * rows < lo, i * rows, i * rows + (hi - lo)), 8), 0))],
        out_specs=pl.BlockSpec((rows, d), lambda i: (i, 0)),
        compiler_params=_params(("arbitrary",), 32),
        name="w_in_repack",
    )(w_t)


def _in_proj_kernel(x_ref, g_ref, w_ref, wg_ref, o_ref, og_ref):
    u = _rmsnorm_f32(x_ref[...], g_ref[...]).astype(BF16)
    og_ref[...] = lax.dot_general(u, wg_ref[...], _NT, preferred_element_type=F32)
    o_ref[...] = lax.dot_general(u, w_ref[...], _NT, preferred_element_type=F32).astype(o_ref.dtype)


def _in_proj(x2, g, w_main_t, w_gate_t, tm):
    t, d = x2.shape
    n = w_main_t.shape[0]
    resident = pl.Buffered(1)
    return pl.pallas_call(
        _in_proj_kernel,
        out_shape=(jax.ShapeDtypeStruct((t, n), BF16), jax.ShapeDtypeStruct((t, LANES), F32)),
        grid=(t // tm,),
        in_specs=[
            pl.BlockSpec((tm, d), lambda i: (i, 0)),
            pl.BlockSpec((1, d), lambda i: (0, 0)),
            pl.BlockSpec((n, d), lambda i: (0, 0), pipeline_mode=resident),
            pl.BlockSpec((LANES, d), lambda i: (0, 0), pipeline_mode=resident),
        ],
        out_specs=(pl.BlockSpec((tm, n), lambda i: (i, 0)),
                   pl.BlockSpec((tm, LANES), lambda i: (i, 0))),
        compiler_params=_params(("arbitrary",), 62),
        name="in_proj",
    )(x2, g, w_main_t, w_gate_t)


def _lane_prefix(x, op, ident, lane):
    shift = 1
    while shift < CHUNK:
        y = pltpu.roll(x, shift, axis=1)
        x = op(x, jnp.where(lane >= shift, y, ident))
        shift *= 2
    return x


def _gate_prep_kernel(gi_ref, gf_ref, ib_ref, fb_ref,
                      r_ref, mt_ref, wi_ref, et_ref, ws_ref, dc_ref,
                      g_sc, c_sc, m_sc, *, n_chunks, n_bh):
    shape = gi_ref.shape
    lane = lax.broadcasted_iota(jnp.int32, shape, 1)
    valid = lane < CHUNK
    i_pre = gi_ref[...] + ib_ref[...]
    f_pre = gf_ref[...] + fb_ref[...]
    lf = jnp.minimum(f_pre, 0.0) - jnp.log1p(jnp.exp(-jnp.abs(f_pre)))
    lf = jnp.where(valid, lf, 0.0)
    b = _lane_prefix(lf, jnp.add, 0.0, lane)
    r = jnp.where(valid, i_pre - b, NEG_INF)
    cm = _lane_prefix(r, jnp.maximum, NEG_INF, lane)
    g_tot = jnp.sum(jnp.where(lane == CHUNK - 1, b, 0.0), axis=1, keepdims=True)
    c_last = jnp.max(r, axis=1, keepdims=True)
    g_sc[...] = jnp.broadcast_to(g_tot, shape)
    c_sc[...] = jnp.broadcast_to(c_last, shape)

    def body(c, m):
        rows = pl.ds(pl.multiple_of(c * n_bh, n_bh), n_bh)
        m_sc[rows, :] = m
        return g_sc[rows, :] + jnp.maximum(m, c_sc[rows, :])

    lax.fori_loop(0, n_chunks, body, jnp.zeros((n_bh, shape[1]), F32))
    m = m_sc[...]
    big_m = jnp.maximum(m, cm)
    m_last = jnp.maximum(m, c_sc[...])
    r_ref[...] = jnp.where(valid, r, 0.0)
    mt_ref[...] = big_m
    wi_ref[...] = jnp.exp(m - big_m)
    et_ref[...] = jnp.exp(-b - big_m)
    ws_ref[...] = jnp.exp(r - m_last)
    dc_ref[...] = jnp.exp(m - m_last)


def _gate_prep(gi, gf, ib, fb, n_chunks, n_bh):
    shape = jax.ShapeDtypeStruct(gi.shape, F32)
    return pl.pallas_call(
        functools.partial(_gate_prep_kernel, n_chunks=n_chunks, n_bh=n_bh),
        out_shape=(shape,) * 6,
        scratch_shapes=[pltpu.VMEM(gi.shape, F32)] * 3,
        compiler_params=_params(None, 32),
        name="gate_prep",
    )(gi, gf, ib, fb)


_QK_HALO = BF16_SUBLANES


def _mlstm_kernel(qk_ref, halo_ref, v_ref, o_ref, cw_ref, cb_ref, cols_ref, rows_ref, ng_ref, *refs,
                  heads, dh, chunks_per_step, n_cast):
    cast_in = refs[:n_cast]
    out_ref = refs[n_cast]
    cast_out = refs[n_cast + 1:2 * n_cast + 1]
    xs_ref, c_ref, n_ref = refs[2 * n_cast + 1:]
    for src, dst in zip(cast_in, cast_out):
        dst[...] = src[...].astype(BF16)
    group_id = pl.program_id(1)
    dm = heads * dh
    n_slabs = 2 * dm // LANES

    @pl.when(group_id == 0)
    def _():
        c_ref[...] = jnp.zeros(c_ref.shape, F32)
        n_ref[...] = jnp.zeros(n_ref.shape, F32)

    for s in range(n_slabs):
        lanes = slice(s * LANES, (s + 1) * LANES)
        xs_ref[s, 0:_QK_HALO, :] = jnp.where(group_id == 0, 0.0, halo_ref[:, lanes].astype(F32))
        xs_ref[s, _QK_HALO:, :] = qk_ref[:, lanes].astype(F32)

    def conv_silu(col0, row0):
        parts = []
        for s in range(col0 // LANES, (col0 + dh) // LANES):
            lanes = slice(s * LANES, (s + 1) * LANES)
            acc = cb_ref[:, lanes]
            for j in range(MLSTM_CONV):
                off = row0 + _QK_HALO - (MLSTM_CONV - 1) + j
                acc = acc + cw_ref[j:j + 1, lanes] * xs_ref[s, off:off + CHUNK, :]
            parts.append(acc * _sigmoid(acc))
        return jnp.concatenate(parts, axis=1)

    tri = (lax.broadcasted_iota(jnp.int32, (CHUNK, CHUNK), 0)
           >= lax.broadcasted_iota(jnp.int32, (CHUNK, CHUNK), 1))
    eye = (lax.broadcasted_iota(jnp.int32, (dh, dh), 0)
           == lax.broadcasted_iota(jnp.int32, (dh, dh), 1)).astype(BF16)
    for cc in range(chunks_per_step):
        row0 = cc * CHUNK
        rs = slice(row0, row0 + CHUNK)
        for h in range(heads):
            hs = slice(h * dh, (h + 1) * dh)
            q = conv_silu(h * dh, row0)
            k = conv_silu(dm + h * dh, row0) * (dh ** -0.5)
            qb = q.astype(BF16)
            kb = k.astype(BF16)
            v = v_ref[rs, hs]
            mt = cols_ref[cc, :, h:h + 1]
            wi = cols_ref[cc, :, heads + h:heads + h + 1]
            et = cols_ref[cc, :, 2 * heads + h:2 * heads + h + 1]
            ws = cols_ref[cc, :, 3 * heads + h:3 * heads + h + 1]
            r = rows_ref[cc, h:h + 1, 0:CHUNK]
            dc = rows_ref[cc, heads + h:heads + h + 1, 0:1]

            s = lax.dot_general(qb, kb, _NT, preferred_element_type=F32)
            sw = s * jnp.where(tri, jnp.exp(r - mt), 0.0)
            c_old = c_ref[h]
            inter = lax.dot_general(qb, c_old.astype(BF16), _NT, preferred_element_type=F32)
            intra = jnp.dot(sw.astype(BF16), v, preferred_element_type=F32)
            n_old = n_ref[h:h + 1, :]
            num = intra + wi * inter
            den = (jnp.sum(sw, axis=-1, keepdims=True)
                   + wi * jnp.sum(q * n_old, axis=-1, keepdims=True))
            hh = num / jnp.maximum(jnp.abs(den), et)
            hn = hh * lax.rsqrt(jnp.mean(hh * hh, axis=-1, keepdims=True) + EPS) * ng_ref[:, hs]
            out_ref[rs, hs] = (_sigmoid(o_ref[rs, hs].astype(F32)) * hn).astype(out_ref.dtype)

            vw = (v.astype(F32) * ws).astype(BF16)
            vw_t = lax.dot_general(eye, vw, _NT, preferred_element_type=F32).astype(BF16)
            upd = jnp.dot(vw_t, kb, preferred_element_type=F32)
            c_ref[h] = dc * c_old + upd
            n_ref[h:h + 1, :] = dc * n_old + jnp.sum(k * ws, axis=0, keepdims=True)


def _mlstm(proj, conv_w, conv_b, cols, rows, norm_g, batch, n_chunks, chunks_per_step, cast_weights):
    t = proj.shape[0]
    dm = norm_g.shape[1]
    dh = dm // MLSTM_HEADS
    n_groups = n_chunks // chunks_per_step
    rows_per_step = chunks_per_step * CHUNK
    halo_blocks = rows_per_step // _QK_HALO
    n_steps = batch * n_groups

    def cast_spec(w):
        assert w.shape[0] % (n_steps * BF16_SUBLANES) == 0
        return pl.BlockSpec((w.shape[0] // n_steps, w.shape[1]), lambda b, c: (b * n_groups + c, 0))

    cast_specs = [cast_spec(w) for w in cast_weights]
    outs = pl.pallas_call(
        functools.partial(_mlstm_kernel, heads=MLSTM_HEADS, dh=dh, chunks_per_step=chunks_per_step,
                          n_cast=len(cast_weights)),
        out_shape=(jax.ShapeDtypeStruct((t, dm), BF16),
                   *[jax.ShapeDtypeStruct(w.shape, BF16) for w in cast_weights]),
        grid=(batch, n_groups),
        in_specs=[
            pl.BlockSpec((rows_per_step, 2 * dm), lambda b, c: (b * n_groups + c, 0)),
            pl.BlockSpec((_QK_HALO, 2 * dm),
                         lambda b, c: (jnp.maximum((b * n_groups + c) * halo_blocks - 1, 0), 0)),
            pl.BlockSpec((rows_per_step, dm), lambda b, c: (b * n_groups + c, 2)),
            pl.BlockSpec((rows_per_step, dm), lambda b, c: (b * n_groups + c, 3)),
            pl.BlockSpec((MLSTM_CONV, 2 * dm), lambda b, c: (0, 0)),
            pl.BlockSpec((1, 2 * dm), lambda b, c: (0, 0)),
            pl.BlockSpec((None, chunks_per_step, CHUNK, LANES), lambda b, c: (b, c, 0, 0)),
            pl.BlockSpec((None, chunks_per_step, 2 * MLSTM_HEADS, LANES), lambda b, c: (b, c, 0, 0)),
            pl.BlockSpec((1, dm), lambda b, c: (0, 0)),
            *cast_specs,
        ],
        out_specs=(pl.BlockSpec((rows_per_step, dm), lambda b, c: (b * n_groups + c, 0)), *cast_specs),
        scratch_shapes=[
            pltpu.VMEM((2 * dm // LANES, _QK_HALO + rows_per_step, LANES), F32),
            pltpu.VMEM((MLSTM_HEADS, dh, dh), F32),
            pltpu.VMEM((2 * MLSTM_HEADS, dh), F32),
        ],
        compiler_params=_params(("arbitrary", "arbitrary"), 40),
        name="mlstm",
    )(proj, proj, proj, proj, conv_w, conv_b, cols, rows, norm_g, *cast_weights)
    return outs[0], outs[1:]


def _attn_kernel(*refs, heads, dh, n_pieces, n_cast):
    q_ref = refs[0]
    k_refs = refs[1:1 + n_pieces]
    v_refs = refs[1 + n_pieces:1 + 2 * n_pieces]
    tab_refs = refs[1 + 2 * n_pieces:1 + 3 * n_pieces]
    cast_in = refs[1 + 3 * n_pieces:1 + 3 * n_pieces + n_cast]
    out_ref = refs[1 + 3 * n_pieces + n_cast]
    cast_out = refs[2 + 3 * n_pieces + n_cast:]
    for src, dst in zip(cast_in, cast_out):
        dst[...] = src[...].astype(BF16)
    scale2 = (dh ** -0.5) * _LOG2_E
    for h in range(heads):
        hs = slice(h * dh, (h + 1) * dh)
        q = q_ref[:, hs]
        scores = [lax.dot_general(q, k_ref[:, hs], _NT, preferred_element_type=F32) * scale2 + tab_ref[h]
                  for k_ref, tab_ref in zip(k_refs, tab_refs)]
        m = jnp.max(functools.reduce(jnp.maximum, scores), axis=-1, keepdims=True)
        es = [jnp.exp2(s - m) for s in scores]
        denom = jnp.sum(functools.reduce(jnp.add, es), axis=-1, keepdims=True)
        acc = functools.reduce(jnp.add, [
            jnp.dot(e.astype(BF16), v_ref[:, hs], preferred_element_type=F32) for e, v_ref in zip(es, v_refs)])
        out_ref[:, hs] = (acc / denom).astype(out_ref.dtype)


def _attention(proj, tab, d_attn, batch, seq, q_rows, cast_weights):
    t = proj.shape[0]
    dh = d_attn // ATTN_HEADS
    n_q = seq // q_rows
    n_steps = batch * n_q
    pad_blocks = (LEFT_CHUNKS * CHUNK) // q_rows
    n_pieces = 1 + pad_blocks
    assert tab.shape == (n_pieces + 1, ATTN_HEADS, q_rows, q_rows)
    q_col = proj.shape[1] // d_attn - 3

    def window_spec(col_block, p):
        return pl.BlockSpec(
            (q_rows, d_attn),
            lambda b, qi: (b * n_q + jnp.maximum(qi + p - pad_blocks, 0), col_block))

    def tab_spec(p):
        return pl.BlockSpec(
            (None, ATTN_HEADS, q_rows, q_rows),
            lambda b, qi: (jnp.where(qi + p >= pad_blocks, p, n_pieces), 0, 0, 0))

    def cast_spec(w):
        assert w.shape[0] % (n_steps * BF16_SUBLANES) == 0
        return pl.BlockSpec((w.shape[0] // n_steps, w.shape[1]), lambda b, qi: (b * n_q + qi, 0))

    cast_specs = [cast_spec(w) for w in cast_weights]
    outs = pl.pallas_call(
        functools.partial(_attn_kernel, heads=ATTN_HEADS, dh=dh, n_pieces=n_pieces, n_cast=len(cast_weights)),
        out_shape=(jax.ShapeDtypeStruct((t, d_attn), BF16),
                   *[jax.ShapeDtypeStruct(w.shape, BF16) for w in cast_weights]),
        grid=(batch, n_q),
        in_specs=[pl.BlockSpec((q_rows, d_attn), lambda b, qi: (b * n_q + qi, q_col))]
        + [window_spec(q_col + 1, p) for p in range(n_pieces)]
        + [window_spec(q_col + 2, p) for p in range(n_pieces)]
        + [tab_spec(p) for p in range(n_pieces)]
        + cast_specs,
        out_specs=(pl.BlockSpec((q_rows, d_attn), lambda b, qi: (b * n_q + qi, 0)), *cast_specs),
        compiler_params=_params(("arbitrary", "arbitrary"), 48),
        name="chunk_attn",
    )(proj, *([proj] * (2 * n_pieces)), *([tab] * n_pieces), *cast_weights)
    return outs[0], outs[1:]


def _bias_table(rel, q_rows):
    heads = rel.shape[0]
    pad_rows = LEFT_CHUNKS * CHUNK
    width = q_rows + pad_rows
    d_lo, d_hi = pad_rows - width + 1, pad_rows + q_rows - 1
    mid = rel[:, max(d_lo, -MAX_REL) + MAX_REL:min(d_hi, MAX_REL) + MAX_REL + 1]
    left = jnp.repeat(rel[:, :1], max(0, -MAX_REL - d_lo), axis=1)
    right = jnp.repeat(rel[:, -1:], max(0, d_hi - MAX_REL), axis=1)
    by_dist = jnp.concatenate([left, mid, right], axis=1)[:, ::-1]
    length = by_dist.shape[1]
    ring = jnp.pad(by_dist, ((0, 0), (0, 1)))
    skew = jnp.tile(ring, (1, q_rows))[:, :q_rows * length].reshape(heads, q_rows, length)
    bias = skew[:, :, q_rows - 1:q_rows - 1 + width]
    q_chunk = jnp.arange(q_rows)[:, None] // CHUNK
    k_chunk = jnp.arange(width)[None, :] // CHUNK
    in_band = (k_chunk >= q_chunk) & (k_chunk <= q_chunk + LEFT_CHUNKS)
    tab = jnp.where(in_band[None], bias * _LOG2_E, NEG_INF)
    pieces = [tab[:, :, p * q_rows:(p + 1) * q_rows] for p in range(width // q_rows)]
    pieces.append(jnp.full((heads, q_rows, q_rows), NEG_INF, F32))
    return jnp.stack(pieces, axis=0)


def _out_proj_kernel(a_ref, b_ref, wa_ref, wb_ref, x_ref, g_ref, o_ref, u_ref):
    acc = jnp.dot(a_ref[...], wa_ref[...], preferred_element_type=F32)
    acc = acc + jnp.dot(b_ref[...], wb_ref[...], preferred_element_type=F32)
    h = x_ref[...] + acc
    o_ref[...] = h
    u_ref[...] = _rmsnorm_f32(h, g_ref[...]).astype(BF16)


def _out_proj(h_a, h_b, w, x2, g_next, tm):
    t, d = x2.shape
    ka, kb = h_a.shape[1], h_b.shape[1]
    assert ka == kb
    return pl.pallas_call(
        _out_proj_kernel,
        out_shape=(jax.ShapeDtypeStruct((t, d), F32), jax.ShapeDtypeStruct((t, d), BF16)),
        grid=(t // tm,),
        in_specs=[
            pl.BlockSpec((tm, ka), lambda i: (i, 0)),
            pl.BlockSpec((tm, kb), lambda i: (i, 0)),
            pl.BlockSpec((None, ka, d), lambda i: (0, 0, 0)),
            pl.BlockSpec((None, kb, d), lambda i: (1, 0, 0)),
            pl.BlockSpec((tm, d), lambda i: (i, 0)),
            pl.BlockSpec((1, d), lambda i: (0, 0)),
        ],
        out_specs=(pl.BlockSpec((tm, d), lambda i: (i, 0)), pl.BlockSpec((tm, d), lambda i: (i, 0))),
        compiler_params=_params(("arbitrary",), 56),
        name="out_proj",
    )(h_a, h_b, w, w, x2, g_next)


def _mlp_kernel(h_ref, u_ref, w1_ref, w2_ref, gp_ref, *out_refs, final_norm):
    o_ref = out_refs[0]
    j = pl.program_id(1)

    a = jnp.dot(u_ref[...], w1_ref[...], preferred_element_type=F32)
    a = jnp.square(jnp.maximum(a, 0.0)).astype(BF16)
    base = jnp.where(j == 0, h_ref[...], o_ref[...])
    o_ref[...] = base + jnp.dot(a, w2_ref[...], preferred_element_type=F32)

    @pl.when(j == pl.num_programs(1) - 1)
    def _():
        normed = _rmsnorm_f32(o_ref[...], gp_ref[...])
        if final_norm:
            o_ref[...] = normed
        else:
            out_refs[1][...] = normed.astype(BF16)


def _mlp(h2, u, w1, w2, layer, g_post, final_norm, tm, tf):
    t, d = h2.shape
    f = w1.shape[2]
    row_tile = pl.BlockSpec((tm, d), lambda i, j: (i, 0))
    out_shape = [jax.ShapeDtypeStruct((t, d), F32)]
    if not final_norm:
        out_shape.append(jax.ShapeDtypeStruct((t, d), BF16))
    return pl.pallas_call(
        functools.partial(_mlp_kernel, final_norm=final_norm),
        out_shape=tuple(out_shape),
        grid=(t // tm, f // tf),
        in_specs=[
            row_tile,
            row_tile,
            pl.BlockSpec((None, d, tf), lambda i, j: (layer, 0, j)),
            pl.BlockSpec((None, tf, d), lambda i, j: (layer, j, 0)),
            pl.BlockSpec((1, d), lambda i, j: (0, 0)),
        ],
        out_specs=tuple([row_tile] * len(out_shape)),
        compiler_params=_params(("arbitrary", "arbitrary"), 62),
        name="mlp_final" if final_norm else "mlp",
    )(h2, u, w1, w2, g_post)


def _glu_kernel(u_ref, wa_ref, wg_ref, ba_ref, bg_ref, o_ref):
    u = u_ref[...]
    a = jnp.dot(u, wa_ref[...], preferred_element_type=F32) + ba_ref[...]
    gate = jnp.dot(u, wg_ref[...], preferred_element_type=F32) + bg_ref[...]
    o_ref[...] = (a * _sigmoid(gate)).astype(o_ref.dtype)


def _glu(u, pw1, pw1_b, tm, tn):
    t, d = u.shape
    n = pw1.shape[2] // 2
    nb = n // tn
    return pl.pallas_call(
        _glu_kernel,
        out_shape=jax.ShapeDtypeStruct((t, n), BF16),
        grid=(t // tm, nb),
        in_specs=[
            pl.BlockSpec((tm, d), lambda i, j: (i, 0)),
            pl.BlockSpec((None, d, tn), lambda i, j: (0, 0, j)),
            pl.BlockSpec((None, d, tn), lambda i, j: (0, 0, j + nb)),
            pl.BlockSpec((1, tn), lambda i, j: (0, j)),
            pl.BlockSpec((1, tn), lambda i, j: (0, j + nb)),
        ],
        out_specs=pl.BlockSpec((tm, tn), lambda i, j: (i, j)),
        compiler_params=_params(("arbitrary", "arbitrary"), 56),
        name="conv_glu",
    )(u, pw1, pw1, pw1_b, pw1_b)


_DW_HALO = 2 * BF16_SUBLANES
_DW_ROWS = 64


def _conv_tail_kernel(z_ref, halo_ref, dw_ref, dwb_ref, lng_ref, lnb_ref, w2_ref, b2_ref, h_ref, g_ref,
                      o_ref, u_ref, zs_ref, y_ref, *, tm, tiles_per_seq):
    i = pl.program_id(0)
    n_slabs = z_ref.shape[1] // LANES
    seq_start = (i % tiles_per_seq) == 0
    for s in range(n_slabs):
        lanes = slice(s * LANES, (s + 1) * LANES)
        zs_ref[s, 0:_DW_HALO, :] = jnp.where(seq_start, 0.0, halo_ref[:, lanes].astype(F32))
        zs_ref[s, _DW_HALO:, :] = z_ref[:, lanes].astype(F32)
    first_tap = _DW_HALO - (CONV_WIDTH - 1)

    def slab_body(s, carry):
        lanes = pl.ds(pl.multiple_of(s * LANES, LANES), LANES)
        for rb in range(tm // _DW_ROWS):
            r0 = rb * _DW_ROWS
            acc = jnp.broadcast_to(dwb_ref[:, lanes], (_DW_ROWS, LANES))
            for k in range(CONV_WIDTH):
                acc = acc + dw_ref[k:k + 1, lanes] * zs_ref[s, r0 + first_tap + k:r0 + first_tap + k + _DW_ROWS, :]
            y_ref[r0:r0 + _DW_ROWS, lanes] = acc
        return carry

    lax.fori_loop(0, n_slabs, slab_body, 0)
    y = y_ref[...]
    mu = jnp.mean(y, axis=-1, keepdims=True)
    yc = y - mu
    var = jnp.mean(yc * yc, axis=-1, keepdims=True)
    yn = yc * lax.rsqrt(var + EPS) * lng_ref[...] + lnb_ref[...]
    a = (yn * _sigmoid(yn)).astype(BF16)
    h = h_ref[...] + jnp.dot(a, w2_ref[...], preferred_element_type=F32) + b2_ref[...]
    o_ref[...] = h
    u_ref[...] = _rmsnorm_f32(h, g_ref[...]).astype(BF16)


def _conv_tail(z, dw_w, dw_b, ln_g, ln_b, pw2, pw2_b, h2, g_next, seq, tm):
    t, d = h2.shape
    halo_blocks = tm // _DW_HALO
    row_tile = pl.BlockSpec((tm, d), lambda i: (i, 0))
    vec = pl.BlockSpec((1, d), lambda i: (0, 0))
    return pl.pallas_call(
        functools.partial(_conv_tail_kernel, tm=tm, tiles_per_seq=seq // tm),
        out_shape=(jax.ShapeDtypeStruct((t, d), F32), jax.ShapeDtypeStruct((t, d), BF16)),
        grid=(t // tm,),
        in_specs=[
            row_tile,
            pl.BlockSpec((_DW_HALO, d), lambda i: (jnp.maximum(i * halo_blocks - 1, 0), 0)),
            pl.BlockSpec(dw_w.shape, lambda i: (0, 0)),
            vec, vec, vec,
            pl.BlockSpec((None, d, d), lambda i: (0, 0, 0), pipeline_mode=pl.Buffered(1)),
            vec,
            row_tile,
            vec,
        ],
        out_specs=(row_tile, row_tile),
        scratch_shapes=[pltpu.VMEM((d // LANES, _DW_HALO + tm, LANES), F32), pltpu.VMEM((tm, d), F32)],
        compiler_params=_params(("arbitrary",), 56),
        name="conv_tail",
    )(z, z, dw_w, dw_b, ln_g, ln_b, pw2, pw2_b, h2, g_next)


def _tiles(seq):
    def rows(want):
        return min(want, seq)

    return dict(
        repack_rows=512,
        in_proj_tm=rows(512),
        mlstm_chunks_per_step=min(4, seq // CHUNK),
        attn_q_rows=rows(256),
        out_proj_tm=rows(512),
        mlp_tm=rows(512), mlp_tf=2048,
        glu_tm=rows(1024), glu_tn=2048,
        conv_tm=rows(512),
    )


def _row(v):
    return v.reshape(1, -1).astype(F32)


def kernel(x, mixer_norm_g, mix_w_in, qk_conv_w, qk_conv_b, igate_b, fgate_b, mlstm_norm_g, rel_bias,
           mix_w_out, conv_pw1_w, conv_pw1_b, conv_dw_w, conv_dw_b, conv_ln_g, conv_ln_b, conv_pw2_w,
           conv_pw2_b, mlp_norm_g, mlp_w1, mlp_w2, final_norm_g):
    batch, seq, d = x.shape
    tokens = batch * seq
    n_chunks = seq // CHUNK
    dm = d // 2
    da = d - dm
    heads = MLSTM_HEADS
    tl = _tiles(seq)
    x2 = x.reshape(tokens, d)

    w_in_t = jnp.swapaxes(mix_w_in, 1, 2)
    gate_lo = 4 * dm
    gate_hi = gate_lo + 2 * heads
    w_main_t = _drop_rows_bf16(w_in_t, gate_lo, gate_hi, tl["repack_rows"])
    w_gate_t = jnp.pad(w_in_t[0, gate_lo:gate_hi], ((0, LANES - 2 * heads), (0, 0))).astype(BF16)
    proj, gates = _in_proj(x2, _row(mixer_norm_g[0]), w_main_t, w_gate_t, tl["in_proj_tm"])

    n_bh = batch * heads
    g8 = gates[:, :2 * heads].reshape(batch, n_chunks, CHUNK, 2, heads)
    g8 = jnp.transpose(g8, (3, 1, 0, 4, 2)).reshape(2, n_chunks * n_bh, CHUNK)
    g8 = jnp.pad(g8, ((0, 0), (0, 0), (0, LANES - CHUNK)))
    bias_rows = lambda bvec: jnp.broadcast_to(
        jnp.tile(bvec.astype(F32), batch * n_chunks)[:, None], (n_chunks * n_bh, LANES))
    r, mt, wi, et, ws, dc = _gate_prep(g8[0], g8[1], bias_rows(igate_b[0]), bias_rows(fgate_b[0]),
                                       n_chunks, n_bh)
    per_frame = jnp.stack([mt, wi, et, ws], axis=0)[:, :, :CHUNK]
    per_frame = per_frame.reshape(4, n_chunks, batch, heads, CHUNK)
    cols = jnp.transpose(per_frame, (2, 1, 4, 0, 3)).reshape(batch, n_chunks, CHUNK, 4 * heads)
    cols = jnp.pad(cols, ((0, 0), (0, 0), (0, 0), (0, LANES - 4 * heads)))
    rows = jnp.concatenate([r.reshape(n_chunks, batch, heads, LANES),
                            dc.reshape(n_chunks, batch, heads, LANES)], axis=2)
    rows = jnp.transpose(rows, (1, 0, 2, 3))

    later_weights = [mlp_w1, mlp_w2, mix_w_out, conv_pw1_w, conv_pw2_w]
    h_a, cast = _mlstm(proj, qk_conv_w[0].astype(F32), _row(qk_conv_b[0]), cols, rows,
                       _row(mlstm_norm_g[0]), batch, n_chunks, tl["mlstm_chunks_per_step"],
                       [w.reshape(-1, w.shape[2]) for w in later_weights])
    mlp_w1_bf, mlp_w2_bf, w_out_bf, pw1_bf, pw2_bf = [c.reshape(w.shape) for c, w in zip(cast, later_weights)]

    tab = _bias_table(rel_bias[0].astype(F32), tl["attn_q_rows"])
    h_b, _ = _attention(proj, tab, da, batch, seq, tl["attn_q_rows"], [])

    h, u = _out_proj(h_a, h_b, w_out_bf.reshape(2, dm, d), x2, _row(mlp_norm_g[0]), tl["out_proj_tm"])

    h, u = _mlp(h, u, mlp_w1_bf, mlp_w2_bf, 0, _row(mixer_norm_g[1]), False, tl["mlp_tm"], tl["mlp_tf"])

    z = _glu(u, pw1_bf, _row(conv_pw1_b[0]), tl["glu_tm"], tl["glu_tn"])
    h, u = _conv_tail(z, conv_dw_w[0].astype(F32), _row(conv_dw_b[0]), _row(conv_ln_g[0]),
                      _row(conv_ln_b[0]), pw2_bf, _row(conv_pw2_b[0]), h,
                      _row(mlp_norm_g[1]), seq, tl["conv_tm"])

    (out,) = _mlp(h, u, mlp_w1_bf, mlp_w2_bf, 1, _row(final_norm_g), True, tl["mlp_tm"], tl["mlp_tf"])
    return out.reshape(batch, seq, d)
```
